```python
import math
import jax
import jax.numpy as jnp
from jax import lax
import numpy as np

D_MODEL = 1024
BATCH = 8
SEQ = 4096
DEPTH = 2

N_BRANCH = 3
BRANCH_WIDTH = D_MODEL // 2

DA_HEADS = 4
DA_HEAD_DIM = BRANCH_WIDTH // (2 * DA_HEADS)
DA_V_DIM = 2 * DA_HEAD_DIM
DA_QK = DA_HEADS * 2 * DA_HEAD_DIM
DA_V = DA_HEADS * DA_V_DIM
Q_BLOCK = 128
ALIBI_MAX_EXP = 8.0

S5_GROUP_SIZE = 16
S5_GROUPS = BRANCH_WIDTH // S5_GROUP_SIZE
S5_STATE = 64
S5_DT_MIN = 1e-3
S5_DT_MAX = 1e-1

GLA_HEADS = 4
GLA_DV = BRANCH_WIDTH // GLA_HEADS
GLA_DK = GLA_DV // 2
GLA_K = GLA_HEADS * GLA_DK
GLA_V = GLA_HEADS * GLA_DV
GLA_GATE_RANK = 16
GLA_TAU = 16.0
GLA_CHUNK = 64

SPLIT_SIZES = (DA_QK, DA_QK, DA_V, BRANCH_WIDTH, GLA_K, GLA_K, GLA_V, GLA_V, GLA_GATE_RANK, GLA_GATE_RANK, N_BRANCH * D_MODEL)
N_IN = sum(SPLIT_SIZES)

N_EXPERTS = 16
N_EXPERT_GROUPS = 4
EXPERTS_PER_GROUP = N_EXPERTS // N_EXPERT_GROUPS
TOP_K = 2
GROUP_SCORE_K = 2
EXPERT_FF = D_MODEL // 2

DN_ALPHA = (2.0 * DEPTH) ** 0.25
DN_BETA = (8.0 * DEPTH) ** -0.25
NORM_EPS = 1e-5

kernel_name = 'hybrid_diffattn_s5_gla_moe_encoder'


def layer_norm(x, g, b):
    xf = x.astype(jnp.float32)
    mu = jnp.mean(xf, axis=-1, keepdims=True)
    xc = xf - mu
    var = jnp.mean(xc * xc, axis=-1, keepdims=True)
    return (xc * lax.rsqrt(var + NORM_EPS) * g.astype(jnp.float32) + b.astype(jnp.float32)).astype(x.dtype)


def rms_norm(x, g):
    xf = x.astype(jnp.float32)
    ms = jnp.mean(xf * xf, axis=-1, keepdims=True)
    return (xf * lax.rsqrt(ms + NORM_EPS) * g.astype(jnp.float32)).astype(x.dtype)


def diff_attention(q, k, v, lam):
    b, s, h, _, dh = q.shape
    n_blk = s // Q_BLOCK
    scale = dh ** -0.5
    q_blocks = q.astype(jnp.float32).reshape(b, n_blk, Q_BLOCK, h, 2, dh).transpose(1, 0, 3, 4, 2, 5)
    k_t = k.astype(jnp.float32).transpose(0, 2, 3, 1, 4)
    v_t = v.astype(jnp.float32).transpose(0, 2, 1, 3)
    slopes = 2.0 ** (-ALIBI_MAX_EXP * jnp.arange(1, h + 1, dtype=jnp.float32) / h)
    pos_k = jnp.arange(s, dtype=jnp.int32)

    def block(args):
        q_blk, start = args
        logits = jnp.einsum('bhiqd,bhikd->bhiqk', q_blk, k_t) * scale
        pos_q = start + jnp.arange(Q_BLOCK, dtype=jnp.int32)
        dist = jnp.abs(pos_q[:, None] - pos_k[None, :]).astype(jnp.float32)
        alibi = -slopes[:, None, None] * dist
        probs = jax.nn.softmax(logits + alibi[None, :, None], axis=-1)
        weights = probs[:, :, 0] - lam * probs[:, :, 1]
        return jnp.einsum('bhqk,bhkv->bhqv', weights, v_t)

    starts = jnp.arange(n_blk, dtype=jnp.int32) * Q_BLOCK
    out = lax.map(block, (q_blocks, starts))
    return out.transpose(1, 0, 3, 2, 4).reshape(b, s, h, 2 * dh).astype(v.dtype)


def _ssm_combine(left, right):
    a_l, b_l = left
    a_r, b_r = right
    return a_r * a_l, a_r * b_l + b_r


def s5_bidirectional(u, a_re, a_im, log_dt, b_re, b_im, c_re, c_im, d_skip):
    bsz, s, w = u.shape
    uf = u.astype(jnp.float32).reshape(bsz, s, S5_GROUPS, S5_GROUP_SIZE)
    uc = uf.astype(jnp.complex64)
    y = uf * d_skip.astype(jnp.float32).reshape(S5_GROUPS, S5_GROUP_SIZE)
    for direction, rev in ((0, False), (1, True)):
        lam = lax.complex(a_re[direction].astype(jnp.float32), a_im[direction].astype(jnp.float32))
        dt = jnp.exp(log_dt[direction].astype(jnp.float32))[:, None]
        a_bar = jnp.exp(lam * dt)
        b_mat = lax.complex(b_re[direction].astype(jnp.float32), b_im[direction].astype(jnp.float32))
        b_bar = ((a_bar - 1.0) / lam)[..., None] * b_mat
        c_mat = lax.complex(c_re[direction].astype(jnp.float32), c_im[direction].astype(jnp.float32))
        bu = jnp.einsum('bsgc,gpc->bsgp', uc, b_bar)
        a_seq = jnp.broadcast_to(a_bar, (s,) + a_bar.shape)
        states = jax.vmap(lambda e: lax.associative_scan(_ssm_combine, (a_seq, e), reverse=rev)[1])(bu)
        y = y + jnp.real(jnp.einsum('bsgp,gcp->bsgc', states, c_mat))
    return y.reshape(bsz, s, w).astype(u.dtype)


def gla_chunked(q, k, v, log_a, inclusive):
    b, s, h, dk = q.shape
    dv = v.shape[-1]
    c = GLA_CHUNK
    n = s // c
    qf = q.astype(jnp.float32).reshape(b, n, c, h, dk)
    kf = k.astype(jnp.float32).reshape(b, n, c, h, dk)
    vf = v.astype(jnp.float32).reshape(b, n, c, h, dv)
    cum = jnp.cumsum(log_a.astype(jnp.float32).reshape(b, n, c, h, dk), axis=2)
    ref = cum[:, :, c // 2][:, :, None]
    last = cum[:, :, -1]
    scores = jnp.einsum('bnchd,bnshd->bnhcs', qf * jnp.exp(cum - ref), kf * jnp.exp(ref - cum))
    mask = jnp.tril(jnp.ones((c, c), dtype=bool), 0 if inclusive else -1)
    scores = jnp.where(mask, scores, 0.0)
    o_intra = jnp.einsum('bnhcs,bnshv->bnchv', scores, vf)
    d_state = jnp.einsum('bnchd,bnchv->bnhdv', kf * jnp.exp(last[:, :, None] - cum), vf)
    chunk_decay = jnp.exp(last)

    def step(state, inp):
        ds, dec = inp
        return dec[..., None] * state + ds, state

    init = jnp.zeros((b, h, dk, dv), jnp.float32)
    _, s_in = lax.scan(step, init, (jnp.moveaxis(d_state, 1, 0), jnp.moveaxis(chunk_decay, 1, 0)))
    s_in = jnp.moveaxis(s_in, 0, 1)
    o_inter = jnp.einsum('bnchd,bnhdv->bnchv', qf * jnp.exp(cum), s_in)
    return (o_intra + o_inter).reshape(b, s, h, dv)


def _flip_seq(t):
    return jnp.flip(t, axis=1)


def hybrid_mixer(x, layer, w_in, da_lambda, da_norm_g, s5_a_re, s5_a_im, s5_log_dt, s5_b_re, s5_b_im,
                 s5_c_re, s5_c_im, s5_d, s5_w_glu, s5_b_glu, gla_w_gate, gla_b_gate, gla_norm_g,
                 merge_w_up, merge_b, w_out):
    b, s, d = x.shape
    split_points = np.cumsum(SPLIT_SIZES)[:-1].tolist()
    (q_a, k_a, v_a, u_s5, q_g, k_g, v_g, r_g, z_f, z_b, gate_logits) = jnp.split(x @ w_in, split_points, axis=-1)

    lam_init = 0.8 - 0.6 * math.exp(-0.3 * layer)
    lv = da_lambda.astype(jnp.float32)
    lam = jnp.exp(jnp.sum(lv[0] * lv[1])) - jnp.exp(jnp.sum(lv[2] * lv[3])) + lam_init
    o_a = diff_attention(q_a.reshape(b, s, DA_HEADS, 2, DA_HEAD_DIM),
                         k_a.reshape(b, s, DA_HEADS, 2, DA_HEAD_DIM),
                         v_a.reshape(b, s, DA_HEADS, DA_V_DIM), lam)
    o_a = (rms_norm(o_a, da_norm_g) * (1.0 - lam_init)).reshape(b, s, BRANCH_WIDTH)

    y_s5 = jax.nn.gelu(s5_bidirectional(u_s5, s5_a_re, s5_a_im, s5_log_dt, s5_b_re, s5_b_im, s5_c_re, s5_c_im, s5_d))
    o_b = y_s5 * jax.nn.sigmoid(y_s5 @ s5_w_glu + s5_b_glu)

    qg = q_g.reshape(b, s, GLA_HEADS, GLA_DK) * (GLA_DK ** -0.5)
    kg = k_g.reshape(b, s, GLA_HEADS, GLA_DK)
    vg = v_g.reshape(b, s, GLA_HEADS, GLA_DV)
    log_f = (jax.nn.log_sigmoid((z_f @ gla_w_gate[0] + gla_b_gate[0]).astype(jnp.float32)) / GLA_TAU).reshape(b, s, GLA_HEADS, GLA_DK)
    log_b = (jax.nn.log_sigmoid((z_b @ gla_w_gate[1] + gla_b_gate[1]).astype(jnp.float32)) / GLA_TAU).reshape(b, s, GLA_HEADS, GLA_DK)
    o_fwd = gla_chunked(qg, kg, vg, log_f, True)
    o_bwd = _flip_seq(gla_chunked(_flip_seq(qg), _flip_seq(kg), _flip_seq(vg), _flip_seq(log_b), False))
    o_c = rms_norm((o_fwd + o_bwd).astype(x.dtype), gla_norm_g) * jax.nn.silu(r_g).reshape(b, s, GLA_HEADS, GLA_DV)
    o_c = o_c.reshape(b, s, BRANCH_WIDTH)

    gate_logits = gate_logits.reshape(b, s, N_BRANCH, d)
    merged = jnp.zeros_like(x)
    for n, o_n in enumerate((o_a, o_b, o_c)):
        gate = jax.nn.sigmoid(gate_logits[:, :, n] + merge_b[n])
        merged = merged + gate * (o_n @ merge_w_up[n])
    return merged @ w_out


def grouped_moe(x, router_w, router_bias, w_gate, w_up, w_down):
    b, s, d = x.shape
    t = x.reshape(b * s, d)
    scores = jax.nn.sigmoid((t @ router_w).astype(jnp.float32))
    biased = (scores + router_bias.astype(jnp.float32)).reshape(-1, N_EXPERT_GROUPS, EXPERTS_PER_GROUP)
    group_score = jnp.sum(lax.top_k(biased, GROUP_SCORE_K)[0], axis=-1)
    group_sel = jnp.argmax(group_score, axis=-1)
    in_group = jnp.take_along_axis(biased, group_sel[:, None, None], axis=1)[:, 0]
    _, local = lax.top_k(in_group, TOP_K)
    expert_idx = group_sel[:, None] * EXPERTS_PER_GROUP + local
    w_sel = jnp.take_along_axis(scores, expert_idx, axis=1)
    w_sel = w_sel / jnp.sum(w_sel, axis=-1, keepdims=True)
    combine = jnp.sum(jax.nn.one_hot(expert_idx, N_EXPERTS, dtype=jnp.float32) * w_sel[..., None], axis=1).astype(t.dtype)
    out = jnp.zeros_like(t)
    for e in range(N_EXPERTS):
        h = jax.nn.silu(t @ w_gate[e]) * (t @ w_up[e])
        out = out + combine[:, e:e + 1] * (h @ w_down[e])
    return out.reshape(b, s, d)


def setup_inputs(seed: int = 0) -> dict:
    key = jax.random.key(seed)
    ks = jax.random.split(key, 32)
    f32 = jnp.float32

    def nrm(i, shape, std):
        return jax.random.normal(ks[i], shape, f32) * std

    W, G, P, C = BRANCH_WIDTH, S5_GROUPS, S5_STATE, S5_GROUP_SIZE
    n_idx = jnp.arange(P, dtype=f32)
    return {
        'x': nrm(0, (BATCH, SEQ, D_MODEL), 1.0),
        'ln0_g': 1.0 + nrm(1, (D_MODEL,), 0.02),
        'ln0_b': nrm(2, (D_MODEL,), 0.02),
        'w_in': nrm(3, (DEPTH, D_MODEL, N_IN), D_MODEL ** -0.5),
        'da_lambda': nrm(4, (DEPTH, 4, DA_HEAD_DIM), 0.1),
        'da_norm_g': 1.0 + nrm(5, (DEPTH, DA_V_DIM), 0.02),
        's5_a_re': -0.5 + nrm(6, (DEPTH, 2, G, P), 0.01),
        's5_a_im': math.pi * n_idx + nrm(7, (DEPTH, 2, G, P), 0.01),
        's5_log_dt': jax.random.uniform(ks[8], (DEPTH, 2, G), f32, math.log(S5_DT_MIN), math.log(S5_DT_MAX)),
        's5_b_re': nrm(9, (DEPTH, 2, G, P, C), (2.0 * C) ** -0.5),
        's5_b_im': nrm(10, (DEPTH, 2, G, P, C), (2.0 * C) ** -0.5),
        's5_c_re': nrm(11, (DEPTH, 2, G, C, P), (2.0 * P) ** -0.5),
        's5_c_im': nrm(12, (DEPTH, 2, G, C, P), (2.0 * P) ** -0.5),
        's5_d': nrm(13, (DEPTH, W), 1.0),
        's5_w_glu': nrm(14, (DEPTH, W, W), W ** -0.5),
        's5_b_glu': nrm(15, (DEPTH, W), 0.02),
        'gla_w_gate': nrm(16, (DEPTH, 2, GLA_GATE_RANK, GLA_K), GLA_GATE_RANK ** -0.5),
        'gla_b_gate': nrm(17, (DEPTH, 2, GLA_K), 0.1),
        'gla_norm_g': 1.0 + nrm(18, (DEPTH, GLA_DV), 0.02),
        'merge_w_up': nrm(19, (DEPTH, N_BRANCH, W, D_MODEL), (W ** -0.5) * DN_BETA),
        'merge_b': nrm(20, (DEPTH, N_BRANCH, D_MODEL), 0.02),
        'w_out': nrm(21, (DEPTH, D_MODEL, D_MODEL), (D_MODEL ** -0.5) * DN_BETA),
        'ln1_g': 1.0 + nrm(22, (DEPTH, D_MODEL), 0.02),
        'ln1_b': nrm(23, (DEPTH, D_MODEL), 0.02),
        'router_w': nrm(24, (D_MODEL, N_EXPERTS), D_MODEL ** -0.5),
        'router_bias': nrm(25, (N_EXPERTS,), 0.01),
        'moe_w_gate': nrm(26, (DEPTH, N_EXPERTS, D_MODEL, EXPERT_FF), (D_MODEL ** -0.5) * DN_BETA),
        'moe_w_up': nrm(27, (DEPTH, N_EXPERTS, D_MODEL, EXPERT_FF), (D_MODEL ** -0.5) * DN_BETA),
        'moe_w_down': nrm(28, (DEPTH, N_EXPERTS, EXPERT_FF, D_MODEL), (EXPERT_FF ** -0.5) * DN_BETA),
        'ln2_g': 1.0 + nrm(29, (DEPTH, D_MODEL), 0.02),
        'ln2_b': nrm(30, (DEPTH, D_MODEL), 0.02),
    }


def reference(x, ln0_g, ln0_b, w_in, da_lambda, da_norm_g, s5_a_re, s5_a_im, s5_log_dt, s5_b_re, s5_b_im,
              s5_c_re, s5_c_im, s5_d, s5_w_glu, s5_b_glu, gla_w_gate, gla_b_gate, gla_norm_g, merge_w_up,
              merge_b, w_out, ln1_g, ln1_b, router_w, router_bias, moe_w_gate, moe_w_up, moe_w_down,
              ln2_g, ln2_b):
    x = layer_norm(x, ln0_g, ln0_b)
    for l in range(DEPTH):
        mix = hybrid_mixer(x, l, w_in[l], da_lambda[l], da_norm_g[l], s5_a_re[l], s5_a_im[l], s5_log_dt[l],
                           s5_b_re[l], s5_b_im[l], s5_c_re[l], s5_c_im[l], s5_d[l], s5_w_glu[l], s5_b_glu[l],
                           gla_w_gate[l], gla_b_gate[l], gla_norm_g[l], merge_w_up[l], merge_b[l], w_out[l])
        x = layer_norm(DN_ALPHA * x + mix, ln1_g[l], ln1_b[l])
        ffn = grouped_moe(x, router_w, router_bias, moe_w_gate[l], moe_w_up[l], moe_w_down[l])
        x = layer_norm(DN_ALPHA * x + ffn, ln2_g[l], ln2_b[l])
    return x
```

```python
import functools
import math

import jax
import jax.numpy as jnp
from jax import lax
from jax.experimental import pallas as pl
from jax.experimental.pallas import tpu as pltpu

F32 = jnp.float32
BF16 = jnp.bfloat16
HIGHEST = lax.Precision.HIGHEST

D_MODEL = 1024
DEPTH = 2
BRANCH_WIDTH = D_MODEL // 2
N_BRANCH = 3

DA_HEADS = 4
DA_HEAD_DIM = BRANCH_WIDTH // (2 * DA_HEADS)
DA_V_DIM = 2 * DA_HEAD_DIM
ALIBI_MAX_EXP = 8.0

S5_GROUP_SIZE = 16
S5_GROUPS = BRANCH_WIDTH // S5_GROUP_SIZE
S5_STATE = 64
S5_CHUNK = 16

GLA_HEADS = 4
GLA_DV = BRANCH_WIDTH // GLA_HEADS
GLA_DK = GLA_DV // 2
GLA_K = GLA_HEADS * GLA_DK
GLA_GATE_RANK = 16
GLA_TAU = 16.0
GLA_CHUNK = 64

N_EXPERTS = 16
EXPERTS_PER_GROUP = 4
EXPERT_FF = D_MODEL // 2

DN_ALPHA = (2.0 * DEPTH) ** 0.25
NORM_EPS = 1e-5
LOG2E = math.log2(math.e)

LANES = 128
PROJ_A = 3584
PROJ_B = 3200
VMEM_LIMIT = 56 * 1024 * 1024


def _params(n_grid_dims):
    return pltpu.CompilerParams(dimension_semantics=("arbitrary",) * n_grid_dims,
                                vmem_limit_bytes=VMEM_LIMIT)


def _layer_norm(xf, g, b):
    mu = jnp.mean(xf, axis=-1, keepdims=True)
    xc = xf - mu
    var = jnp.mean(xc * xc, axis=-1, keepdims=True)
    return xc * lax.rsqrt(var + NORM_EPS) * g + b


def _sigmoid(x):
    return 1.0 / (1.0 + jnp.exp(-x))


def _ln0_kernel(x_ref, g_ref, b_ref, of_ref, ob_ref):
    y = _layer_norm(x_ref[...], g_ref[...], b_ref[...])
    of_ref[...] = y
    ob_ref[...] = y.astype(BF16)


def _ln0(x2, g, b, tm=512):
    t, d = x2.shape
    row = pl.BlockSpec((tm, d), lambda i: (i, 0))
    vec = pl.BlockSpec((1, d), lambda i: (0, 0))
    return pl.pallas_call(
        _ln0_kernel, grid=(t // tm,), in_specs=[row, vec, vec], out_specs=[row, row],
        out_shape=[jax.ShapeDtypeStruct((t, d), F32), jax.ShapeDtypeStruct((t, d), BF16)],
        compiler_params=_params(1), name="ln0")(x2, g.reshape(1, d), b.reshape(1, d))


def _mm_kernel(x_ref, w_ref, o_ref):
    o_ref[...] = jnp.dot(x_ref[...], w_ref[...], preferred_element_type=F32).astype(o_ref.dtype)


def _matmul(x, w, tm, tn, name):
    t, k = x.shape
    n = w.shape[1]
    return pl.pallas_call(
        _mm_kernel, grid=(t // tm, n // tn),
        in_specs=[pl.BlockSpec((tm, k), lambda i, j: (i, 0)), pl.BlockSpec((k, tn), lambda i, j: (0, j))],
        out_specs=pl.BlockSpec((tm, tn), lambda i, j: (i, j)),
        out_shape=jax.ShapeDtypeStruct((t, n), F32),
        compiler_params=_params(2), name=name)(x, w)


def _attn_kernel(lam_ref, g_ref, q_ref, k_ref, v_ref, o_ref, kb_ref, vb_ref, *, tq, tk, seq, lam_init):
    h = pl.program_id(1)
    qi = pl.program_id(2)

    @pl.when(qi == 0)
    def _():
        kb_ref[...] = k_ref[0].astype(BF16)
        vb_ref[...] = v_ref[0].astype(BF16)

    slopes = [2.0 ** (-ALIBI_MAX_EXP * (i + 1) / DA_HEADS) for i in range(DA_HEADS)]
    slope = jnp.float32(slopes[-1])
    for i in range(DA_HEADS - 1):
        slope = jnp.where(h == i, jnp.float32(slopes[i]), slope)
    nslope = -slope * LOG2E

    q = q_ref[0] * (DA_HEAD_DIM ** -0.5 * LOG2E)
    lane = lax.broadcasted_iota(jnp.int32, (tq, 2 * DA_HEAD_DIM), 1)
    q1 = jnp.where(lane < DA_HEAD_DIM, q, 0.0).astype(BF16)
    q2 = jnp.where(lane >= DA_HEAD_DIM, q, 0.0).astype(BF16)
    qpos = (qi * tq + lax.broadcasted_iota(jnp.int32, (tq, 1), 0)).astype(F32)

    def body(j, carry):
        m1, l1, a1, m2, l2, a2 = carry
        off = pl.multiple_of(j * tk, tk)
        kb = kb_ref[pl.ds(off, tk), :]
        vb = vb_ref[pl.ds(off, tk), :]
        kpos = (off + lax.broadcasted_iota(jnp.int32, (1, tk), 1)).astype(F32)
        bias = jnp.abs(qpos - kpos) * nslope

        def update(qm, m, l, a):
            s = lax.dot_general(qm, kb, (((1,), (1,)), ((), ())), preferred_element_type=F32) + bias
            m_new = jnp.maximum(m, jnp.max(s, axis=-1, keepdims=True))
            alpha = jnp.exp2(m - m_new)
            p = jnp.exp2(s - m_new)
            l = alpha * l + jnp.sum(p, axis=-1, keepdims=True)
            a = alpha * a + jnp.dot(p.astype(BF16), vb, preferred_element_type=F32)
            return m_new, l, a

        m1, l1, a1 = update(q1, m1, l1, a1)
        m2, l2, a2 = update(q2, m2, l2, a2)
        return m1, l1, a1, m2, l2, a2

    m0 = jnp.full((tq, 1), -1e30, F32)
    l0 = jnp.zeros((tq, 1), F32)
    a0 = jnp.zeros((tq, DA_V_DIM), F32)
    _, l1, a1, _, l2, a2 = lax.fori_loop(0, seq // tk, body, (m0, l0, a0, m0, l0, a0))

    lv = lam_ref[...]
    lam = (jnp.exp(jnp.sum(lv[0:1] * lv[1:2], axis=-1, keepdims=True))
           - jnp.exp(jnp.sum(lv[2:3] * lv[3:4], axis=-1, keepdims=True)) + lam_init)
    o = a1 / l1 - lam * (a2 / l2)
    ms = jnp.mean(o * o, axis=-1, keepdims=True)
    o_ref[0] = o * lax.rsqrt(ms + NORM_EPS) * g_ref[...] * (1.0 - lam_init)


def _diff_attention(proj_a3, da_lambda, da_norm_g, lam_init, tq=128, tk=512):
    b, s, _ = proj_a3.shape
    kern = functools.partial(_attn_kernel, tq=tq, tk=tk, seq=s, lam_init=lam_init)
    return pl.pallas_call(
        kern, grid=(b, DA_HEADS, s // tq),
        in_specs=[
            pl.BlockSpec((4, DA_HEAD_DIM), lambda bi, h, i: (0, 0)),
            pl.BlockSpec((1, DA_V_DIM), lambda bi, h, i: (0, 0)),
            pl.BlockSpec((1, tq, LANES), lambda bi, h, i: (bi, i, h)),
            pl.BlockSpec((1, s, LANES), lambda bi, h, i: (bi, 0, DA_HEADS + h)),
            pl.BlockSpec((1, s, LANES), lambda bi, h, i: (bi, 0, 2 * DA_HEADS + h)),
        ],
        out_specs=pl.BlockSpec((1, tq, LANES), lambda bi, h, i: (bi, i, h)),
        out_shape=jax.ShapeDtypeStruct((b, s, BRANCH_WIDTH), F32),
        scratch_shapes=[pltpu.VMEM((s, LANES), BF16), pltpu.VMEM((s, LANES), BF16)],
        compiler_params=_params(3), name="diff_attention")(
            da_lambda, da_norm_g.reshape(1, DA_V_DIM), proj_a3, proj_a3, proj_a3)


def _s5_matrices(a_re, a_im, log_dt, b_re, b_im, c_re, c_im):
    L, C, P, G = S5_CHUNK, S5_GROUP_SIZE, S5_STATE, S5_GROUPS
    f = lambda t: t.astype(F32)
    a_re, a_im, b_re, b_im, c_re, c_im = map(f, (a_re, a_im, b_re, b_im, c_re, c_im))
    dt = jnp.exp(f(log_dt))[..., None]
    lr, li = a_re * dt, a_im * dt

    def apow(j):
        jj = j.astype(F32)[:, None, None, None]
        mag = jnp.exp(lr[None] * jj)
        return mag * jnp.cos(li[None] * jj), mag * jnp.sin(li[None] * jj)

    a1r, a1i = apow(jnp.arange(1, 2))
    nr, ni = a1r[0] - 1.0, a1i[0]
    den = a_re * a_re + a_im * a_im
    cr, ci = (nr * a_re + ni * a_im) / den, (ni * a_re - nr * a_im) / den
    bbr = cr[..., None] * b_re - ci[..., None] * b_im
    bbi = cr[..., None] * b_im + ci[..., None] * b_re

    pr, pi = apow(jnp.arange(L))
    wr = pr[..., None] * bbr[None] - pi[..., None] * bbi[None]
    wi = pr[..., None] * bbi[None] + pi[..., None] * bbr[None]
    kern = (jnp.einsum('dgcp,jdgpe->jdgce', c_re, wr, precision=HIGHEST)
            - jnp.einsum('dgcp,jdgpe->jdgce', c_im, wi, precision=HIGHEST))

    s_idx = jnp.arange(L)[:, None]
    t_idx = jnp.arange(L)[None, :]

    def toeplitz(kd, lag, valid):
        m = kd[jnp.clip(lag, 0, L - 1)] * valid[:, :, None, None, None].astype(F32)
        return m.transpose(2, 0, 4, 1, 3).reshape(G, L * C, L * C)

    m_mat = jnp.stack([toeplitz(kern[:, 0], t_idx - s_idx, t_idx >= s_idx),
                       toeplitz(kern[:, 1], s_idx - t_idx, s_idx >= t_idx)])

    def summary(w_dir):
        return w_dir.transpose(1, 0, 3, 2).reshape(G, L * C, P)

    p_re = jnp.stack([summary(wr[::-1, 0]), summary(wr[:, 1])])
    p_im = jnp.stack([summary(wi[::-1, 0]), summary(wi[:, 1])])

    qr, qi = apow(jnp.arange(1, L + 1))
    car = c_re[None] * qr[:, :, :, None, :] - c_im[None] * qi[:, :, :, None, :]
    cai = c_re[None] * qi[:, :, :, None, :] + c_im[None] * qr[:, :, :, None, :]

    def readout(ca_dir):
        return ca_dir.transpose(1, 3, 0, 2).reshape(G, P, L * C)

    q_re = jnp.stack([readout(car[:, 0]), readout(car[::-1, 1])])
    q_im = jnp.stack([-readout(cai[:, 0]), -readout(cai[::-1, 1])])

    half = jax.nn.one_hot(jnp.arange(G) % 2, 2, dtype=F32)
    p_pad = jnp.stack([p_re, p_im], axis=-2)[:, :, :, :, None, :] * half[None, :, None, None, :, None]
    p_pad = p_pad.reshape(2, G, L * C, 4 * P)
    q_pad = jnp.stack([q_re, q_im], axis=2)[:, :, :, None] * half[None, :, None, :, None, None]
    q_pad = q_pad.reshape(2, G, 4 * P, L * C)

    alr, ali = apow(jnp.arange(L, L + 1))
    a_l = jnp.concatenate([alr[0].reshape(2, G // 2, 2 * P), ali[0].reshape(2, G // 2, 2 * P)], axis=-1)
    return m_mat.astype(BF16), p_pad.astype(BF16), q_pad.astype(BF16), a_l.reshape(2, G // 2, 1, 4 * P)


def _s5_kernel(x_ref, m_ref, p_ref, q_ref, al_ref, d_ref, y_ref, s_ref, xin_ref, *, n_chunks, bsz):
    hp = 2 * S5_STATE
    xb = [x_ref[g].astype(BF16) for g in range(2)]
    for d in range(2):
        s_ref[d] = (jnp.dot(xb[0], p_ref[d, 0], preferred_element_type=F32)
                    + jnp.dot(xb[1], p_ref[d, 1], preferred_element_type=F32))

    al = [al_ref[d, 0] for d in range(2)]
    alr = [a[:, :hp] for a in al]
    ali = [a[:, hp:] for a in al]

    def body(i, carry):
        new = []
        for d in range(2):
            re, im = carry[2 * d], carry[2 * d + 1]
            n = i if d == 0 else n_chunks - 1 - i
            r0 = pl.multiple_of(n * bsz, bsz)
            xin_ref[d, pl.ds(r0, bsz), :hp] = re
            xin_ref[d, pl.ds(r0, bsz), hp:] = im
            s = s_ref[d, pl.ds(r0, bsz), :]
            new.append(alr[d] * re - ali[d] * im + s[:, :hp])
            new.append(alr[d] * im + ali[d] * re + s[:, hp:])
        return tuple(new)

    zero = jnp.zeros((bsz, hp), F32)
    lax.fori_loop(0, n_chunks, body, (zero, zero, zero, zero))

    xin = [xin_ref[d].astype(BF16) for d in range(2)]
    for g in range(2):
        y = x_ref[g] * d_ref[g]
        for d in range(2):
            y = y + jnp.dot(xb[g], m_ref[d, g], preferred_element_type=F32)
            y = y + jnp.dot(xin[d], q_ref[d, g], preferred_element_type=F32)
        y_ref[g] = y


def _s5(u3, mats, d_skip):
    b, s, w = u3.shape
    L, C, G = S5_CHUNK, S5_GROUP_SIZE, S5_GROUPS
    n = s // L
    lc = L * C
    m_mat, p_pad, q_pad, a_l = mats
    x = u3.reshape(b, n, L, G, C).transpose(3, 1, 0, 2, 4).reshape(G, n * b, lc)
    d_row = jnp.tile(d_skip.astype(F32).reshape(G, 1, C), (1, L, 1)).reshape(G, 1, lc)
    mat_spec = pl.BlockSpec((2, 2, lc, lc), lambda k: (0, k, 0, 0))
    kern = functools.partial(_s5_kernel, n_chunks=n, bsz=b)
    y = pl.pallas_call(
        kern, grid=(G // 2,),
        in_specs=[pl.BlockSpec((2, n * b, lc), lambda k: (k, 0, 0)), mat_spec, mat_spec, mat_spec,
                  pl.BlockSpec((2, 1, 1, lc), lambda k: (0, k, 0, 0)),
                  pl.BlockSpec((2, 1, lc), lambda k: (k, 0, 0))],
        out_specs=pl.BlockSpec((2, n * b, lc), lambda k: (k, 0, 0)),
        out_shape=jax.ShapeDtypeStruct((G, n * b, lc), F32),
        scratch_shapes=[pltpu.VMEM((2, n * b, lc), F32), pltpu.VMEM((2, n * b, lc), F32)],
        compiler_params=_params(1), name="s5")(x, m_mat, p_pad, q_pad, a_l, d_row)
    return y.reshape(G, n, b, L, C).transpose(2, 1, 3, 0, 4).reshape(b * s, w)


def _gla_kernel(qf_ref, kf_ref, vf_ref, zf_ref, qb_ref, kb_ref, vb_ref, zb_ref, wg_ref, bg_ref,
                of_ref, ob_ref, st_ref, *, rows):
    c = GLA_CHUNK

    @pl.when(pl.program_id(1) == 0)
    def _():
        st_ref[...] = jnp.zeros_like(st_ref)

    row = lax.broadcasted_iota(jnp.int32, (c, c), 0)
    col = lax.broadcasted_iota(jnp.int32, (c, c), 1)
    lane = lax.broadcasted_iota(jnp.int32, (1, GLA_K), 1)
    head_mask = [(lane >> int(math.log2(GLA_DK))) == h for h in range(GLA_HEADS)]
    nt = (((1,), (1,)), ((), ()))
    tn = (((0,), (0,)), ((), ()))

    for d, (q_ref, k_ref, v_ref, z_ref, o_ref) in enumerate(
            ((qf_ref, kf_ref, vf_ref, zf_ref, of_ref), (qb_ref, kb_ref, vb_ref, zb_ref, ob_ref))):
        gate = jnp.dot(z_ref[0], wg_ref[d], precision=HIGHEST, preferred_element_type=F32) + bg_ref[d]
        log_a = (jnp.minimum(gate, 0.0) - jnp.log(1.0 + jnp.exp(-jnp.abs(gate)))) * (1.0 / GLA_TAU)
        q = q_ref[0] * (GLA_DK ** -0.5)
        k = k_ref[0]
        v = v_ref[0]
        n_sub = rows // c
        for ci in (range(n_sub) if d == 0 else range(n_sub - 1, -1, -1)):
            sl = slice(ci * c, (ci + 1) * c)
            if d == 0:
                tri, keep, i_ref, i_last = (row >= col), (row >= col), c // 2, c - 1
            else:
                tri, keep, i_ref, i_last = (col >= row), (col > row), c - 1 - c // 2, 0
            cum = jnp.dot(tri.astype(F32), log_a[sl], precision=HIGHEST, preferred_element_type=F32)
            mid = cum[i_ref:i_ref + 1]
            last = cum[i_last:i_last + 1]
            qc, kc, vc = q[sl], k[sl], v[sl]
            qe = qc * jnp.exp(cum - mid)
            ke = (kc * jnp.exp(mid - cum)).astype(BF16)
            kd = kc * jnp.exp(last - cum)
            qs = qc * jnp.exp(cum)
            decay = jnp.exp(last)
            for h in range(GLA_HEADS):
                vh = vc[:, h * GLA_DV:(h + 1) * GLA_DV].astype(BF16)
                qeh = jnp.where(head_mask[h], qe, 0.0).astype(BF16)
                sc = lax.dot_general(qeh, ke, nt, preferred_element_type=F32)
                sc = jnp.where(keep, sc, 0.0)
                o = jnp.dot(sc.astype(BF16), vh, preferred_element_type=F32)
                st = st_ref[d, h]
                qsh = jnp.where(head_mask[h], qs, 0.0).astype(BF16)
                o = o + lax.dot_general(qsh, st.astype(BF16), nt, preferred_element_type=F32)
                o_ref[0, sl, h * GLA_DV:(h + 1) * GLA_DV] = o
                kdh = jnp.where(head_mask[h], kd, 0.0).astype(BF16)
                st_ref[d, h] = decay * st + lax.dot_general(vh, kdh, tn, preferred_element_type=F32)


def _gla(proj_a3, proj_b3, w_gate, b_gate, rows=128):
    b, s, _ = proj_a3.shape
    n = s // rows
    wg = jnp.zeros((2, LANES, GLA_K), F32)
    wg = wg.at[0, :GLA_GATE_RANK].set(w_gate[0].astype(F32))
    wg = wg.at[1, GLA_GATE_RANK:2 * GLA_GATE_RANK].set(w_gate[1].astype(F32))
    bg = b_gate.astype(F32).reshape(2, 1, GLA_K)

    def specs(rev):
        blk = (lambda i: n - 1 - i) if rev else (lambda i: i)
        return [pl.BlockSpec((1, rows, GLA_K), lambda bi, i: (bi, blk(i), 2048 // GLA_K)),
                pl.BlockSpec((1, rows, GLA_K), lambda bi, i: (bi, blk(i), 2304 // GLA_K)),
                pl.BlockSpec((1, rows, BRANCH_WIDTH), lambda bi, i: (bi, blk(i), 2560 // BRANCH_WIDTH)),
                pl.BlockSpec((1, rows, LANES), lambda bi, i: (bi, blk(i), 3072 // LANES))]

    out_f = pl.BlockSpec((1, rows, BRANCH_WIDTH), lambda bi, i: (bi, i, 0))
    out_b = pl.BlockSpec((1, rows, BRANCH_WIDTH), lambda bi, i: (bi, n - 1 - i, 0))
    shape = jax.ShapeDtypeStruct((b, s, BRANCH_WIDTH), F32)
    return pl.pallas_call(
        functools.partial(_gla_kernel, rows=rows), grid=(b, n),
        in_specs=specs(False) + specs(True) + [pl.BlockSpec((2, LANES, GLA_K), lambda bi, i: (0, 0, 0)),
                                               pl.BlockSpec((2, 1, GLA_K), lambda bi, i: (0, 0, 0))],
        out_specs=[out_f, out_b], out_shape=[shape, shape],
        scratch_shapes=[pltpu.VMEM((2, GLA_HEADS, GLA_DV, GLA_K), F32)],
        compiler_params=_params(2), name="gla")(
            proj_a3, proj_a3, proj_a3, proj_b3, proj_a3, proj_a3, proj_a3, proj_b3, wg, bg)


def _group_roll(x, k, group, lane):
    a = pltpu.roll(x, k, axis=1)
    b = pltpu.roll(x, (k - group) % LANES, axis=1)
    return jnp.where((lane & (group - 1)) >= k, a, b)


def _rank_in_group(x, stride, group, lane):
    rank = jnp.zeros(x.shape, F32)
    for m in range(1, group // stride):
        other = _group_roll(x, m * stride, group, lane)
        lower_index = (lane & (group - 1)) >= m * stride
        rank = rank + jnp.where(lower_index, jnp.where(other >= x, 1.0, 0.0), jnp.where(other > x, 1.0, 0.0))
    return rank


def _merge_kernel(x_ref, oa_ref, ys_ref, gf_ref, gb_ref, r_ref, gl0_ref, gl1_ref, gl2_ref,
                  wglu_ref, bglu_ref, gng_ref, wup_ref, mb_ref, wout_ref, lng_ref, lnb_ref, rw_ref, rb_ref,
                  xo_ref, xob_ref, comb_ref):
    y = ys_ref[...]
    gelu = 0.5 * y * (1.0 + jnp.tanh(math.sqrt(2.0 / math.pi) * (y + 0.044715 * (y * y * y))))
    o_b = gelu * _sigmoid(jnp.dot(gelu.astype(BF16), wglu_ref[...], preferred_element_type=F32) + bglu_ref[...])

    r = r_ref[...]
    gated = r * _sigmoid(r)
    heads = []
    for h in range(GLA_HEADS):
        sl = slice(h * GLA_DV, (h + 1) * GLA_DV)
        o = gf_ref[:, sl] + gb_ref[:, sl]
        ms = jnp.mean(o * o, axis=-1, keepdims=True)
        heads.append(o * lax.rsqrt(ms + NORM_EPS) * gng_ref[...] * gated[:, sl])
    o_c = jnp.concatenate(heads, axis=-1)

    merged = None
    for n, (o_n, gl_ref) in enumerate(((oa_ref[...], gl0_ref), (o_b, gl1_ref), (o_c, gl2_ref))):
        gate = _sigmoid(gl_ref[...] + mb_ref[n])
        term = gate * jnp.dot(o_n.astype(BF16), wup_ref[n], preferred_element_type=F32)
        merged = term if merged is None else merged + term
    mix = jnp.dot(merged.astype(BF16), wout_ref[...], preferred_element_type=F32)
    x1 = _layer_norm(DN_ALPHA * x_ref[...] + mix, lng_ref[...], lnb_ref[...])
    xo_ref[...] = x1
    xob_ref[...] = x1.astype(BF16)

    scores = _sigmoid(jnp.dot(x1, rw_ref[...], precision=HIGHEST, preferred_element_type=F32))
    lane = lax.broadcasted_iota(jnp.int32, scores.shape, 1)
    biased = scores + rb_ref[...]
    top2 = _rank_in_group(biased, 1, EXPERTS_PER_GROUP, lane) < 2.0
    kept = jnp.where(top2, biased, 0.0)
    group_score = kept
    for m in range(1, EXPERTS_PER_GROUP):
        group_score = group_score + _group_roll(kept, m, EXPERTS_PER_GROUP, lane)
    best_group = _rank_in_group(group_score, EXPERTS_PER_GROUP, N_EXPERTS, lane) < 1.0
    w_sel = jnp.where(top2, jnp.where(best_group, jnp.where(lane < N_EXPERTS, scores, 0.0), 0.0), 0.0)
    comb_ref[...] = w_sel / jnp.sum(w_sel, axis=-1, keepdims=True)


def _merge(x, o_a, y_s5, g_f, g_b, proj_a, proj_b, wglu, bglu, gng, wup, mb, wout, lng, lnb, rw, rb, tm=256):
    t, d = x.shape
    w = BRANCH_WIDTH
    row = lambda width, cb=0: pl.BlockSpec((tm, width), lambda i: (i, cb))
    full = lambda a: pl.BlockSpec(a.shape, lambda i: (0,) * a.ndim)
    weights = (wglu, bglu, gng, wup, mb, wout, lng, lnb, rw, rb)
    return pl.pallas_call(
        _merge_kernel, grid=(t // tm,),
        in_specs=[row(d), row(w), row(w), row(w), row(w), row(w, 3072 // w), row(d, 0), row(d, 1), row(d, 2)]
        + [full(a) for a in weights],
        out_specs=[row(d), row(d), row(LANES)],
        out_shape=[jax.ShapeDtypeStruct((t, d), F32), jax.ShapeDtypeStruct((t, d), BF16),
                   jax.ShapeDtypeStruct((t, LANES), F32)],
        compiler_params=_params(1), name="merge")(
            x, o_a, y_s5, g_f, g_b, proj_a, proj_b, proj_b, proj_b, *weights)


def _moe_kernel(xb_ref, xf_ref, comb_ref, wg_ref, wu_ref, wd_ref, lng_ref, lnb_ref, of_ref, ob_ref, acc_ref):
    e = pl.program_id(1)

    @pl.when(e == 0)
    def _():
        acc_ref[...] = jnp.zeros_like(acc_ref)

    xb = xb_ref[...]
    g = jnp.dot(xb, wg_ref[0], preferred_element_type=F32)
    u = jnp.dot(xb, wu_ref[0], preferred_element_type=F32)
    comb = comb_ref[...]
    lane = lax.broadcasted_iota(jnp.int32, comb.shape, 1)
    ce = jnp.sum(jnp.where(lane == e, comb, 0.0), axis=-1, keepdims=True)
    h = (g * _sigmoid(g)) * u * ce
    acc_ref[...] += jnp.dot(h.astype(BF16), wd_ref[0], preferred_element_type=F32)

    @pl.when(e == N_EXPERTS - 1)
    def _():
        y = _layer_norm(DN_ALPHA * xf_ref[...] + acc_ref[...], lng_ref[...], lnb_ref[...])
        of_ref[...] = y
        ob_ref[...] = y.astype(BF16)


def _moe(xb, xf, comb, wg, wu, wd, lng, lnb, tm=1024):
    t, d = xf.shape
    row = lambda width: pl.BlockSpec((tm, width), lambda i, e: (i, 0))
    vec = pl.BlockSpec((1, d), lambda i, e: (0, 0))
    return pl.pallas_call(
        _moe_kernel, grid=(t // tm, N_EXPERTS),
        in_specs=[row(d), row(d), row(LANES),
                  pl.BlockSpec((1, d, EXPERT_FF), lambda i, e: (e, 0, 0)),
                  pl.BlockSpec((1, d, EXPERT_FF), lambda i, e: (e, 0, 0)),
                  pl.BlockSpec((1, EXPERT_FF, d), lambda i, e: (e, 0, 0)), vec, vec],
        out_specs=[row(d), row(d)],
        out_shape=[jax.ShapeDtypeStruct((t, d), F32), jax.ShapeDtypeStruct((t, d), BF16)],
        scratch_shapes=[pltpu.VMEM((tm, d), F32)],
        compiler_params=_params(2), name="moe")(xb, xf, comb, wg, wu, wd, lng, lnb)


def kernel(x, ln0_g, ln0_b, w_in, da_lambda, da_norm_g, s5_a_re, s5_a_im, s5_log_dt, s5_b_re, s5_b_im,
           s5_c_re, s5_c_im, s5_d, s5_w_glu, s5_b_glu, gla_w_gate, gla_b_gate, gla_norm_g, merge_w_up,
           merge_b, w_out, ln1_g, ln1_b, router_w, router_bias, moe_w_gate, moe_w_up, moe_w_down,
           ln2_g, ln2_b):
    b, s, d = x.shape
    t = b * s
    vec = lambda a: a.astype(F32).reshape(1, -1)
    xf, xb = _ln0(x.reshape(t, d), ln0_g.astype(F32), ln0_b.astype(F32))
    rw = jnp.zeros((d, LANES), F32).at[:, :N_EXPERTS].set(router_w.astype(F32))
    rb = jnp.zeros((1, LANES), F32).at[0, :N_EXPERTS].set(router_bias.astype(F32))
    for l in range(DEPTH):
        w_a = w_in[l][:, :PROJ_A].astype(BF16)
        w_b = jnp.concatenate([w_in[l][:, PROJ_A + 2 * GLA_GATE_RANK:], w_in[l][:, PROJ_A:PROJ_A + 2 * GLA_GATE_RANK],
                               jnp.zeros((d, LANES - 2 * GLA_GATE_RANK), w_in.dtype)], axis=1).astype(BF16)
        proj_a = _matmul(xb, w_a, 1024, 512, "in_proj_a")
        proj_b = _matmul(xb, w_b, 1024, 640, "in_proj_b")
        proj_a3 = proj_a.reshape(b, s, PROJ_A)
        proj_b3 = proj_b.reshape(b, s, PROJ_B)

        lam_init = 0.8 - 0.6 * math.exp(-0.3 * l)
        o_a = _diff_attention(proj_a3, da_lambda[l].astype(F32), da_norm_g[l].astype(F32), lam_init)
        mats = _s5_matrices(s5_a_re[l], s5_a_im[l], s5_log_dt[l], s5_b_re[l], s5_b_im[l], s5_c_re[l], s5_c_im[l])
        y_s5 = _s5(proj_a3[:, :, 1536:2048], mats, s5_d[l])
        g_f, g_b = _gla(proj_a3, proj_b3, gla_w_gate[l], gla_b_gate[l])

        xf, xb, comb = _merge(
            xf, o_a.reshape(t, BRANCH_WIDTH), y_s5, g_f.reshape(t, BRANCH_WIDTH), g_b.reshape(t, BRANCH_WIDTH),
            proj_a, proj_b, s5_w_glu[l].astype(BF16), vec(s5_b_glu[l]), vec(gla_norm_g[l]),
            merge_w_up[l].astype(BF16), merge_b[l].astype(F32).reshape(N_BRANCH, 1, d), w_out[l].astype(BF16),
            vec(ln1_g[l]), vec(ln1_b[l]), rw, rb)
        xf, xb = _moe(xb, xf, comb, moe_w_gate[l].astype(BF16), moe_w_up[l].astype(BF16),
                      moe_w_down[l].astype(BF16), vec(ln2_g[l]), vec(ln2_b[l]))
    return xf.reshape(b, s, d)
```

```python
import functools
import math

import jax
import jax.numpy as jnp
from jax import lax
from jax.experimental import pallas as pl
from jax.experimental.pallas import tpu as pltpu

F32 = jnp.float32
BF16 = jnp.bfloat16
HIGHEST = lax.Precision.HIGHEST

D_MODEL = 1024
DEPTH = 2
BRANCH_WIDTH = D_MODEL // 2
N_BRANCH = 3

DA_HEADS = 4
DA_HEAD_DIM = BRANCH_WIDTH // (2 * DA_HEADS)
DA_V_DIM = 2 * DA_HEAD_DIM
ALIBI_MAX_EXP = 8.0

S5_GROUP_SIZE = 16
S5_GROUPS = BRANCH_WIDTH // S5_GROUP_SIZE
S5_STATE = 64
S5_CHUNK = 16

GLA_HEADS = 4
GLA_DV = BRANCH_WIDTH // GLA_HEADS
GLA_DK = GLA_DV // 2
GLA_K = GLA_HEADS * GLA_DK
GLA_GATE_RANK = 16
GLA_TAU = 16.0
GLA_CHUNK = 64

N_EXPERTS = 16
EXPERTS_PER_GROUP = 4
EXPERT_FF = D_MODEL // 2

DN_ALPHA = (2.0 * DEPTH) ** 0.25
NORM_EPS = 1e-5
LOG2E = math.log2(math.e)

LANES = 128
PROJ_A = 3584
PROJ_B = 3200
VMEM_LIMIT = 56 * 1024 * 1024


def _params(n_grid_dims):
    return pltpu.CompilerParams(dimension_semantics=("arbitrary",) * n_grid_dims,
                                vmem_limit_bytes=VMEM_LIMIT)


def _layer_norm(xf, g, b):
    mu = jnp.mean(xf, axis=-1, keepdims=True)
    xc = xf - mu
    var = jnp.mean(xc * xc, axis=-1, keepdims=True)
    return xc * lax.rsqrt(var + NORM_EPS) * g + b


def _sigmoid(x):
    return 1.0 / (1.0 + jnp.exp(-x))


def _ln0_kernel(x_ref, g_ref, b_ref, of_ref, ob_ref):
    y = _layer_norm(x_ref[...], g_ref[...], b_ref[...])
    of_ref[...] = y
    ob_ref[...] = y.astype(BF16)


def _ln0(x2, g, b, tm=512):
    t, d = x2.shape
    row = pl.BlockSpec((tm, d), lambda i: (i, 0))
    vec = pl.BlockSpec((1, d), lambda i: (0, 0))
    return pl.pallas_call(
        _ln0_kernel, grid=(t // tm,), in_specs=[row, vec, vec], out_specs=[row, row],
        out_shape=[jax.ShapeDtypeStruct((t, d), F32), jax.ShapeDtypeStruct((t, d), BF16)],
        compiler_params=_params(1), name="ln0")(x2, g.reshape(1, d), b.reshape(1, d))


def _mm_kernel(x_ref, w_ref, o_ref):
    o_ref[...] = jnp.dot(x_ref[...], w_ref[...], preferred_element_type=F32).astype(o_ref.dtype)


def _matmul(x, w, tm, tn, name):
    t, k = x.shape
    n = w.shape[1]
    return pl.pallas_call(
        _mm_kernel, grid=(t // tm, n // tn),
        in_specs=[pl.BlockSpec((tm, k), lambda i, j: (i, 0)), pl.BlockSpec((k, tn), lambda i, j: (0, j))],
        out_specs=pl.BlockSpec((tm, tn), lambda i, j: (i, j)),
        out_shape=jax.ShapeDtypeStruct((t, n), F32),
        compiler_params=_params(2), name=name)(x, w)


def _split3(x):
    hi = x.astype(BF16).astype(F32)
    mid = (x - hi).astype(BF16).astype(F32)
    lo = (x - hi - mid).astype(BF16).astype(F32)
    return hi, mid, lo


def _attn_kernel(lam_ref, g_ref, q_ref, k_ref, v_ref, o_ref, ka_ref, vb_ref, lhs_ref, fix_ref, m_ref, acc_ref,
                 *, tq, tk, rc, seq, lam_init):
    h = pl.program_id(1)
    qi = pl.program_id(2)
    dh = DA_HEAD_DIM

    slopes = [2.0 ** (-ALIBI_MAX_EXP * (i + 1) / DA_HEADS) for i in range(DA_HEADS)]
    slope = jnp.float32(slopes[-1])
    for i in range(DA_HEADS - 1):
        slope = jnp.where(h == i, jnp.float32(slopes[i]), slope)
    sigma = slope * LOG2E

    @pl.when(qi == 0)
    def _():
        ka_ref[:LANES, :] = k_ref[0].T.astype(BF16)
        t_hi, t_mid, t_lo = _split3(lax.broadcasted_iota(jnp.int32, (1, seq), 1).astype(F32) * sigma)
        sub_k = lax.broadcasted_iota(jnp.int32, (LANES, seq), 0)
        aug_k = jnp.where(sub_k < 3, 1.0, jnp.where(sub_k == 3, t_hi, jnp.where(sub_k == 4, t_mid,
                          jnp.where(sub_k == 5, t_lo, 0.0))))
        ka_ref[LANES:, :] = aug_k.astype(BF16)
        lane_v = lax.broadcasted_iota(jnp.int32, (seq, LANES), 1)
        vb_ref[:, :LANES] = v_ref[0].astype(BF16)
        vb_ref[:, LANES:] = jnp.where(lane_v == 0, 1.0, 0.0).astype(BF16)
        for r in range(tk // rc):
            delta = (r * rc + lax.broadcasted_iota(jnp.int32, (rc, 1), 0)
                     - lax.broadcasted_iota(jnp.int32, (1, tk), 1)).astype(F32)
            fix_ref[r] = (2.0 * sigma) * jnp.minimum(delta, 0.0)

    q = q_ref[0] * (dh ** -0.5 * LOG2E)
    lane = lax.broadcasted_iota(jnp.int32, (tq, LANES), 1)
    row = lax.broadcasted_iota(jnp.int32, (tq, 1), 0)
    u_hi, u_mid, u_lo = _split3((qi * tq + row).astype(F32) * (-sigma))
    aug_q = jnp.where(lane == 0, u_hi, jnp.where(lane == 1, u_mid, jnp.where(lane == 2, u_lo,
                      jnp.where(lane < 6, 1.0, 0.0))))
    for mp in range(2):
        qz = jnp.where(lane < dh, q, 0.0) if mp == 0 else jnp.where(lane >= dh, q, 0.0)
        lhs_ref[mp, 0] = jnp.concatenate([qz, aug_q], axis=-1).astype(BF16)
        lhs_ref[mp, 1] = jnp.concatenate([qz, -aug_q], axis=-1).astype(BF16)
    m_ref[...] = jnp.full(m_ref.shape, -1e30, F32)
    acc_ref[...] = jnp.zeros(acc_ref.shape, F32)

    def block(j, local):
        off = pl.multiple_of(j * tk, tk)
        ka = ka_ref[:, pl.ds(off, tk)]
        vb = vb_ref[pl.ds(off, tk), :]
        for mp in range(2):
            for c in range(tq // rc):
                rows = slice(c * rc, (c + 1) * rc)
                if isinstance(local, str):
                    sign, fixup = (0 if local == "before" else 1), None
                elif (c + 1) * rc <= local * tk:
                    sign, fixup = 1, None
                elif c * rc >= (local + 1) * tk:
                    sign, fixup = 0, None
                else:
                    sign, fixup = 0, fix_ref[(c * rc - local * tk) // rc]
                s = jnp.dot(lhs_ref[mp, sign, rows, :], ka, preferred_element_type=F32)
                if fixup is not None:
                    s = s + fixup
                m_old = m_ref[mp, rows, :]
                m_new = jnp.maximum(m_old, jnp.max(s, axis=-1, keepdims=True))
                alpha = jnp.exp2(m_old - m_new)
                p = jnp.exp2(s - m_new).astype(BF16)
                acc_ref[mp, rows, :] = alpha * acc_ref[mp, rows, :] + jnp.dot(p, vb, preferred_element_type=F32)
                m_ref[mp, rows, :] = m_new

    def run(lo, hi, where):
        def body(j, carry):
            block(j, where)
            return carry
        lax.fori_loop(lo, hi, body, 0)

    j0 = (qi * tq) // tk
    run(0, j0, "before")
    for local in range(tq // tk):
        block(j0 + local, local)
    run(j0 + tq // tk, seq // tk, "after")

    lv = lam_ref[...]
    lam = (jnp.exp(jnp.sum(lv[0:1] * lv[1:2], axis=-1, keepdims=True))
           - jnp.exp(jnp.sum(lv[2:3] * lv[3:4], axis=-1, keepdims=True)) + lam_init)
    o1 = acc_ref[0, :, :LANES] / acc_ref[0, :, LANES:LANES + 1]
    o2 = acc_ref[1, :, :LANES] / acc_ref[1, :, LANES:LANES + 1]
    o = o1 - lam * o2
    ms = jnp.mean(o * o, axis=-1, keepdims=True)
    o_ref[0] = o * lax.rsqrt(ms + NORM_EPS) * g_ref[...] * (1.0 - lam_init)


def _diff_attention(proj_a3, da_lambda, da_norm_g, lam_init, tq=1024, tk=512, rc=256):
    b, s, _ = proj_a3.shape
    assert tq % tk == 0 and s % tq == 0 and tk % rc == 0
    kern = functools.partial(_attn_kernel, tq=tq, tk=tk, rc=rc, seq=s, lam_init=lam_init)
    return pl.pallas_call(
        kern, grid=(b, DA_HEADS, s // tq),
        in_specs=[
            pl.BlockSpec((4, DA_HEAD_DIM), lambda bi, h, i: (0, 0)),
            pl.BlockSpec((1, DA_V_DIM), lambda bi, h, i: (0, 0)),
            pl.BlockSpec((1, tq, LANES), lambda bi, h, i: (bi, i, h)),
            pl.BlockSpec((1, s, LANES), lambda bi, h, i: (bi, 0, DA_HEADS + h)),
            pl.BlockSpec((1, s, LANES), lambda bi, h, i: (bi, 0, 2 * DA_HEADS + h)),
        ],
        out_specs=pl.BlockSpec((1, tq, LANES), lambda bi, h, i: (bi, i, h)),
        out_shape=jax.ShapeDtypeStruct((b, s, BRANCH_WIDTH), F32),
        scratch_shapes=[pltpu.VMEM((2 * LANES, s), BF16), pltpu.VMEM((s, 2 * LANES), BF16),
                        pltpu.VMEM((2, 2, tq, 2 * LANES), BF16), pltpu.VMEM((tk // rc, rc, tk), F32),
                        pltpu.VMEM((2, tq, 1), F32), pltpu.VMEM((2, tq, 2 * LANES), F32)],
        compiler_params=_params(3), name="diff_attention")(
            da_lambda, da_norm_g.reshape(1, DA_V_DIM), proj_a3, proj_a3, proj_a3)


def _s5_matrices(a_re, a_im, log_dt, b_re, b_im, c_re, c_im):
    L, C, P, G = S5_CHUNK, S5_GROUP_SIZE, S5_STATE, S5_GROUPS
    f = lambda t: t.astype(F32)
    a_re, a_im, b_re, b_im, c_re, c_im = map(f, (a_re, a_im, b_re, b_im, c_re, c_im))
    dt = jnp.exp(f(log_dt))[..., None]
    lr, li = a_re * dt, a_im * dt

    def apow(j):
        jj = j.astype(F32)[:, None, None, None]
        mag = jnp.exp(lr[None] * jj)
        return mag * jnp.cos(li[None] * jj), mag * jnp.sin(li[None] * jj)

    a1r, a1i = apow(jnp.arange(1, 2))
    nr, ni = a1r[0] - 1.0, a1i[0]
    den = a_re * a_re + a_im * a_im
    cr, ci = (nr * a_re + ni * a_im) / den, (ni * a_re - nr * a_im) / den
    bbr = cr[..., None] * b_re - ci[..., None] * b_im
    bbi = cr[..., None] * b_im + ci[..., None] * b_re

    pr, pi = apow(jnp.arange(L))
    wr = pr[..., None] * bbr[None] - pi[..., None] * bbi[None]
    wi = pr[..., None] * bbi[None] + pi[..., None] * bbr[None]
    kern = (jnp.einsum('dgcp,jdgpe->jdgce', c_re, wr, precision=HIGHEST)
            - jnp.einsum('dgcp,jdgpe->jdgce', c_im, wi, precision=HIGHEST))

    s_idx = jnp.arange(L)[:, None]
    t_idx = jnp.arange(L)[None, :]

    def toeplitz(kd, lag, valid):
        m = kd[jnp.clip(lag, 0, L - 1)] * valid[:, :, None, None, None].astype(F32)
        return m.transpose(2, 0, 4, 1, 3).reshape(G, L * C, L * C)

    m_mat = jnp.stack([toeplitz(kern[:, 0], t_idx - s_idx, t_idx >= s_idx),
                       toeplitz(kern[:, 1], s_idx - t_idx, s_idx >= t_idx)])

    def summary(w_dir):
        return w_dir.transpose(1, 0, 3, 2).reshape(G, L * C, P)

    p_re = jnp.stack([summary(wr[::-1, 0]), summary(wr[:, 1])])
    p_im = jnp.stack([summary(wi[::-1, 0]), summary(wi[:, 1])])

    qr, qi = apow(jnp.arange(1, L + 1))
    car = c_re[None] * qr[:, :, :, None, :] - c_im[None] * qi[:, :, :, None, :]
    cai = c_re[None] * qi[:, :, :, None, :] + c_im[None] * qr[:, :, :, None, :]

    def readout(ca_dir):
        return ca_dir.transpose(1, 3, 0, 2).reshape(G, P, L * C)

    q_re = jnp.stack([readout(car[:, 0]), readout(car[::-1, 1])])
    q_im = jnp.stack([-readout(cai[:, 0]), -readout(cai[::-1, 1])])

    half = jax.nn.one_hot(jnp.arange(G) % 2, 2, dtype=F32)
    p_pad = jnp.stack([p_re, p_im], axis=-2)[:, :, :, :, None, :] * half[None, :, None, None, :, None]
    p_pad = p_pad.reshape(2, G, L * C, 4 * P)
    q_pad = jnp.stack([q_re, q_im], axis=2)[:, :, :, None] * half[None, :, None, :, None, None]
    q_pad = q_pad.reshape(2, G, 4 * P, L * C)

    alr, ali = apow(jnp.arange(L, L + 1))
    a_l = jnp.concatenate([alr[0].reshape(2, G // 2, 2 * P), ali[0].reshape(2, G // 2, 2 * P)], axis=-1)
    return m_mat.astype(BF16), p_pad.astype(BF16), q_pad.astype(BF16), a_l.reshape(2, G // 2, 1, 4 * P)


def _s5_kernel(x_ref, m_ref, p_ref, q_ref, al_ref, d_ref, y_ref, s_ref, xin_ref, *, n_chunks, bsz):
    hp = 2 * S5_STATE
    xb = [x_ref[g].astype(BF16) for g in range(2)]
    for d in range(2):
        s_ref[d] = (jnp.dot(xb[0], p_ref[d, 0], preferred_element_type=F32)
                    + jnp.dot(xb[1], p_ref[d, 1], preferred_element_type=F32))

    al = [al_ref[d, 0] for d in range(2)]
    alr = [a[:, :hp] for a in al]
    ali = [a[:, hp:] for a in al]

    def body(i, carry):
        new = []
        for d in range(2):
            re, im = carry[2 * d], carry[2 * d + 1]
            n = i if d == 0 else n_chunks - 1 - i
            r0 = pl.multiple_of(n * bsz, bsz)
            xin_ref[d, pl.ds(r0, bsz), :hp] = re
            xin_ref[d, pl.ds(r0, bsz), hp:] = im
            s = s_ref[d, pl.ds(r0, bsz), :]
            new.append(alr[d] * re - ali[d] * im + s[:, :hp])
            new.append(alr[d] * im + ali[d] * re + s[:, hp:])
        return tuple(new)

    zero = jnp.zeros((bsz, hp), F32)
    lax.fori_loop(0, n_chunks, body, (zero, zero, zero, zero))

    xin = [xin_ref[d].astype(BF16) for d in range(2)]
    for g in range(2):
        y = x_ref[g] * d_ref[g]
        for d in range(2):
            y = y + jnp.dot(xb[g], m_ref[d, g], preferred_element_type=F32)
            y = y + jnp.dot(xin[d], q_ref[d, g], preferred_element_type=F32)
        y_ref[g] = y


def _s5(u3, mats, d_skip):
    b, s, w = u3.shape
    L, C, G = S5_CHUNK, S5_GROUP_SIZE, S5_GROUPS
    n = s // L
    lc = L * C
    m_mat, p_pad, q_pad, a_l = mats
    x = u3.reshape(b, n, L, G, C).transpose(3, 1, 0, 2, 4).reshape(G, n * b, lc)
    d_row = jnp.tile(d_skip.astype(F32).reshape(G, 1, C), (1, L, 1)).reshape(G, 1, lc)
    mat_spec = pl.BlockSpec((2, 2, lc, lc), lambda k: (0, k, 0, 0))
    kern = functools.partial(_s5_kernel, n_chunks=n, bsz=b)
    y = pl.pallas_call(
        kern, grid=(G // 2,),
        in_specs=[pl.BlockSpec((2, n * b, lc), lambda k: (k, 0, 0)), mat_spec, mat_spec, mat_spec,
                  pl.BlockSpec((2, 1, 1, lc), lambda k: (0, k, 0, 0)),
                  pl.BlockSpec((2, 1, lc), lambda k: (k, 0, 0))],
        out_specs=pl.BlockSpec((2, n * b, lc), lambda k: (k, 0, 0)),
        out_shape=jax.ShapeDtypeStruct((G, n * b, lc), F32),
        scratch_shapes=[pltpu.VMEM((2, n * b, lc), F32), pltpu.VMEM((2, n * b, lc), F32)],
        compiler_params=_params(1), name="s5")(x, m_mat, p_pad, q_pad, a_l, d_row)
    return y.reshape(G, n, b, L, C).transpose(2, 1, 3, 0, 4).reshape(b * s, w)


def _gla_kernel(qf_ref, kf_ref, vf_ref, zf_ref, qb_ref, kb_ref, vb_ref, zb_ref, wg_ref, bg_ref,
                of_ref, ob_ref, st_ref, *, rows):
    c = GLA_CHUNK

    @pl.when(pl.program_id(1) == 0)
    def _():
        st_ref[...] = jnp.zeros_like(st_ref)

    row = lax.broadcasted_iota(jnp.int32, (c, c), 0)
    col = lax.broadcasted_iota(jnp.int32, (c, c), 1)
    lane = lax.broadcasted_iota(jnp.int32, (1, GLA_K), 1)
    head_mask = [(lane >> int(math.log2(GLA_DK))) == h for h in range(GLA_HEADS)]
    nt = (((1,), (1,)), ((), ()))
    tn = (((0,), (0,)), ((), ()))

    for d, (q_ref, k_ref, v_ref, z_ref, o_ref) in enumerate(
            ((qf_ref, kf_ref, vf_ref, zf_ref, of_ref), (qb_ref, kb_ref, vb_ref, zb_ref, ob_ref))):
        gate = jnp.dot(z_ref[0], wg_ref[d], precision=HIGHEST, preferred_element_type=F32) + bg_ref[d]
        log_a = (jnp.minimum(gate, 0.0) - jnp.log(1.0 + jnp.exp(-jnp.abs(gate)))) * (1.0 / GLA_TAU)
        q = q_ref[0] * (GLA_DK ** -0.5)
        k = k_ref[0]
        v = v_ref[0]
        n_sub = rows // c
        for ci in (range(n_sub) if d == 0 else range(n_sub - 1, -1, -1)):
            sl = slice(ci * c, (ci + 1) * c)
            if d == 0:
                tri, keep, i_ref, i_last = (row >= col), (row >= col), c // 2, c - 1
            else:
                tri, keep, i_ref, i_last = (col >= row), (col > row), c - 1 - c // 2, 0
            cum = jnp.dot(tri.astype(F32), log_a[sl], precision=HIGHEST, preferred_element_type=F32)
            mid = cum[i_ref:i_ref + 1]
            last = cum[i_last:i_last + 1]
            qc, kc, vc = q[sl], k[sl], v[sl]
            qe = qc * jnp.exp(cum - mid)
            ke = (kc * jnp.exp(mid - cum)).astype(BF16)
            kd = kc * jnp.exp(last - cum)
            qs = qc * jnp.exp(cum)
            decay = jnp.exp(last)
            for h in range(GLA_HEADS):
                vh = vc[:, h * GLA_DV:(h + 1) * GLA_DV].astype(BF16)
                qeh = jnp.where(head_mask[h], qe, 0.0).astype(BF16)
                sc = lax.dot_general(qeh, ke, nt, preferred_element_type=F32)
                sc = jnp.where(keep, sc, 0.0)
                o = jnp.dot(sc.astype(BF16), vh, preferred_element_type=F32)
                st = st_ref[d, h]
                qsh = jnp.where(head_mask[h], qs, 0.0).astype(BF16)
                o = o + lax.dot_general(qsh, st.astype(BF16), nt, preferred_element_type=F32)
                o_ref[0, sl, h * GLA_DV:(h + 1) * GLA_DV] = o
                kdh = jnp.where(head_mask[h], kd, 0.0).astype(BF16)
                st_ref[d, h] = decay * st + lax.dot_general(vh, kdh, tn, preferred_element_type=F32)


def _gla(proj_a3, proj_b3, w_gate, b_gate, rows=128):
    b, s, _ = proj_a3.shape
    n = s // rows
    wg = jnp.zeros((2, LANES, GLA_K), F32)
    wg = wg.at[0, :GLA_GATE_RANK].set(w_gate[0].astype(F32))
    wg = wg.at[1, GLA_GATE_RANK:2 * GLA_GATE_RANK].set(w_gate[1].astype(F32))
    bg = b_gate.astype(F32).reshape(2, 1, GLA_K)

    def specs(rev):
        blk = (lambda i: n - 1 - i) if rev else (lambda i: i)
        return [pl.BlockSpec((1, rows, GLA_K), lambda bi, i: (bi, blk(i), 2048 // GLA_K)),
                pl.BlockSpec((1, rows, GLA_K), lambda bi, i: (bi, blk(i), 2304 // GLA_K)),
                pl.BlockSpec((1, rows, BRANCH_WIDTH), lambda bi, i: (bi, blk(i), 2560 // BRANCH_WIDTH)),
                pl.BlockSpec((1, rows, LANES), lambda bi, i: (bi, blk(i), 3072 // LANES))]

    out_f = pl.BlockSpec((1, rows, BRANCH_WIDTH), lambda bi, i: (bi, i, 0))
    out_b = pl.BlockSpec((1, rows, BRANCH_WIDTH), lambda bi, i: (bi, n - 1 - i, 0))
    shape = jax.ShapeDtypeStruct((b, s, BRANCH_WIDTH), F32)
    return pl.pallas_call(
        functools.partial(_gla_kernel, rows=rows), grid=(b, n),
        in_specs=specs(False) + specs(True) + [pl.BlockSpec((2, LANES, GLA_K), lambda bi, i: (0, 0, 0)),
                                               pl.BlockSpec((2, 1, GLA_K), lambda bi, i: (0, 0, 0))],
        out_specs=[out_f, out_b], out_shape=[shape, shape],
        scratch_shapes=[pltpu.VMEM((2, GLA_HEADS, GLA_DV, GLA_K), F32)],
        compiler_params=_params(2), name="gla")(
            proj_a3, proj_a3, proj_a3, proj_b3, proj_a3, proj_a3, proj_a3, proj_b3, wg, bg)


def _group_roll(x, k, group, lane):
    a = pltpu.roll(x, k, axis=1)
    b = pltpu.roll(x, (k - group) % LANES, axis=1)
    return jnp.where((lane & (group - 1)) >= k, a, b)


def _rank_in_group(x, stride, group, lane):
    rank = jnp.zeros(x.shape, F32)
    for m in range(1, group // stride):
        other = _group_roll(x, m * stride, group, lane)
        lower_index = (lane & (group - 1)) >= m * stride
        rank = rank + jnp.where(lower_index, jnp.where(other >= x, 1.0, 0.0), jnp.where(other > x, 1.0, 0.0))
    return rank


def _merge_kernel(x_ref, oa_ref, ys_ref, gf_ref, gb_ref, r_ref, gl0_ref, gl1_ref, gl2_ref,
                  wglu_ref, bglu_ref, gng_ref, wup_ref, mb_ref, wout_ref, lng_ref, lnb_ref, rw_ref, rb_ref,
                  xo_ref, xob_ref, comb_ref):
    y = ys_ref[...]
    gelu = 0.5 * y * (1.0 + jnp.tanh(math.sqrt(2.0 / math.pi) * (y + 0.044715 * (y * y * y))))
    o_b = gelu * _sigmoid(jnp.dot(gelu.astype(BF16), wglu_ref[...], preferred_element_type=F32) + bglu_ref[...])

    r = r_ref[...]
    gated = r * _sigmoid(r)
    heads = []
    for h in range(GLA_HEADS):
        sl = slice(h * GLA_DV, (h + 1) * GLA_DV)
        o = gf_ref[:, sl] + gb_ref[:, sl]
        ms = jnp.mean(o * o, axis=-1, keepdims=True)
        heads.append(o * lax.rsqrt(ms + NORM_EPS) * gng_ref[...] * gated[:, sl])
    o_c = jnp.concatenate(heads, axis=-1)

    merged = None
    for n, (o_n, gl_ref) in enumerate(((oa_ref[...], gl0_ref), (o_b, gl1_ref), (o_c, gl2_ref))):
        gate = _sigmoid(gl_ref[...] + mb_ref[n])
        term = gate * jnp.dot(o_n.astype(BF16), wup_ref[n], preferred_element_type=F32)
        merged = term if merged is None else merged + term
    mix = jnp.dot(merged.astype(BF16), wout_ref[...], preferred_element_type=F32)
    x1 = _layer_norm(DN_ALPHA * x_ref[...] + mix, lng_ref[...], lnb_ref[...])
    xo_ref[...] = x1
    xob_ref[...] = x1.astype(BF16)

    scores = _sigmoid(jnp.dot(x1, rw_ref[...], precision=HIGHEST, preferred_element_type=F32))
    lane = lax.broadcasted_iota(jnp.int32, scores.shape, 1)
    biased = scores + rb_ref[...]
    top2 = _rank_in_group(biased, 1, EXPERTS_PER_GROUP, lane) < 2.0
    kept = jnp.where(top2, biased, 0.0)
    group_score = kept
    for m in range(1, EXPERTS_PER_GROUP):
        group_score = group_score + _group_roll(kept, m, EXPERTS_PER_GROUP, lane)
    best_group = _rank_in_group(group_score, EXPERTS_PER_GROUP, N_EXPERTS, lane) < 1.0
    w_sel = jnp.where(top2, jnp.where(best_group, jnp.where(lane < N_EXPERTS, scores, 0.0), 0.0), 0.0)
    comb_ref[...] = w_sel / jnp.sum(w_sel, axis=-1, keepdims=True)


def _merge(x, o_a, y_s5, g_f, g_b, proj_a, proj_b, wglu, bglu, gng, wup, mb, wout, lng, lnb, rw, rb, tm=256):
    t, d = x.shape
    w = BRANCH_WIDTH
    row = lambda width, cb=0: pl.BlockSpec((tm, width), lambda i: (i, cb))
    full = lambda a: pl.BlockSpec(a.shape, lambda i: (0,) * a.ndim)
    weights = (wglu, bglu, gng, wup, mb, wout, lng, lnb, rw, rb)
    return pl.pallas_call(
        _merge_kernel, grid=(t // tm,),
        in_specs=[row(d), row(w), row(w), row(w), row(w), row(w, 3072 // w), row(d, 0), row(d, 1), row(d, 2)]
        + [full(a) for a in weights],
        out_specs=[row(d), row(d), row(LANES)],
        out_shape=[jax.ShapeDtypeStruct((t, d), F32), jax.ShapeDtypeStruct((t, d), BF16),
                   jax.ShapeDtypeStruct((t, LANES), F32)],
        compiler_params=_params(1), name="merge")(
            x, o_a, y_s5, g_f, g_b, proj_a, proj_b, proj_b, proj_b, *weights)


def _moe_kernel(xb_ref, xf_ref, comb_ref, wg_ref, wu_ref, wd_ref, lng_ref, lnb_ref, of_ref, ob_ref, acc_ref):
    e = pl.program_id(1)

    @pl.when(e == 0)
    def _():
        acc_ref[...] = jnp.zeros_like(acc_ref)

    xb = xb_ref[...]
    g = jnp.dot(xb, wg_ref[0], preferred_element_type=F32)
    u = jnp.dot(xb, wu_ref[0], preferred_element_type=F32)
    comb = comb_ref[...]
    lane = lax.broadcasted_iota(jnp.int32, comb.shape, 1)
    ce = jnp.sum(jnp.where(lane == e, comb, 0.0), axis=-1, keepdims=True)
    h = (g * _sigmoid(g)) * u * ce
    acc_ref[...] += jnp.dot(h.astype(BF16), wd_ref[0], preferred_element_type=F32)

    @pl.when(e == N_EXPERTS - 1)
    def _():
        y = _layer_norm(DN_ALPHA * xf_ref[...] + acc_ref[...], lng_ref[...], lnb_ref[...])
        of_ref[...] = y
        ob_ref[...] = y.astype(BF16)


def _moe(xb, xf, comb, wg, wu, wd, lng, lnb, tm=1024):
    t, d = xf.shape
    row = lambda width: pl.BlockSpec((tm, width), lambda i, e: (i, 0))
    vec = pl.BlockSpec((1, d), lambda i, e: (0, 0))
    return pl.pallas_call(
        _moe_kernel, grid=(t // tm, N_EXPERTS),
        in_specs=[row(d), row(d), row(LANES),
                  pl.BlockSpec((1, d, EXPERT_FF), lambda i, e: (e, 0, 0)),
                  pl.BlockSpec((1, d, EXPERT_FF), lambda i, e: (e, 0, 0)),
                  pl.BlockSpec((1, EXPERT_FF, d), lambda i, e: (e, 0, 0)), vec, vec],
        out_specs=[row(d), row(d)],
        out_shape=[jax.ShapeDtypeStruct((t, d), F32), jax.ShapeDtypeStruct((t, d), BF16)],
        scratch_shapes=[pltpu.VMEM((tm, d), F32)],
        compiler_params=_params(2), name="moe")(xb, xf, comb, wg, wu, wd, lng, lnb)


def kernel(x, ln0_g, ln0_b, w_in, da_lambda, da_norm_g, s5_a_re, s5_a_im, s5_log_dt, s5_b_re, s5_b_im,
           s5_c_re, s5_c_im, s5_d, s5_w_glu, s5_b_glu, gla_w_gate, gla_b_gate, gla_norm_g, merge_w_up,
           merge_b, w_out, ln1_g, ln1_b, router_w, router_bias, moe_w_gate, moe_w_up, moe_w_down,
           ln2_g, ln2_b):
    b, s, d = x.shape
    t = b * s
    vec = lambda a: a.astype(F32).reshape(1, -1)
    xf, xb = _ln0(x.reshape(t, d), ln0_g.astype(F32), ln0_b.astype(F32))
    rw = jnp.zeros((d, LANES), F32).at[:, :N_EXPERTS].set(router_w.astype(F32))
    rb = jnp.zeros((1, LANES), F32).at[0, :N_EXPERTS].set(router_bias.astype(F32))
    for l in range(DEPTH):
        w_a = w_in[l][:, :PROJ_A].astype(BF16)
        w_b = jnp.concatenate([w_in[l][:, PROJ_A + 2 * GLA_GATE_RANK:], w_in[l][:, PROJ_A:PROJ_A + 2 * GLA_GATE_RANK],
                               jnp.zeros((d, LANES - 2 * GLA_GATE_RANK), w_in.dtype)], axis=1).astype(BF16)
        proj_a = _matmul(xb, w_a, 1024, PROJ_A // 2, "in_proj_a")
        proj_b = _matmul(xb, w_b, 512, PROJ_B, "in_proj_b")
        proj_a3 = proj_a.reshape(b, s, PROJ_A)
        proj_b3 = proj_b.reshape(b, s, PROJ_B)

        lam_init = 0.8 - 0.6 * math.exp(-0.3 * l)
        o_a = _diff_attention(proj_a3, da_lambda[l].astype(F32), da_norm_g[l].astype(F32), lam_init)
        mats = _s5_matrices(s5_a_re[l], s5_a_im[l], s5_log_dt[l], s5_b_re[l], s5_b_im[l], s5_c_re[l], s5_c_im[l])
        y_s5 = _s5(proj_a3[:, :, 1536:2048], mats, s5_d[l])
        g_f, g_b = _gla(proj_a3, proj_b3, gla_w_gate[l], gla_b_gate[l])

        xf, xb, comb = _merge(
            xf, o_a.reshape(t, BRANCH_WIDTH), y_s5, g_f.reshape(t, BRANCH_WIDTH), g_b.reshape(t, BRANCH_WIDTH),
            proj_a, proj_b, s5_w_glu[l].astype(BF16), vec(s5_b_glu[l]), vec(gla_norm_g[l]),
            merge_w_up[l].astype(BF16), merge_b[l].astype(F32).reshape(N_BRANCH, 1, d), w_out[l].astype(BF16),
            vec(ln1_g[l]), vec(ln1_b[l]), rw, rb)
        xf, xb = _moe(xb, xf, comb, moe_w_gate[l].astype(BF16), moe_w_up[l].astype(BF16),
                      moe_w_down[l].astype(BF16), vec(ln2_g[l]), vec(ln2_b[l]))
    return xf.reshape(b, s, d)
```

```python
import functools
import math

import jax
import jax.numpy as jnp
from jax import lax
from jax.experimental import pallas as pl
from jax.experimental.pallas import tpu as pltpu

F32 = jnp.float32
BF16 = jnp.bfloat16
HIGHEST = lax.Precision.HIGHEST

D_MODEL = 1024
DEPTH = 2
BRANCH_WIDTH = D_MODEL // 2
N_BRANCH = 3

DA_HEADS = 4
DA_HEAD_DIM = BRANCH_WIDTH // (2 * DA_HEADS)
DA_V_DIM = 2 * DA_HEAD_DIM
ALIBI_MAX_EXP = 8.0

S5_GROUP_SIZE = 16
S5_GROUPS = BRANCH_WIDTH // S5_GROUP_SIZE
S5_STATE = 64
S5_CHUNK = 16

GLA_HEADS = 4
GLA_DV = BRANCH_WIDTH // GLA_HEADS
GLA_DK = GLA_DV // 2
GLA_K = GLA_HEADS * GLA_DK
GLA_GATE_RANK = 16
GLA_TAU = 16.0
GLA_CHUNK = 64

N_EXPERTS = 16
EXPERTS_PER_GROUP = 4
EXPERT_FF = D_MODEL // 2

DN_ALPHA = (2.0 * DEPTH) ** 0.25
NORM_EPS = 1e-5
LOG2E = math.log2(math.e)

LANES = 128
PROJ_A = 3584
PROJ_B = 3200
VMEM_LIMIT = 56 * 1024 * 1024


def _params(n_grid_dims):
    return pltpu.CompilerParams(dimension_semantics=("arbitrary",) * n_grid_dims,
                                vmem_limit_bytes=VMEM_LIMIT)


def _layer_norm(xf, g, b):
    mu = jnp.mean(xf, axis=-1, keepdims=True)
    xc = xf - mu
    var = jnp.mean(xc * xc, axis=-1, keepdims=True)
    return xc * lax.rsqrt(var + NORM_EPS) * g + b


def _sigmoid(x):
    return 1.0 / (1.0 + jnp.exp(-x))


def _ln0_kernel(x_ref, g_ref, b_ref, of_ref, ob_ref):
    y = _layer_norm(x_ref[...], g_ref[...], b_ref[...])
    of_ref[...] = y
    ob_ref[...] = y.astype(BF16)


def _ln0(x2, g, b, tm=512):
    t, d = x2.shape
    row = pl.BlockSpec((tm, d), lambda i: (i, 0))
    vec = pl.BlockSpec((1, d), lambda i: (0, 0))
    return pl.pallas_call(
        _ln0_kernel, grid=(t // tm,), in_specs=[row, vec, vec], out_specs=[row, row],
        out_shape=[jax.ShapeDtypeStruct((t, d), F32), jax.ShapeDtypeStruct((t, d), BF16)],
        compiler_params=_params(1), name="ln0")(x2, g.reshape(1, d), b.reshape(1, d))


def _mm_kernel(x_ref, w_ref, o_ref):
    o_ref[...] = jnp.dot(x_ref[...], w_ref[...], preferred_element_type=F32).astype(o_ref.dtype)


def _matmul(x, w, tm, tn, name):
    t, k = x.shape
    n = w.shape[1]
    return pl.pallas_call(
        _mm_kernel, grid=(t // tm, n // tn),
        in_specs=[pl.BlockSpec((tm, k), lambda i, j: (i, 0)), pl.BlockSpec((k, tn), lambda i, j: (0, j))],
        out_specs=pl.BlockSpec((tm, tn), lambda i, j: (i, j)),
        out_shape=jax.ShapeDtypeStruct((t, n), F32),
        compiler_params=_params(2), name=name)(x, w)


def _split3(x):
    hi = x.astype(BF16).astype(F32)
    mid = (x - hi).astype(BF16).astype(F32)
    lo = (x - hi - mid).astype(BF16).astype(F32)
    return hi, mid, lo


def _attn_kernel(lam_ref, g_ref, q_ref, k_ref, v_ref, o_ref, ka_ref, vb_ref, lhs_ref, fix_ref, m_ref, acc_ref,
                 *, tq, tk, rc, seq, lam_init):
    h = pl.program_id(1)
    qi = pl.program_id(2)
    dh = DA_HEAD_DIM

    slopes = [2.0 ** (-ALIBI_MAX_EXP * (i + 1) / DA_HEADS) for i in range(DA_HEADS)]
    slope = jnp.float32(slopes[-1])
    for i in range(DA_HEADS - 1):
        slope = jnp.where(h == i, jnp.float32(slopes[i]), slope)
    sigma = slope * LOG2E

    @pl.when(qi == 0)
    def _():
        ka_ref[:LANES, :] = k_ref[0].T.astype(BF16)
        t_hi, t_mid, t_lo = _split3(lax.broadcasted_iota(jnp.int32, (1, seq), 1).astype(F32) * sigma)
        sub_k = lax.broadcasted_iota(jnp.int32, (LANES, seq), 0)
        aug_k = jnp.where(sub_k < 3, 1.0, jnp.where(sub_k == 3, t_hi, jnp.where(sub_k == 4, t_mid,
                          jnp.where(sub_k == 5, t_lo, 0.0))))
        ka_ref[LANES:, :] = aug_k.astype(BF16)
        lane_v = lax.broadcasted_iota(jnp.int32, (seq, LANES), 1)
        vb_ref[:, :LANES] = v_ref[0].astype(BF16)
        vb_ref[:, LANES:] = jnp.where(lane_v == 0, 1.0, 0.0).astype(BF16)
        for r in range(tk // rc):
            delta = (r * rc + lax.broadcasted_iota(jnp.int32, (rc, 1), 0)
                     - lax.broadcasted_iota(jnp.int32, (1, tk), 1)).astype(F32)
            fix_ref[r] = (2.0 * sigma) * jnp.minimum(delta, 0.0)

    q = q_ref[0] * (dh ** -0.5 * LOG2E)
    lane = lax.broadcasted_iota(jnp.int32, (tq, LANES), 1)
    row = lax.broadcasted_iota(jnp.int32, (tq, 1), 0)
    u_hi, u_mid, u_lo = _split3((qi * tq + row).astype(F32) * (-sigma))
    aug_q = jnp.where(lane == 0, u_hi, jnp.where(lane == 1, u_mid, jnp.where(lane == 2, u_lo,
                      jnp.where(lane < 6, 1.0, 0.0))))
    for mp in range(2):
        qz = jnp.where(lane < dh, q, 0.0) if mp == 0 else jnp.where(lane >= dh, q, 0.0)
        lhs_ref[mp, 0] = jnp.concatenate([qz, aug_q], axis=-1).astype(BF16)
        lhs_ref[mp, 1] = jnp.concatenate([qz, -aug_q], axis=-1).astype(BF16)
    m_ref[...] = jnp.full(m_ref.shape, -1e30, F32)
    acc_ref[...] = jnp.zeros(acc_ref.shape, F32)

    def block(j, local):
        off = pl.multiple_of(j * tk, tk)
        ka = ka_ref[:, pl.ds(off, tk)]
        vb = vb_ref[pl.ds(off, tk), :]
        for mp in range(2):
            for c in range(tq // rc):
                rows = slice(c * rc, (c + 1) * rc)
                if isinstance(local, str):
                    sign, fixup = (0 if local == "before" else 1), None
                elif (c + 1) * rc <= local * tk:
                    sign, fixup = 1, None
                elif c * rc >= (local + 1) * tk:
                    sign, fixup = 0, None
                else:
                    sign, fixup = 0, fix_ref[(c * rc - local * tk) // rc]
                s = jnp.dot(lhs_ref[mp, sign, rows, :], ka, preferred_element_type=F32)
                if fixup is not None:
                    s = s + fixup
                m_old = m_ref[mp, rows, :]
                m_new = jnp.maximum(m_old, jnp.max(s, axis=-1, keepdims=True))
                alpha = jnp.exp2(m_old - m_new)
                p = jnp.exp2(s - m_new).astype(BF16)
                acc_ref[mp, rows, :] = alpha * acc_ref[mp, rows, :] + jnp.dot(p, vb, preferred_element_type=F32)
                m_ref[mp, rows, :] = m_new

    def run(lo, hi, where):
        def body(j, carry):
            block(j, where)
            return carry
        lax.fori_loop(lo, hi, body, 0)

    j0 = (qi * tq) // tk
    run(0, j0, "before")
    for local in range(tq // tk):
        block(j0 + local, local)
    run(j0 + tq // tk, seq // tk, "after")

    lv = lam_ref[...]
    lam = (jnp.exp(jnp.sum(lv[0:1] * lv[1:2], axis=-1, keepdims=True))
           - jnp.exp(jnp.sum(lv[2:3] * lv[3:4], axis=-1, keepdims=True)) + lam_init)
    o1 = acc_ref[0, :, :LANES] / acc_ref[0, :, LANES:LANES + 1]
    o2 = acc_ref[1, :, :LANES] / acc_ref[1, :, LANES:LANES + 1]
    o = o1 - lam * o2
    ms = jnp.mean(o * o, axis=-1, keepdims=True)
    o_ref[0] = o * lax.rsqrt(ms + NORM_EPS) * g_ref[...] * (1.0 - lam_init)


def _diff_attention(proj_a3, da_lambda, da_norm_g, lam_init, tq=1024, tk=512, rc=256):
    b, s, _ = proj_a3.shape
    assert tq % tk == 0 and s % tq == 0 and tk % rc == 0
    kern = functools.partial(_attn_kernel, tq=tq, tk=tk, rc=rc, seq=s, lam_init=lam_init)
    return pl.pallas_call(
        kern, grid=(b, DA_HEADS, s // tq),
        in_specs=[
            pl.BlockSpec((4, DA_HEAD_DIM), lambda bi, h, i: (0, 0)),
            pl.BlockSpec((1, DA_V_DIM), lambda bi, h, i: (0, 0)),
            pl.BlockSpec((1, tq, LANES), lambda bi, h, i: (bi, i, h)),
            pl.BlockSpec((1, s, LANES), lambda bi, h, i: (bi, 0, DA_HEADS + h)),
            pl.BlockSpec((1, s, LANES), lambda bi, h, i: (bi, 0, 2 * DA_HEADS + h)),
        ],
        out_specs=pl.BlockSpec((1, tq, LANES), lambda bi, h, i: (bi, i, h)),
        out_shape=jax.ShapeDtypeStruct((b, s, BRANCH_WIDTH), F32),
        scratch_shapes=[pltpu.VMEM((2 * LANES, s), BF16), pltpu.VMEM((s, 2 * LANES), BF16),
                        pltpu.VMEM((2, 2, tq, 2 * LANES), BF16), pltpu.VMEM((tk // rc, rc, tk), F32),
                        pltpu.VMEM((2, tq, 1), F32), pltpu.VMEM((2, tq, 2 * LANES), F32)],
        compiler_params=_params(3), name="diff_attention")(
            da_lambda, da_norm_g.reshape(1, DA_V_DIM), proj_a3, proj_a3, proj_a3)


def _s5_matrices(a_re, a_im, log_dt, b_re, b_im, c_re, c_im):
    L, C, P, G = S5_CHUNK, S5_GROUP_SIZE, S5_STATE, S5_GROUPS
    f = lambda t: t.astype(F32)
    a_re, a_im, b_re, b_im, c_re, c_im = map(f, (a_re, a_im, b_re, b_im, c_re, c_im))
    dt = jnp.exp(f(log_dt))[..., None]
    lr, li = a_re * dt, a_im * dt

    def apow(j):
        jj = j.astype(F32)[:, None, None, None]
        mag = jnp.exp(lr[None] * jj)
        return mag * jnp.cos(li[None] * jj), mag * jnp.sin(li[None] * jj)

    a1r, a1i = apow(jnp.arange(1, 2))
    nr, ni = a1r[0] - 1.0, a1i[0]
    den = a_re * a_re + a_im * a_im
    cr, ci = (nr * a_re + ni * a_im) / den, (ni * a_re - nr * a_im) / den
    bbr = cr[..., None] * b_re - ci[..., None] * b_im
    bbi = cr[..., None] * b_im + ci[..., None] * b_re

    pr, pi = apow(jnp.arange(L))
    wr = pr[..., None] * bbr[None] - pi[..., None] * bbi[None]
    wi = pr[..., None] * bbi[None] + pi[..., None] * bbr[None]
    kern = (jnp.einsum('dgcp,jdgpe->jdgce', c_re, wr, precision=HIGHEST)
            - jnp.einsum('dgcp,jdgpe->jdgce', c_im, wi, precision=HIGHEST))

    s_idx = jnp.arange(L)[:, None]
    t_idx = jnp.arange(L)[None, :]

    def toeplitz(kd, lag, valid):
        m = kd[jnp.clip(lag, 0, L - 1)] * valid[:, :, None, None, None].astype(F32)
        return m.transpose(2, 0, 4, 1, 3).reshape(G, L * C, L * C)

    m_mat = jnp.stack([toeplitz(kern[:, 0], t_idx - s_idx, t_idx >= s_idx),
                       toeplitz(kern[:, 1], s_idx - t_idx, s_idx >= t_idx)])

    def summary(w_dir):
        return w_dir.transpose(1, 0, 3, 2).reshape(G, L * C, P)

    p_re = jnp.stack([summary(wr[::-1, 0]), summary(wr[:, 1])])
    p_im = jnp.stack([summary(wi[::-1, 0]), summary(wi[:, 1])])

    qr, qi = apow(jnp.arange(1, L + 1))
    car = c_re[None] * qr[:, :, :, None, :] - c_im[None] * qi[:, :, :, None, :]
    cai = c_re[None] * qi[:, :, :, None, :] + c_im[None] * qr[:, :, :, None, :]

    def readout(ca_dir):
        return ca_dir.transpose(1, 3, 0, 2).reshape(G, P, L * C)

    q_re = jnp.stack([readout(car[:, 0]), readout(car[::-1, 1])])
    q_im = jnp.stack([-readout(cai[:, 0]), -readout(cai[::-1, 1])])

    half = jax.nn.one_hot(jnp.arange(G) % 2, 2, dtype=F32)
    p_pad = jnp.stack([p_re, p_im], axis=-2)[:, :, :, :, None, :] * half[None, :, None, None, :, None]
    p_pad = p_pad.reshape(2, G, L * C, 4 * P)
    q_pad = jnp.stack([q_re, q_im], axis=2)[:, :, :, None] * half[None, :, None, :, None, None]
    q_pad = q_pad.reshape(2, G, 4 * P, L * C)

    alr, ali = apow(jnp.arange(L, L + 1))
    a_l = jnp.concatenate([alr[0].reshape(2, G // 2, 2 * P), ali[0].reshape(2, G // 2, 2 * P)], axis=-1)
    return m_mat.astype(BF16), p_pad.astype(BF16), q_pad.astype(BF16), a_l.reshape(2, G // 2, 1, 4 * P)


def _s5_kernel(x_ref, m_ref, p_ref, q_ref, al_ref, d_ref, y_ref, s_ref, xin_ref, *, n_chunks, bsz):
    hp = 2 * S5_STATE
    xb = [x_ref[g].astype(BF16) for g in range(2)]
    for d in range(2):
        s_ref[d] = (jnp.dot(xb[0], p_ref[d, 0], preferred_element_type=F32)
                    + jnp.dot(xb[1], p_ref[d, 1], preferred_element_type=F32))

    al = [al_ref[d, 0] for d in range(2)]
    alr = [a[:, :hp] for a in al]
    ali = [a[:, hp:] for a in al]

    def body(i, carry):
        new = []
        for d in range(2):
            re, im = carry[2 * d], carry[2 * d + 1]
            n = i if d == 0 else n_chunks - 1 - i
            r0 = pl.multiple_of(n * bsz, bsz)
            xin_ref[d, pl.ds(r0, bsz), :hp] = re
            xin_ref[d, pl.ds(r0, bsz), hp:] = im
            s = s_ref[d, pl.ds(r0, bsz), :]
            new.append(alr[d] * re - ali[d] * im + s[:, :hp])
            new.append(alr[d] * im + ali[d] * re + s[:, hp:])
        return tuple(new)

    zero = jnp.zeros((bsz, hp), F32)
    lax.fori_loop(0, n_chunks, body, (zero, zero, zero, zero))

    xin = [xin_ref[d].astype(BF16) for d in range(2)]
    for g in range(2):
        y = x_ref[g] * d_ref[g]
        for d in range(2):
            y = y + jnp.dot(xb[g], m_ref[d, g], preferred_element_type=F32)
            y = y + jnp.dot(xin[d], q_ref[d, g], preferred_element_type=F32)
        y_ref[g] = y


def _s5(u3, mats, d_skip):
    b, s, w = u3.shape
    L, C, G = S5_CHUNK, S5_GROUP_SIZE, S5_GROUPS
    n = s // L
    lc = L * C
    m_mat, p_pad, q_pad, a_l = mats
    x = u3.reshape(b, n, L, G, C).transpose(3, 1, 0, 2, 4).reshape(G, n * b, lc)
    d_row = jnp.tile(d_skip.astype(F32).reshape(G, 1, C), (1, L, 1)).reshape(G, 1, lc)
    mat_spec = pl.BlockSpec((2, 2, lc, lc), lambda k: (0, k, 0, 0))
    kern = functools.partial(_s5_kernel, n_chunks=n, bsz=b)
    y = pl.pallas_call(
        kern, grid=(G // 2,),
        in_specs=[pl.BlockSpec((2, n * b, lc), lambda k: (k, 0, 0)), mat_spec, mat_spec, mat_spec,
                  pl.BlockSpec((2, 1, 1, lc), lambda k: (0, k, 0, 0)),
                  pl.BlockSpec((2, 1, lc), lambda k: (k, 0, 0))],
        out_specs=pl.BlockSpec((2, n * b, lc), lambda k: (k, 0, 0)),
        out_shape=jax.ShapeDtypeStruct((G, n * b, lc), F32),
        scratch_shapes=[pltpu.VMEM((2, n * b, lc), F32), pltpu.VMEM((2, n * b, lc), F32)],
        compiler_params=_params(1), name="s5")(x, m_mat, p_pad, q_pad, a_l, d_row)
    return y.reshape(G, n, b, L, C).transpose(2, 1, 3, 0, 4).reshape(b * s, w)


def _gla_kernel(qf_ref, kf_ref, vf_ref, zf_ref, qb_ref, kb_ref, vb_ref, zb_ref, wg_ref, bg_ref,
                of_ref, ob_ref, st_ref, *, rows):
    c = GLA_CHUNK

    @pl.when(pl.program_id(1) == 0)
    def _():
        st_ref[...] = jnp.zeros_like(st_ref)

    row = lax.broadcasted_iota(jnp.int32, (c, c), 0)
    col = lax.broadcasted_iota(jnp.int32, (c, c), 1)
    lane = lax.broadcasted_iota(jnp.int32, (1, GLA_K), 1)
    head_mask = [(lane >> int(math.log2(GLA_DK))) == h for h in range(GLA_HEADS)]
    nt = (((1,), (1,)), ((), ()))
    tn = (((0,), (0,)), ((), ()))

    for d, (q_ref, k_ref, v_ref, z_ref, o_ref) in enumerate(
            ((qf_ref, kf_ref, vf_ref, zf_ref, of_ref), (qb_ref, kb_ref, vb_ref, zb_ref, ob_ref))):
        gate = jnp.dot(z_ref[0], wg_ref[d], precision=HIGHEST, preferred_element_type=F32) + bg_ref[d]
        log_a = (jnp.minimum(gate, 0.0) - jnp.log(1.0 + jnp.exp(-jnp.abs(gate)))) * (1.0 / GLA_TAU)
        q = q_ref[0] * (GLA_DK ** -0.5)
        k = k_ref[0]
        v = v_ref[0]
        n_sub = rows // c
        for ci in (range(n_sub) if d == 0 else range(n_sub - 1, -1, -1)):
            sl = slice(ci * c, (ci + 1) * c)
            if d == 0:
                tri, keep, i_ref, i_last = (row >= col), (row >= col), c // 2, c - 1
            else:
                tri, keep, i_ref, i_last = (col >= row), (col > row), c - 1 - c // 2, 0
            cum = jnp.dot(tri.astype(F32), log_a[sl], precision=HIGHEST, preferred_element_type=F32)
            mid = cum[i_ref:i_ref + 1]
            last = cum[i_last:i_last + 1]
            qc, kc, vc = q[sl], k[sl], v[sl]
            qe = qc * jnp.exp(cum - mid)
            ke = (kc * jnp.exp(mid - cum)).astype(BF16)
            kd = kc * jnp.exp(last - cum)
            qs = qc * jnp.exp(cum)
            decay = jnp.exp(last)
            for h in range(GLA_HEADS):
                vh = vc[:, h * GLA_DV:(h + 1) * GLA_DV].astype(BF16)
                qeh = jnp.where(head_mask[h], qe, 0.0).astype(BF16)
                sc = lax.dot_general(qeh, ke, nt, preferred_element_type=F32)
                sc = jnp.where(keep, sc, 0.0)
                o = jnp.dot(sc.astype(BF16), vh, preferred_element_type=F32)
                st = st_ref[d, h]
                qsh = jnp.where(head_mask[h], qs, 0.0).astype(BF16)
                o = o + lax.dot_general(qsh, st.astype(BF16), nt, preferred_element_type=F32)
                o_ref[0, sl, h * GLA_DV:(h + 1) * GLA_DV] = o
                kdh = jnp.where(head_mask[h], kd, 0.0).astype(BF16)
                st_ref[d, h] = decay * st + lax.dot_general(vh, kdh, tn, preferred_element_type=F32)


def _gla(proj_a3, proj_b3, w_gate, b_gate, rows=128):
    b, s, _ = proj_a3.shape
    n = s // rows
    wg = jnp.zeros((2, LANES, GLA_K), F32)
    wg = wg.at[0, :GLA_GATE_RANK].set(w_gate[0].astype(F32))
    wg = wg.at[1, GLA_GATE_RANK:2 * GLA_GATE_RANK].set(w_gate[1].astype(F32))
    bg = b_gate.astype(F32).reshape(2, 1, GLA_K)

    def specs(rev):
        blk = (lambda i: n - 1 - i) if rev else (lambda i: i)
        return [pl.BlockSpec((1, rows, GLA_K), lambda bi, i: (bi, blk(i), 2048 // GLA_K)),
                pl.BlockSpec((1, rows, GLA_K), lambda bi, i: (bi, blk(i), 2304 // GLA_K)),
                pl.BlockSpec((1, rows, BRANCH_WIDTH), lambda bi, i: (bi, blk(i), 2560 // BRANCH_WIDTH)),
                pl.BlockSpec((1, rows, LANES), lambda bi, i: (bi, blk(i), 3072 // LANES))]

    out_f = pl.BlockSpec((1, rows, BRANCH_WIDTH), lambda bi, i: (bi, i, 0))
    out_b = pl.BlockSpec((1, rows, BRANCH_WIDTH), lambda bi, i: (bi, n - 1 - i, 0))
    shape = jax.ShapeDtypeStruct((b, s, BRANCH_WIDTH), F32)
    return pl.pallas_call(
        functools.partial(_gla_kernel, rows=rows), grid=(b, n),
        in_specs=specs(False) + specs(True) + [pl.BlockSpec((2, LANES, GLA_K), lambda bi, i: (0, 0, 0)),
                                               pl.BlockSpec((2, 1, GLA_K), lambda bi, i: (0, 0, 0))],
        out_specs=[out_f, out_b], out_shape=[shape, shape],
        scratch_shapes=[pltpu.VMEM((2, GLA_HEADS, GLA_DV, GLA_K), F32)],
        compiler_params=_params(2), name="gla")(
            proj_a3, proj_a3, proj_a3, proj_b3, proj_a3, proj_a3, proj_a3, proj_b3, wg, bg)


def _group_roll(x, k, group, lane):
    a = pltpu.roll(x, k, axis=1)
    b = pltpu.roll(x, (k - group) % LANES, axis=1)
    return jnp.where((lane & (group - 1)) >= k, a, b)


def _rank_in_group(x, stride, group, lane):
    rank = jnp.zeros(x.shape, F32)
    for m in range(1, group // stride):
        other = _group_roll(x, m * stride, group, lane)
        lower_index = (lane & (group - 1)) >= m * stride
        rank = rank + jnp.where(lower_index, jnp.where(other >= x, 1.0, 0.0), jnp.where(other > x, 1.0, 0.0))
    return rank


def _merge_kernel(x_ref, oa_ref, ys_ref, gf_ref, gb_ref, r_ref, gl0_ref, gl1_ref, gl2_ref,
                  wglu_ref, bglu_ref, gng_ref, wup_ref, mb_ref, wout_ref, lng_ref, lnb_ref, rw_ref, rb_ref,
                  xo_ref, xob_ref, comb_ref):
    y = ys_ref[...]
    gelu = 0.5 * y * (1.0 + jnp.tanh(math.sqrt(2.0 / math.pi) * (y + 0.044715 * (y * y * y))))
    o_b = gelu * _sigmoid(jnp.dot(gelu.astype(BF16), wglu_ref[...], preferred_element_type=F32) + bglu_ref[...])

    r = r_ref[...]
    gated = r * _sigmoid(r)
    heads = []
    for h in range(GLA_HEADS):
        sl = slice(h * GLA_DV, (h + 1) * GLA_DV)
        o = gf_ref[:, sl] + gb_ref[:, sl]
        ms = jnp.mean(o * o, axis=-1, keepdims=True)
        heads.append(o * lax.rsqrt(ms + NORM_EPS) * gng_ref[...] * gated[:, sl])
    o_c = jnp.concatenate(heads, axis=-1)

    merged = None
    for n, (o_n, gl_ref) in enumerate(((oa_ref[...], gl0_ref), (o_b, gl1_ref), (o_c, gl2_ref))):
        gate = _sigmoid(gl_ref[...] + mb_ref[n])
        term = gate * jnp.dot(o_n.astype(BF16), wup_ref[n], preferred_element_type=F32)
        merged = term if merged is None else merged + term
    mix = jnp.dot(merged.astype(BF16), wout_ref[...], preferred_element_type=F32)
    x1 = _layer_norm(DN_ALPHA * x_ref[...] + mix, lng_ref[...], lnb_ref[...])
    xo_ref[...] = x1
    xob_ref[...] = x1.astype(BF16)

    scores = _sigmoid(jnp.dot(x1, rw_ref[...], precision=HIGHEST, preferred_element_type=F32))
    lane = lax.broadcasted_iota(jnp.int32, scores.shape, 1)
    biased = scores + rb_ref[...]
    top2 = _rank_in_group(biased, 1, EXPERTS_PER_GROUP, lane) < 2.0
    kept = jnp.where(top2, biased, 0.0)
    group_score = kept
    for m in range(1, EXPERTS_PER_GROUP):
        group_score = group_score + _group_roll(kept, m, EXPERTS_PER_GROUP, lane)
    best_group = _rank_in_group(group_score, EXPERTS_PER_GROUP, N_EXPERTS, lane) < 1.0
    w_sel = jnp.where(top2, jnp.where(best_group, jnp.where(lane < N_EXPERTS, scores, 0.0), 0.0), 0.0)
    comb_ref[...] = w_sel / jnp.sum(w_sel, axis=-1, keepdims=True)


def _merge(x, o_a, y_s5, g_f, g_b, proj_a, proj_b, wglu, bglu, gng, wup, mb, wout, lng, lnb, rw, rb, tm=256):
    t, d = x.shape
    w = BRANCH_WIDTH
    row = lambda width, cb=0: pl.BlockSpec((tm, width), lambda i: (i, cb))
    full = lambda a: pl.BlockSpec(a.shape, lambda i: (0,) * a.ndim)
    weights = (wglu, bglu, gng, wup, mb, wout, lng, lnb, rw, rb)
    return pl.pallas_call(
        _merge_kernel, grid=(t // tm,),
        in_specs=[row(d), row(w), row(w), row(w), row(w), row(w, 3072 // w), row(d, 0), row(d, 1), row(d, 2)]
        + [full(a) for a in weights],
        out_specs=[row(d), row(d), row(LANES)],
        out_shape=[jax.ShapeDtypeStruct((t, d), F32), jax.ShapeDtypeStruct((t, d), BF16),
                   jax.ShapeDtypeStruct((t, LANES), F32)],
        compiler_params=_params(1), name="merge")(
            x, o_a, y_s5, g_f, g_b, proj_a, proj_b, proj_b, proj_b, *weights)


MOE_SUB = 256
MOE_CAP = 64


def _swiglu(x, wg_ref, wu_ref, wd_ref, row_scale):
    g = jnp.dot(x, wg_ref[0], preferred_element_type=F32)
    u = jnp.dot(x, wu_ref[0], preferred_element_type=F32)
    h = (g * _sigmoid(g)) * u * row_scale
    return jnp.dot(h.astype(BF16), wd_ref[0], preferred_element_type=F32)


def _moe_kernel(xb_ref, xf_ref, comb_ref, wg_ref, wu_ref, wd_ref, lng_ref, lnb_ref, of_ref, ob_ref,
                pos_ref, ind_ref, cmb_ref, sel_ref, xg_ref, y_ref, cg_ref, acc_ref, flag_ref, *, tm, cap):
    e = pl.program_id(1)
    ns = tm // MOE_SUB
    gs = EXPERTS_PER_GROUP
    sub = lambda k: slice(k * MOE_SUB, (k + 1) * MOE_SUB)
    slots = lambda k: slice(k * cap, (k + 1) * cap)
    r_i = lax.broadcasted_iota(jnp.int32, (MOE_SUB, MOE_SUB), 0)
    c_i = lax.broadcasted_iota(jnp.int32, (MOE_SUB, MOE_SUB), 1)

    @pl.when(e == 0)
    def _():
        comb_t = comb_ref[...].T
        ind = jnp.where(comb_t > 0.0, 1.0, 0.0)
        cmb_ref[...] = comb_t
        ind_ref[...] = ind
        before = jnp.where(r_i < c_i, 1.0, 0.0).astype(BF16)
        most = jnp.zeros((LANES, 1), F32)
        for k in range(ns):
            pos_ref[:, sub(k)] = jnp.dot(ind[:, sub(k)].astype(BF16), before, preferred_element_type=F32)
            most = jnp.maximum(most, jnp.sum(ind[:, sub(k)], axis=-1, keepdims=True))
        flag_ref[0] = (jnp.max(most) > cap).astype(jnp.int32)
        acc_ref[...] = jnp.zeros_like(acc_ref)

    slot_id = lax.broadcasted_iota(jnp.int32, (cap, 1), 0).astype(F32)
    for grp in range(N_EXPERTS // gs):
        @pl.when(e == grp * gs)
        def _(grp=grp):
            for k in range(ns):
                pieces = []
                for j in range(gs):
                    ee = grp * gs + j
                    one_hot = jnp.where(pos_ref[ee:ee + 1, sub(k)] == slot_id, ind_ref[ee:ee + 1, sub(k)], 0.0)
                    cg_ref[j, slots(k), :] = jnp.sum(one_hot * cmb_ref[ee:ee + 1, sub(k)], axis=-1, keepdims=True)
                    pieces.append(one_hot)
                sel = jnp.concatenate(pieces, axis=0).astype(BF16)
                sel_ref[k] = sel
                xg = jnp.dot(sel, xb_ref[sub(k), :], preferred_element_type=F32).astype(BF16)
                for j in range(gs):
                    xg_ref[j, slots(k), :] = xg[j * cap:(j + 1) * cap]

    j = e % gs
    y_ref[j] = _swiglu(xg_ref[j], wg_ref, wu_ref, wd_ref, cg_ref[j]).astype(BF16)

    @pl.when(j == gs - 1)
    def _():
        for k in range(ns):
            yk = jnp.concatenate([y_ref[jj, slots(k), :] for jj in range(gs)], axis=0)
            acc_ref[sub(k), :] += lax.dot_general(sel_ref[k], yk, (((0,), (0,)), ((), ())),
                                                  preferred_element_type=F32)

    @pl.when(flag_ref[0] > 0)
    def _():
        comb = comb_ref[...]
        earlier = jnp.where(r_i > c_i, 1.0, 0.0).astype(BF16)
        over = []
        for k in range(ns):
            pos = jnp.dot(earlier, jnp.where(comb[sub(k)] > 0.0, 1.0, 0.0).astype(BF16), preferred_element_type=F32)
            over.append(jnp.where(pos >= cap, comb[sub(k)], 0.0))
        over = jnp.concatenate(over, axis=0)
        lane = lax.broadcasted_iota(jnp.int32, over.shape, 1)
        ce = jnp.sum(jnp.where(lane == e, over, 0.0), axis=-1, keepdims=True)
        acc_ref[...] += _swiglu(xb_ref[...], wg_ref, wu_ref, wd_ref, ce)

    @pl.when(e == N_EXPERTS - 1)
    def _():
        y = _layer_norm(DN_ALPHA * xf_ref[...] + acc_ref[...], lng_ref[...], lnb_ref[...])
        of_ref[...] = y
        ob_ref[...] = y.astype(BF16)


def _moe(xb, xf, comb, wg, wu, wd, lng, lnb, tm=1024, cap=MOE_CAP):
    t, d = xf.shape
    ns = tm // MOE_SUB
    gs = EXPERTS_PER_GROUP
    row = lambda width: pl.BlockSpec((tm, width), lambda i, e: (i, 0))
    vec = pl.BlockSpec((1, d), lambda i, e: (0, 0))
    return pl.pallas_call(
        functools.partial(_moe_kernel, tm=tm, cap=cap), grid=(t // tm, N_EXPERTS),
        in_specs=[row(d), row(d), row(LANES),
                  pl.BlockSpec((1, d, EXPERT_FF), lambda i, e: (e, 0, 0)),
                  pl.BlockSpec((1, d, EXPERT_FF), lambda i, e: (e, 0, 0)),
                  pl.BlockSpec((1, EXPERT_FF, d), lambda i, e: (e, 0, 0)), vec, vec],
        out_specs=[row(d), row(d)],
        out_shape=[jax.ShapeDtypeStruct((t, d), F32), jax.ShapeDtypeStruct((t, d), BF16)],
        scratch_shapes=[pltpu.VMEM((LANES, tm), F32), pltpu.VMEM((LANES, tm), F32), pltpu.VMEM((LANES, tm), F32),
                        pltpu.VMEM((ns, gs * cap, MOE_SUB), BF16), pltpu.VMEM((gs, ns * cap, d), BF16),
                        pltpu.VMEM((gs, ns * cap, d), BF16), pltpu.VMEM((gs, ns * cap, 1), F32),
                        pltpu.VMEM((tm, d), F32), pltpu.SMEM((1,), jnp.int32)],
        compiler_params=_params(2), name="moe")(xb, xf, comb, wg, wu, wd, lng, lnb)


def kernel(x, ln0_g, ln0_b, w_in, da_lambda, da_norm_g, s5_a_re, s5_a_im, s5_log_dt, s5_b_re, s5_b_im,
           s5_c_re, s5_c_im, s5_d, s5_w_glu, s5_b_glu, gla_w_gate, gla_b_gate, gla_norm_g, merge_w_up,
           merge_b, w_out, ln1_g, ln1_b, router_w, router_bias, moe_w_gate, moe_w_up, moe_w_down,
           ln2_g, ln2_b):
    b, s, d = x.shape
    t = b * s
    vec = lambda a: a.astype(F32).reshape(1, -1)
    xf, xb = _ln0(x.reshape(t, d), ln0_g.astype(F32), ln0_b.astype(F32))
    rw = jnp.zeros((d, LANES), F32).at[:, :N_EXPERTS].set(router_w.astype(F32))
    rb = jnp.zeros((1, LANES), F32).at[0, :N_EXPERTS].set(router_bias.astype(F32))
    for l in range(DEPTH):
        w_a = w_in[l][:, :PROJ_A].astype(BF16)
        w_b = jnp.concatenate([w_in[l][:, PROJ_A + 2 * GLA_GATE_RANK:], w_in[l][:, PROJ_A:PROJ_A + 2 * GLA_GATE_RANK],
                               jnp.zeros((d, LANES - 2 * GLA_GATE_RANK), w_in.dtype)], axis=1).astype(BF16)
        proj_a = _matmul(xb, w_a, 1024, PROJ_A // 2, "in_proj_a")
        proj_b = _matmul(xb, w_b, 512, PROJ_B, "in_proj_b")
        proj_a3 = proj_a.reshape(b, s, PROJ_A)
        proj_b3 = proj_b.reshape(b, s, PROJ_B)

        lam_init = 0.8 - 0.6 * math.exp(-0.3 * l)
        o_a = _diff_attention(proj_a3, da_lambda[l].astype(F32), da_norm_g[l].astype(F32), lam_init)
        mats = _s5_matrices(s5_a_re[l], s5_a_im[l], s5_log_dt[l], s5_b_re[l], s5_b_im[l], s5_c_re[l], s5_c_im[l])
        y_s5 = _s5(proj_a3[:, :, 1536:2048], mats, s5_d[l])
        g_f, g_b = _gla(proj_a3, proj_b3, gla_w_gate[l], gla_b_gate[l])

        xf, xb, comb = _merge(
            xf, o_a.reshape(t, BRANCH_WIDTH), y_s5, g_f.reshape(t, BRANCH_WIDTH), g_b.reshape(t, BRANCH_WIDTH),
            proj_a, proj_b, s5_w_glu[l].astype(BF16), vec(s5_b_glu[l]), vec(gla_norm_g[l]),
            merge_w_up[l].astype(BF16), merge_b[l].astype(F32).reshape(N_BRANCH, 1, d), w_out[l].astype(BF16),
            vec(ln1_g[l]), vec(ln1_b[l]), rw, rb)
        xf, xb = _moe(xb, xf, comb, moe_w_gate[l].astype(BF16), moe_w_up[l].astype(BF16),
                      moe_w_down[l].astype(BF16), vec(ln2_g[l]), vec(ln2_b[l]))
    return xf.reshape(b, s, d)
```

```python
import functools
import math

import jax
import jax.numpy as jnp
from jax import lax
from jax.experimental import pallas as pl
from jax.experimental.pallas import tpu as pltpu

F32 = jnp.float32
BF16 = jnp.bfloat16
HIGHEST = lax.Precision.HIGHEST

D_MODEL = 1024
DEPTH = 2
BRANCH_WIDTH = D_MODEL // 2
N_BRANCH = 3

DA_HEADS = 4
DA_HEAD_DIM = BRANCH_WIDTH // (2 * DA_HEADS)
DA_V_DIM = 2 * DA_HEAD_DIM
ALIBI_MAX_EXP = 8.0

S5_GROUP_SIZE = 16
S5_GROUPS = BRANCH_WIDTH // S5_GROUP_SIZE
S5_STATE = 64
S5_CHUNK = 16

GLA_HEADS = 4
GLA_DV = BRANCH_WIDTH // GLA_HEADS
GLA_DK = GLA_DV // 2
GLA_K = GLA_HEADS * GLA_DK
GLA_GATE_RANK = 16
GLA_TAU = 16.0
GLA_CHUNK = 64

N_EXPERTS = 16
EXPERTS_PER_GROUP = 4
EXPERT_FF = D_MODEL // 2

DN_ALPHA = (2.0 * DEPTH) ** 0.25
NORM_EPS = 1e-5
LOG2E = math.log2(math.e)

LANES = 128
W_IN_U = 3 * BRANCH_WIDTH
W_IN_GLA = W_IN_U + BRANCH_WIDTH
W_IN_Z = W_IN_GLA + 2 * GLA_K + 2 * BRANCH_WIDTH
A_QC = W_IN_U
A_KC = A_QC + GLA_K
A_VC = A_KC + GLA_K
A_RC = A_VC + BRANCH_WIDTH
PROJ_A = A_RC + BRANCH_WIDTH
GATE_COLS = N_BRANCH * D_MODEL
PROJ_B = GATE_COLS + 2 * GLA_K
VMEM_LIMIT = 56 * 1024 * 1024


def _params(n_grid_dims):
    return pltpu.CompilerParams(dimension_semantics=("arbitrary",) * n_grid_dims,
                                vmem_limit_bytes=VMEM_LIMIT)


def _layer_norm(xf, g, b):
    mu = jnp.mean(xf, axis=-1, keepdims=True)
    xc = xf - mu
    var = jnp.mean(xc * xc, axis=-1, keepdims=True)
    return xc * lax.rsqrt(var + NORM_EPS) * g + b


def _sigmoid(x):
    return 1.0 / (1.0 + jnp.exp(-x))


def _ln0_kernel(x_ref, g_ref, b_ref, of_ref, ob_ref):
    y = _layer_norm(x_ref[...], g_ref[...], b_ref[...])
    of_ref[...] = y
    ob_ref[...] = y.astype(BF16)


def _ln0(x2, g, b, tm=512):
    t, d = x2.shape
    row = pl.BlockSpec((tm, d), lambda i: (i, 0))
    vec = pl.BlockSpec((1, d), lambda i: (0, 0))
    return pl.pallas_call(
        _ln0_kernel, grid=(t // tm,), in_specs=[row, vec, vec], out_specs=[row, row],
        out_shape=[jax.ShapeDtypeStruct((t, d), F32), jax.ShapeDtypeStruct((t, d), BF16)],
        compiler_params=_params(1), name="ln0")(x2, g.reshape(1, d), b.reshape(1, d))


def _mm_kernel(x_ref, w_ref, o_ref):
    o_ref[...] = jnp.dot(x_ref[...], w_ref[...], preferred_element_type=F32).astype(o_ref.dtype)


def _matmul(x, w, tm, tn, name):
    t, k = x.shape
    n = w.shape[1]
    return pl.pallas_call(
        _mm_kernel, grid=(t // tm, n // tn),
        in_specs=[pl.BlockSpec((tm, k), lambda i, j: (i, 0)), pl.BlockSpec((k, tn), lambda i, j: (0, j))],
        out_specs=pl.BlockSpec((tm, tn), lambda i, j: (i, j)),
        out_shape=jax.ShapeDtypeStruct((t, n), F32),
        compiler_params=_params(2), name=name)(x, w)


def _split3(x):
    hi = x.astype(BF16).astype(F32)
    mid = (x - hi).astype(BF16).astype(F32)
    lo = (x - hi - mid).astype(BF16).astype(F32)
    return hi, mid, lo


def _attn_kernel(lam_ref, g_ref, q_ref, k_ref, v_ref, o_ref, ka_ref, vb_ref, lhs_ref, fix_ref, m_ref, acc_ref,
                 *, tq, tk, rc, seq, lam_init):
    h = pl.program_id(1)
    qi = pl.program_id(2)
    dh = DA_HEAD_DIM

    slopes = [2.0 ** (-ALIBI_MAX_EXP * (i + 1) / DA_HEADS) for i in range(DA_HEADS)]
    slope = jnp.float32(slopes[-1])
    for i in range(DA_HEADS - 1):
        slope = jnp.where(h == i, jnp.float32(slopes[i]), slope)
    sigma = slope * LOG2E

    @pl.when(qi == 0)
    def _():
        ka_ref[:LANES, :] = k_ref[0].T.astype(BF16)
        t_hi, t_mid, t_lo = _split3(lax.broadcasted_iota(jnp.int32, (1, seq), 1).astype(F32) * sigma)
        sub_k = lax.broadcasted_iota(jnp.int32, (LANES, seq), 0)
        aug_k = jnp.where(sub_k < 3, 1.0, jnp.where(sub_k == 3, t_hi, jnp.where(sub_k == 4, t_mid,
                          jnp.where(sub_k == 5, t_lo, 0.0))))
        ka_ref[LANES:, :] = aug_k.astype(BF16)
        lane_v = lax.broadcasted_iota(jnp.int32, (seq, LANES), 1)
        vb_ref[:, :LANES] = v_ref[0].astype(BF16)
        vb_ref[:, LANES:] = jnp.where(lane_v == 0, 1.0, 0.0).astype(BF16)
        for r in range(tk // rc):
            delta = (r * rc + lax.broadcasted_iota(jnp.int32, (rc, 1), 0)
                     - lax.broadcasted_iota(jnp.int32, (1, tk), 1)).astype(F32)
            fix_ref[r] = (2.0 * sigma) * jnp.minimum(delta, 0.0)

    q = q_ref[0] * (dh ** -0.5 * LOG2E)
    lane = lax.broadcasted_iota(jnp.int32, (tq, LANES), 1)
    row = lax.broadcasted_iota(jnp.int32, (tq, 1), 0)
    u_hi, u_mid, u_lo = _split3((qi * tq + row).astype(F32) * (-sigma))
    aug_q = jnp.where(lane == 0, u_hi, jnp.where(lane == 1, u_mid, jnp.where(lane == 2, u_lo,
                      jnp.where(lane < 6, 1.0, 0.0))))
    for mp in range(2):
        qz = jnp.where(lane < dh, q, 0.0) if mp == 0 else jnp.where(lane >= dh, q, 0.0)
        lhs_ref[mp, 0] = jnp.concatenate([qz, aug_q], axis=-1).astype(BF16)
        lhs_ref[mp, 1] = jnp.concatenate([qz, -aug_q], axis=-1).astype(BF16)
    m_ref[...] = jnp.full(m_ref.shape, -1e30, F32)
    acc_ref[...] = jnp.zeros(acc_ref.shape, F32)

    def block(j, local):
        off = pl.multiple_of(j * tk, tk)
        ka = ka_ref[:, pl.ds(off, tk)]
        vb = vb_ref[pl.ds(off, tk), :]
        for mp in range(2):
            for c in range(tq // rc):
                rows = slice(c * rc, (c + 1) * rc)
                if isinstance(local, str):
                    sign, fixup = (0 if local == "before" else 1), None
                elif (c + 1) * rc <= local * tk:
                    sign, fixup = 1, None
                elif c * rc >= (local + 1) * tk:
                    sign, fixup = 0, None
                else:
                    sign, fixup = 0, fix_ref[(c * rc - local * tk) // rc]
                s = jnp.dot(lhs_ref[mp, sign, rows, :], ka, preferred_element_type=F32)
                if fixup is not None:
                    s = s + fixup
                m_old = m_ref[mp, rows, :]
                m_new = jnp.maximum(m_old, jnp.max(s, axis=-1, keepdims=True))
                alpha = jnp.exp2(m_old - m_new)
                p = jnp.exp2(s - m_new).astype(BF16)
                acc_ref[mp, rows, :] = alpha * acc_ref[mp, rows, :] + jnp.dot(p, vb, preferred_element_type=F32)
                m_ref[mp, rows, :] = m_new

    def run(lo, hi, where):
        def body(j, carry):
            block(j, where)
            return carry
        lax.fori_loop(lo, hi, body, 0)

    j0 = (qi * tq) // tk
    run(0, j0, "before")
    for local in range(tq // tk):
        block(j0 + local, local)
    run(j0 + tq // tk, seq // tk, "after")

    lv = lam_ref[...]
    lam = (jnp.exp(jnp.sum(lv[0:1] * lv[1:2], axis=-1, keepdims=True))
           - jnp.exp(jnp.sum(lv[2:3] * lv[3:4], axis=-1, keepdims=True)) + lam_init)
    o1 = acc_ref[0, :, :LANES] / acc_ref[0, :, LANES:LANES + 1]
    o2 = acc_ref[1, :, :LANES] / acc_ref[1, :, LANES:LANES + 1]
    o = o1 - lam * o2
    ms = jnp.mean(o * o, axis=-1, keepdims=True)
    o_ref[0] = o * lax.rsqrt(ms + NORM_EPS) * g_ref[...] * (1.0 - lam_init)


def _diff_attention(proj_a3, da_lambda, da_norm_g, lam_init, tq=2048, tk=1024, rc=256):
    b, s, _ = proj_a3.shape
    assert tq % tk == 0 and s % tq == 0 and tk % rc == 0
    kern = functools.partial(_attn_kernel, tq=tq, tk=tk, rc=rc, seq=s, lam_init=lam_init)
    return pl.pallas_call(
        kern, grid=(b, DA_HEADS, s // tq),
        in_specs=[
            pl.BlockSpec((4, DA_HEAD_DIM), lambda bi, h, i: (0, 0)),
            pl.BlockSpec((1, DA_V_DIM), lambda bi, h, i: (0, 0)),
            pl.BlockSpec((1, tq, LANES), lambda bi, h, i: (bi, i, h)),
            pl.BlockSpec((1, s, LANES), lambda bi, h, i: (bi, 0, DA_HEADS + h)),
            pl.BlockSpec((1, s, LANES), lambda bi, h, i: (bi, 0, 2 * DA_HEADS + h)),
        ],
        out_specs=pl.BlockSpec((1, tq, LANES), lambda bi, h, i: (bi, i, h)),
        out_shape=jax.ShapeDtypeStruct((b, s, BRANCH_WIDTH), F32),
        scratch_shapes=[pltpu.VMEM((2 * LANES, s), BF16), pltpu.VMEM((s, 2 * LANES), BF16),
                        pltpu.VMEM((2, 2, tq, 2 * LANES), BF16), pltpu.VMEM((tk // rc, rc, tk), F32),
                        pltpu.VMEM((2, tq, 1), F32), pltpu.VMEM((2, tq, 2 * LANES), F32)],
        compiler_params=_params(3), name="diff_attention")(
            da_lambda, da_norm_g.reshape(1, DA_V_DIM), proj_a3, proj_a3, proj_a3)


def _s5_matrices(a_re, a_im, log_dt, b_re, b_im, c_re, c_im):
    L, C, P, G = S5_CHUNK, S5_GROUP_SIZE, S5_STATE, S5_GROUPS
    f = lambda t: t.astype(F32)
    a_re, a_im, b_re, b_im, c_re, c_im = map(f, (a_re, a_im, b_re, b_im, c_re, c_im))
    dt = jnp.exp(f(log_dt))[..., None]
    lr, li = a_re * dt, a_im * dt

    def apow(j):
        jj = j.astype(F32)[:, None, None, None]
        mag = jnp.exp(lr[None] * jj)
        return mag * jnp.cos(li[None] * jj), mag * jnp.sin(li[None] * jj)

    a1r, a1i = apow(jnp.arange(1, 2))
    nr, ni = a1r[0] - 1.0, a1i[0]
    den = a_re * a_re + a_im * a_im
    cr, ci = (nr * a_re + ni * a_im) / den, (ni * a_re - nr * a_im) / den
    bbr = cr[..., None] * b_re - ci[..., None] * b_im
    bbi = cr[..., None] * b_im + ci[..., None] * b_re

    pr, pi = apow(jnp.arange(L))
    wr = pr[..., None] * bbr[None] - pi[..., None] * bbi[None]
    wi = pr[..., None] * bbi[None] + pi[..., None] * bbr[None]
    kern = (jnp.einsum('dgcp,jdgpe->jdgce', c_re, wr, precision=HIGHEST)
            - jnp.einsum('dgcp,jdgpe->jdgce', c_im, wi, precision=HIGHEST))

    s_idx = jnp.arange(L)[:, None]
    t_idx = jnp.arange(L)[None, :]

    def toeplitz(kd, lag, valid):
        m = kd[jnp.clip(lag, 0, L - 1)] * valid[:, :, None, None, None].astype(F32)
        return m.transpose(2, 0, 4, 1, 3).reshape(G, L * C, L * C)

    m_mat = jnp.stack([toeplitz(kern[:, 0], t_idx - s_idx, t_idx >= s_idx),
                       toeplitz(kern[:, 1], s_idx - t_idx, s_idx >= t_idx)])

    def summary(w_dir):
        return w_dir.transpose(1, 0, 3, 2).reshape(G, L * C, P)

    p_re = jnp.stack([summary(wr[::-1, 0]), summary(wr[:, 1])])
    p_im = jnp.stack([summary(wi[::-1, 0]), summary(wi[:, 1])])

    qr, qi = apow(jnp.arange(1, L + 1))
    car = c_re[None] * qr[:, :, :, None, :] - c_im[None] * qi[:, :, :, None, :]
    cai = c_re[None] * qi[:, :, :, None, :] + c_im[None] * qr[:, :, :, None, :]

    def readout(ca_dir):
        return ca_dir.transpose(1, 3, 0, 2).reshape(G, P, L * C)

    q_re = jnp.stack([readout(car[:, 0]), readout(car[::-1, 1])])
    q_im = jnp.stack([-readout(cai[:, 0]), -readout(cai[::-1, 1])])

    half = jax.nn.one_hot(jnp.arange(G) % 2, 2, dtype=F32)
    p_pad = jnp.stack([p_re, p_im], axis=-2)[:, :, :, :, None, :] * half[None, :, None, None, :, None]
    p_pad = p_pad.reshape(2, G, L * C, 4 * P)
    q_pad = jnp.stack([q_re, q_im], axis=2)[:, :, :, None] * half[None, :, None, :, None, None]
    q_pad = q_pad.reshape(2, G, 4 * P, L * C)

    alr, ali = apow(jnp.arange(L, L + 1))
    a_l = jnp.concatenate([alr[0].reshape(2, G // 2, 2 * P), ali[0].reshape(2, G // 2, 2 * P)], axis=-1)
    return m_mat.astype(BF16), p_pad.astype(BF16), q_pad.astype(BF16), a_l.reshape(2, G // 2, 1, 4 * P)


def _s5_kernel(x_ref, m_ref, p_ref, q_ref, al_ref, d_ref, y_ref, s_ref, xin_ref, *, n_chunks, bsz):
    hp = 2 * S5_STATE
    xb = [x_ref[g].astype(BF16) for g in range(2)]
    for d in range(2):
        s = (jnp.dot(xb[0], p_ref[d, 0], preferred_element_type=F32)
             + jnp.dot(xb[1], p_ref[d, 1], preferred_element_type=F32))
        s_ref[2 * d] = s[:, :hp]
        s_ref[2 * d + 1] = s[:, hp:]

    al = [al_ref[d, 0] for d in range(2)]
    alr = [a[:, :hp] for a in al]
    ali = [a[:, hp:] for a in al]

    def body(i, carry):
        new = []
        for d in range(2):
            re, im = carry[2 * d], carry[2 * d + 1]
            n = i if d == 0 else n_chunks - 1 - i
            rows = pl.ds(n, bsz, stride=n_chunks)
            xin_ref[2 * d, rows, :] = re
            xin_ref[2 * d + 1, rows, :] = im
            new.append(alr[d] * re - ali[d] * im + s_ref[2 * d, rows, :])
            new.append(alr[d] * im + ali[d] * re + s_ref[2 * d + 1, rows, :])
        return tuple(new)

    zero = jnp.zeros((bsz, hp), F32)
    lax.fori_loop(0, n_chunks, body, (zero, zero, zero, zero))

    xin = [jnp.concatenate([xin_ref[2 * d], xin_ref[2 * d + 1]], axis=-1).astype(BF16) for d in range(2)]
    for g in range(2):
        y = x_ref[g] * d_ref[g]
        for d in range(2):
            y = y + jnp.dot(xb[g], m_ref[d, g], preferred_element_type=F32)
            y = y + jnp.dot(xin[d], q_ref[d, g], preferred_element_type=F32)
        y_ref[g] = y


def _s5_proj_kernel(x_ref, w_ref, o_ref):
    d = w_ref.shape[0]
    c = S5_GROUP_SIZE
    for t in range(S5_CHUNK):
        res = jnp.dot(x_ref[:, t * d:(t + 1) * d], w_ref[...], preferred_element_type=F32)
        for g in range(S5_GROUPS):
            o_ref[g, :, t * c:(t + 1) * c] = res[:, g * c:(g + 1) * c]


def _s5_unpack_kernel(y_ref, o_ref):
    c = S5_GROUP_SIZE
    w = S5_GROUPS * c
    for t in range(S5_CHUNK):
        for g in range(S5_GROUPS):
            o_ref[:, t * w + g * c:t * w + (g + 1) * c] = y_ref[g, :, t * c:(t + 1) * c]


def _s5_proj(xb, w_u, tm=256):
    t, d = xb.shape
    rows = t // S5_CHUNK
    lc = S5_CHUNK * S5_GROUP_SIZE
    return pl.pallas_call(
        _s5_proj_kernel, grid=(rows // tm,),
        in_specs=[pl.BlockSpec((tm, S5_CHUNK * d), lambda i: (i, 0)), pl.BlockSpec(w_u.shape, lambda i: (0, 0))],
        out_specs=pl.BlockSpec((S5_GROUPS, tm, lc), lambda i: (0, i, 0)),
        out_shape=jax.ShapeDtypeStruct((S5_GROUPS, rows, lc), F32),
        compiler_params=_params(1), name="s5_proj")(xb.reshape(rows, S5_CHUNK * d), w_u)


def _s5(x, mats, d_skip, b, s):
    L, C, G = S5_CHUNK, S5_GROUP_SIZE, S5_GROUPS
    w = G * C
    n = s // L
    lc = L * C
    m_mat, p_pad, q_pad, a_l = mats
    d_row = jnp.tile(d_skip.astype(F32).reshape(G, 1, C), (1, L, 1)).reshape(G, 1, lc)
    mat_spec = pl.BlockSpec((2, 2, lc, lc), lambda k: (0, k, 0, 0))
    kern = functools.partial(_s5_kernel, n_chunks=n, bsz=b)
    y = pl.pallas_call(
        kern, grid=(G // 2,),
        in_specs=[pl.BlockSpec((2, n * b, lc), lambda k: (k, 0, 0)), mat_spec, mat_spec, mat_spec,
                  pl.BlockSpec((2, 1, 1, lc), lambda k: (0, k, 0, 0)),
                  pl.BlockSpec((2, 1, lc), lambda k: (k, 0, 0))],
        out_specs=pl.BlockSpec((2, n * b, lc), lambda k: (k, 0, 0)),
        out_shape=jax.ShapeDtypeStruct((G, n * b, lc), F32),
        scratch_shapes=[pltpu.VMEM((4, n * b, LANES), F32), pltpu.VMEM((4, n * b, LANES), F32)],
        compiler_params=_params(1), name="s5")(x, m_mat, p_pad, q_pad, a_l, d_row)
    tm = 256
    tokens = pl.pallas_call(
        _s5_unpack_kernel, grid=(b * n // tm,),
        in_specs=[pl.BlockSpec((G, tm, lc), lambda i: (0, i, 0))],
        out_specs=pl.BlockSpec((tm, L * w), lambda i: (i, 0)),
        out_shape=jax.ShapeDtypeStruct((b * n, L * w), F32),
        compiler_params=_params(1), name="s5_unpack")(y)
    return tokens.reshape(b * s, w)


def _gla_kernel(qf_ref, kf_ref, vf_ref, gf_ref, qb_ref, kb_ref, vb_ref, gb_ref, bg_ref,
                of_ref, ob_ref, st_ref, *, rows):
    c = GLA_CHUNK
    n_sub = rows // c

    @pl.when(pl.program_id(1) == 0)
    def _():
        st_ref[...] = jnp.zeros_like(st_ref)

    nt = (((1,), (1,)), ((), ()))
    tn = (((0,), (0,)), ((), ()))
    log2 = lambda n: int(math.log2(n))
    r_r = lax.broadcasted_iota(jnp.int32, (rows, rows), 0)
    c_r = lax.broadcasted_iota(jnp.int32, (rows, rows), 1)
    same_chunk = (r_r >> log2(c)) == (c_r >> log2(c))
    t_row = lax.broadcasted_iota(jnp.int32, (c, GLA_K), 0)
    t_col = lax.broadcasted_iota(jnp.int32, (c, GLA_K), 1) & (c - 1)
    k_lane = lax.broadcasted_iota(jnp.int32, (1, GLA_K), 1) >> log2(GLA_DK)
    v_lane = lax.broadcasted_iota(jnp.int32, (1, BRANCH_WIDTH), 1) >> log2(GLA_DV)
    st_same_head = ((lax.broadcasted_iota(jnp.int32, (BRANCH_WIDTH, GLA_K), 0) >> log2(GLA_DV))
                    == (lax.broadcasted_iota(jnp.int32, (BRANCH_WIDTH, GLA_K), 1) >> log2(GLA_DK)))

    for d, (q_ref, k_ref, v_ref, g_ref, o_ref) in enumerate(
            ((qf_ref, kf_ref, vf_ref, gf_ref, of_ref), (qb_ref, kb_ref, vb_ref, gb_ref, ob_ref))):
        gate = g_ref[0] + bg_ref[d]
        log_a = (jnp.minimum(gate, 0.0) - jnp.log(1.0 + jnp.exp(-jnp.abs(gate)))) * (1.0 / GLA_TAU)
        if d == 0:
            tri = jnp.where(same_chunk, jnp.where(r_r >= c_r, 1.0, 0.0), 0.0).astype(BF16)
            keep, i_ref, i_last = (t_row >= t_col), c // 2, c - 1
        else:
            tri = jnp.where(same_chunk, jnp.where(c_r >= r_r, 1.0, 0.0), 0.0).astype(BF16)
            keep, i_ref, i_last = (t_col > t_row), c - 1 - c // 2, 0
        la_hi = log_a.astype(BF16)
        la_lo = (log_a - la_hi.astype(F32)).astype(BF16)
        cum_all = (jnp.dot(tri, la_hi, preferred_element_type=F32) + jnp.dot(tri, la_lo, preferred_element_type=F32))
        q = q_ref[0] * (GLA_DK ** -0.5)
        k = k_ref[0]
        v = v_ref[0]
        for ci in (range(n_sub) if d == 0 else range(n_sub - 1, -1, -1)):
            sl = slice(ci * c, (ci + 1) * c)
            cum = cum_all[sl]
            mid = cum[i_ref:i_ref + 1]
            last = cum[i_last:i_last + 1]
            qc, kc, vc = q[sl], k[sl], v[sl]
            qe = (qc * jnp.exp(cum - mid)).astype(BF16)
            ke = kc * jnp.exp(mid - cum)
            kd = (kc * jnp.exp(last - cum)).astype(BF16)
            qs = (qc * jnp.exp(cum)).astype(BF16)
            decay = jnp.exp(last)
            ke4 = jnp.concatenate([jnp.where(k_lane == h, ke, 0.0) for h in range(GLA_HEADS)], axis=0).astype(BF16)
            v4 = jnp.concatenate([jnp.where(v_lane == h, vc, 0.0) for h in range(GLA_HEADS)], axis=0).astype(BF16)
            sc = lax.dot_general(qe, ke4, nt, preferred_element_type=F32)
            sc = jnp.where(keep, sc, 0.0).astype(BF16)
            st = st_ref[d]
            o = (jnp.dot(sc, v4, preferred_element_type=F32)
                 + lax.dot_general(qs, st.astype(BF16), nt, preferred_element_type=F32))
            o_ref[0, sl, :] = o
            ds = lax.dot_general(vc.astype(BF16), kd, tn, preferred_element_type=F32)
            st_ref[d] = decay * st + jnp.where(st_same_head, ds, 0.0)


def _gla(proj_a3, proj_b3, b_gate, rows=128):
    b, s, _ = proj_a3.shape
    n = s // rows
    bg = b_gate.astype(F32).reshape(2, 1, GLA_K)

    def specs(rev):
        blk = (lambda i: n - 1 - i) if rev else (lambda i: i)
        return [pl.BlockSpec((1, rows, GLA_K), lambda bi, i: (bi, blk(i), A_QC // GLA_K)),
                pl.BlockSpec((1, rows, GLA_K), lambda bi, i: (bi, blk(i), A_KC // GLA_K)),
                pl.BlockSpec((1, rows, BRANCH_WIDTH), lambda bi, i: (bi, blk(i), A_VC // BRANCH_WIDTH)),
                pl.BlockSpec((1, rows, GLA_K), lambda bi, i: (bi, blk(i), GATE_COLS // GLA_K + int(rev)))]

    out_f = pl.BlockSpec((1, rows, BRANCH_WIDTH), lambda bi, i: (bi, i, 0))
    out_b = pl.BlockSpec((1, rows, BRANCH_WIDTH), lambda bi, i: (bi, n - 1 - i, 0))
    shape = jax.ShapeDtypeStruct((b, s, BRANCH_WIDTH), F32)
    return pl.pallas_call(
        functools.partial(_gla_kernel, rows=rows), grid=(b, n),
        in_specs=specs(False) + specs(True) + [pl.BlockSpec((2, 1, GLA_K), lambda bi, i: (0, 0, 0))],
        out_specs=[out_f, out_b], out_shape=[shape, shape],
        scratch_shapes=[pltpu.VMEM((2, BRANCH_WIDTH, GLA_K), F32)],
        compiler_params=_params(2), name="gla")(
            proj_a3, proj_a3, proj_a3, proj_b3, proj_a3, proj_a3, proj_a3, proj_b3, bg)


def _group_roll(x, k, group, lane):
    a = pltpu.roll(x, k, axis=1)
    b = pltpu.roll(x, (k - group) % LANES, axis=1)
    return jnp.where((lane & (group - 1)) >= k, a, b)


def _rank_in_group(x, stride, group, lane):
    rank = jnp.zeros(x.shape, F32)
    for m in range(1, group // stride):
        other = _group_roll(x, m * stride, group, lane)
        lower_index = (lane & (group - 1)) >= m * stride
        rank = rank + jnp.where(lower_index, jnp.where(other >= x, 1.0, 0.0), jnp.where(other > x, 1.0, 0.0))
    return rank


def _merge_kernel(x_ref, oa_ref, ys_ref, gf_ref, gb_ref, r_ref, gl0_ref, gl1_ref, gl2_ref,
                  wglu_ref, bglu_ref, gng_ref, wup_ref, mb_ref, wout_ref, lng_ref, lnb_ref, rw_ref, rb_ref,
                  xo_ref, xob_ref, comb_ref):
    y = ys_ref[...]
    gelu = 0.5 * y * (1.0 + jnp.tanh(math.sqrt(2.0 / math.pi) * (y + 0.044715 * (y * y * y))))
    o_b = gelu * _sigmoid(jnp.dot(gelu.astype(BF16), wglu_ref[...], preferred_element_type=F32) + bglu_ref[...])

    r = r_ref[...]
    gated = r * _sigmoid(r)
    heads = []
    for h in range(GLA_HEADS):
        sl = slice(h * GLA_DV, (h + 1) * GLA_DV)
        o = gf_ref[:, sl] + gb_ref[:, sl]
        ms = jnp.mean(o * o, axis=-1, keepdims=True)
        heads.append(o * lax.rsqrt(ms + NORM_EPS) * gng_ref[...] * gated[:, sl])
    o_c = jnp.concatenate(heads, axis=-1)

    merged = None
    for n, (o_n, gl_ref) in enumerate(((oa_ref[...], gl0_ref), (o_b, gl1_ref), (o_c, gl2_ref))):
        gate = _sigmoid(gl_ref[...] + mb_ref[n])
        term = gate * jnp.dot(o_n.astype(BF16), wup_ref[n], preferred_element_type=F32)
        merged = term if merged is None else merged + term
    mix = jnp.dot(merged.astype(BF16), wout_ref[...], preferred_element_type=F32)
    x1 = _layer_norm(DN_ALPHA * x_ref[...] + mix, lng_ref[...], lnb_ref[...])
    xo_ref[...] = x1
    xob_ref[...] = x1.astype(BF16)

    scores = _sigmoid(jnp.dot(x1, rw_ref[...], precision=HIGHEST, preferred_element_type=F32))
    lane = lax.broadcasted_iota(jnp.int32, scores.shape, 1)
    biased = scores + rb_ref[...]
    top2 = _rank_in_group(biased, 1, EXPERTS_PER_GROUP, lane) < 2.0
    kept = jnp.where(top2, biased, 0.0)
    group_score = kept
    for m in range(1, EXPERTS_PER_GROUP):
        group_score = group_score + _group_roll(kept, m, EXPERTS_PER_GROUP, lane)
    best_group = _rank_in_group(group_score, EXPERTS_PER_GROUP, N_EXPERTS, lane) < 1.0
    w_sel = jnp.where(top2, jnp.where(best_group, jnp.where(lane < N_EXPERTS, scores, 0.0), 0.0), 0.0)
    comb_ref[...] = w_sel / jnp.sum(w_sel, axis=-1, keepdims=True)


def _merge(x, o_a, y_s5, g_f, g_b, proj_a, proj_b, wglu, bglu, gng, wup, mb, wout, lng, lnb, rw, rb, tm=256):
    t, d = x.shape
    w = BRANCH_WIDTH
    row = lambda width, cb=0: pl.BlockSpec((tm, width), lambda i: (i, cb))
    full = lambda a: pl.BlockSpec(a.shape, lambda i: (0,) * a.ndim)
    weights = (wglu, bglu, gng, wup, mb, wout, lng, lnb, rw, rb)
    return pl.pallas_call(
        _merge_kernel, grid=(t // tm,),
        in_specs=[row(d), row(w), row(w), row(w), row(w), row(w, A_RC // w), row(d, 0), row(d, 1), row(d, 2)]
        + [full(a) for a in weights],
        out_specs=[row(d), row(d), row(LANES)],
        out_shape=[jax.ShapeDtypeStruct((t, d), F32), jax.ShapeDtypeStruct((t, d), BF16),
                   jax.ShapeDtypeStruct((t, LANES), F32)],
        compiler_params=_params(1), name="merge")(
            x, o_a, y_s5, g_f, g_b, proj_a, proj_b, proj_b, proj_b, *weights)


MOE_SUB = 256
MOE_CAP = 64


def _swiglu(x, wg_ref, wu_ref, wd_ref, row_scale):
    g = jnp.dot(x, wg_ref[0], preferred_element_type=F32)
    u = jnp.dot(x, wu_ref[0], preferred_element_type=F32)
    h = (g * _sigmoid(g)) * u * row_scale
    return jnp.dot(h.astype(BF16), wd_ref[0], preferred_element_type=F32)


def _moe_kernel(xb_ref, xf_ref, comb_ref, wg_ref, wu_ref, wd_ref, lng_ref, lnb_ref, of_ref, ob_ref,
                pos_ref, ind_ref, cmb_ref, sel_ref, xg_ref, y_ref, cg_ref, acc_ref, flag_ref, *, tm, cap):
    e = pl.program_id(1)
    ns = tm // MOE_SUB
    gs = EXPERTS_PER_GROUP
    sub = lambda k: slice(k * MOE_SUB, (k + 1) * MOE_SUB)
    slots = lambda k: slice(k * cap, (k + 1) * cap)
    r_i = lax.broadcasted_iota(jnp.int32, (MOE_SUB, MOE_SUB), 0)
    c_i = lax.broadcasted_iota(jnp.int32, (MOE_SUB, MOE_SUB), 1)

    @pl.when(e == 0)
    def _():
        comb_t = comb_ref[...].T
        ind = jnp.where(comb_t > 0.0, 1.0, 0.0)
        cmb_ref[...] = comb_t
        ind_ref[...] = ind
        before = jnp.where(r_i < c_i, 1.0, 0.0).astype(BF16)
        most = jnp.zeros((LANES, 1), F32)
        for k in range(ns):
            pos_ref[:, sub(k)] = jnp.dot(ind[:, sub(k)].astype(BF16), before, preferred_element_type=F32)
            most = jnp.maximum(most, jnp.sum(ind[:, sub(k)], axis=-1, keepdims=True))
        flag_ref[0] = (jnp.max(most) > cap).astype(jnp.int32)
        acc_ref[...] = jnp.zeros_like(acc_ref)

    slot_id = lax.broadcasted_iota(jnp.int32, (cap, 1), 0).astype(F32)
    for grp in range(N_EXPERTS // gs):
        @pl.when(e == grp * gs)
        def _(grp=grp):
            for k in range(ns):
                pieces = []
                for j in range(gs):
                    ee = grp * gs + j
                    one_hot = jnp.where(pos_ref[ee:ee + 1, sub(k)] == slot_id, ind_ref[ee:ee + 1, sub(k)], 0.0)
                    cg_ref[j, slots(k), :] = jnp.sum(one_hot * cmb_ref[ee:ee + 1, sub(k)], axis=-1, keepdims=True)
                    pieces.append(one_hot)
                sel = jnp.concatenate(pieces, axis=0).astype(BF16)
                sel_ref[k] = sel
                xg = jnp.dot(sel, xb_ref[sub(k), :], preferred_element_type=F32).astype(BF16)
                for j in range(gs):
                    xg_ref[j, slots(k), :] = xg[j * cap:(j + 1) * cap]

    j = e % gs
    y_ref[j] = _swiglu(xg_ref[j], wg_ref, wu_ref, wd_ref, cg_ref[j]).astype(BF16)

    @pl.when(j == gs - 1)
    def _():
        for k in range(ns):
            yk = jnp.concatenate([y_ref[jj, slots(k), :] for jj in range(gs)], axis=0)
            acc_ref[sub(k), :] += lax.dot_general(sel_ref[k], yk, (((0,), (0,)), ((), ())),
                                                  preferred_element_type=F32)

    @pl.when(flag_ref[0] > 0)
    def _():
        comb = comb_ref[...]
        earlier = jnp.where(r_i > c_i, 1.0, 0.0).astype(BF16)
        over = []
        for k in range(ns):
            pos = jnp.dot(earlier, jnp.where(comb[sub(k)] > 0.0, 1.0, 0.0).astype(BF16), preferred_element_type=F32)
            over.append(jnp.where(pos >= cap, comb[sub(k)], 0.0))
        over = jnp.concatenate(over, axis=0)
        lane = lax.broadcasted_iota(jnp.int32, over.shape, 1)
        ce = jnp.sum(jnp.where(lane == e, over, 0.0), axis=-1, keepdims=True)
        acc_ref[...] += _swiglu(xb_ref[...], wg_ref, wu_ref, wd_ref, ce)

    @pl.when(e == N_EXPERTS - 1)
    def _():
        y = _layer_norm(DN_ALPHA * xf_ref[...] + acc_ref[...], lng_ref[...], lnb_ref[...])
        of_ref[...] = y
        ob_ref[...] = y.astype(BF16)


def _moe(xb, xf, comb, wg, wu, wd, lng, lnb, tm=1024, cap=MOE_CAP):
    t, d = xf.shape
    ns = tm // MOE_SUB
    gs = EXPERTS_PER_GROUP
    row = lambda width: pl.BlockSpec((tm, width), lambda i, e: (i, 0))
    vec = pl.BlockSpec((1, d), lambda i, e: (0, 0))
    return pl.pallas_call(
        functools.partial(_moe_kernel, tm=tm, cap=cap), grid=(t // tm, N_EXPERTS),
        in_specs=[row(d), row(d), row(LANES),
                  pl.BlockSpec((1, d, EXPERT_FF), lambda i, e: (e, 0, 0)),
                  pl.BlockSpec((1, d, EXPERT_FF), lambda i, e: (e, 0, 0)),
                  pl.BlockSpec((1, EXPERT_FF, d), lambda i, e: (e, 0, 0)), vec, vec],
        out_specs=[row(d), row(d)],
        out_shape=[jax.ShapeDtypeStruct((t, d), F32), jax.ShapeDtypeStruct((t, d), BF16)],
        scratch_shapes=[pltpu.VMEM((LANES, tm), F32), pltpu.VMEM((LANES, tm), F32), pltpu.VMEM((LANES, tm), F32),
                        pltpu.VMEM((ns, gs * cap, MOE_SUB), BF16), pltpu.VMEM((gs, ns * cap, d), BF16),
                        pltpu.VMEM((gs, ns * cap, d), BF16), pltpu.VMEM((gs, ns * cap, 1), F32),
                        pltpu.VMEM((tm, d), F32), pltpu.SMEM((1,), jnp.int32)],
        compiler_params=_params(2), name="moe")(xb, xf, comb, wg, wu, wd, lng, lnb)


def _proj_b_weights(w_in_l, w_gate):
    z0 = W_IN_Z
    folded = [jnp.dot(w_in_l[:, z0 + i * GLA_GATE_RANK:z0 + (i + 1) * GLA_GATE_RANK].astype(F32),
                      w_gate[i].astype(F32), precision=HIGHEST) for i in range(2)]
    return jnp.concatenate([w_in_l[:, z0 + 2 * GLA_GATE_RANK:].astype(F32)] + folded, axis=1).astype(BF16)


def kernel(x, ln0_g, ln0_b, w_in, da_lambda, da_norm_g, s5_a_re, s5_a_im, s5_log_dt, s5_b_re, s5_b_im,
           s5_c_re, s5_c_im, s5_d, s5_w_glu, s5_b_glu, gla_w_gate, gla_b_gate, gla_norm_g, merge_w_up,
           merge_b, w_out, ln1_g, ln1_b, router_w, router_bias, moe_w_gate, moe_w_up, moe_w_down,
           ln2_g, ln2_b):
    b, s, d = x.shape
    t = b * s
    vec = lambda a: a.astype(F32).reshape(1, -1)
    xf, xb = _ln0(x.reshape(t, d), ln0_g.astype(F32), ln0_b.astype(F32))
    rw = jnp.zeros((d, LANES), F32).at[:, :N_EXPERTS].set(router_w.astype(F32))
    rb = jnp.zeros((1, LANES), F32).at[0, :N_EXPERTS].set(router_bias.astype(F32))
    for l in range(DEPTH):
        w_a = jnp.concatenate([w_in[l][:, :W_IN_U], w_in[l][:, W_IN_GLA:W_IN_Z]], axis=1).astype(BF16)
        w_u = w_in[l][:, W_IN_U:W_IN_GLA].astype(BF16)
        w_b = _proj_b_weights(w_in[l], gla_w_gate[l])
        proj_a = _matmul(xb, w_a, 1024, PROJ_A // 2, "in_proj_a")
        proj_b = _matmul(xb, w_b, 512, PROJ_B, "in_proj_b")
        proj_a3 = proj_a.reshape(b, s, PROJ_A)
        proj_b3 = proj_b.reshape(b, s, PROJ_B)

        lam_init = 0.8 - 0.6 * math.exp(-0.3 * l)
        o_a = _diff_attention(proj_a3, da_lambda[l].astype(F32), da_norm_g[l].astype(F32), lam_init)
        mats = _s5_matrices(s5_a_re[l], s5_a_im[l], s5_log_dt[l], s5_b_re[l], s5_b_im[l], s5_c_re[l], s5_c_im[l])
        y_s5 = _s5(_s5_proj(xb, w_u), mats, s5_d[l], b, s)
        g_f, g_b = _gla(proj_a3, proj_b3, gla_b_gate[l])

        xf, xb, comb = _merge(
            xf, o_a.reshape(t, BRANCH_WIDTH), y_s5, g_f.reshape(t, BRANCH_WIDTH), g_b.reshape(t, BRANCH_WIDTH),
            proj_a, proj_b, s5_w_glu[l].astype(BF16), vec(s5_b_glu[l]), vec(gla_norm_g[l]),
            merge_w_up[l].astype(BF16), merge_b[l].astype(F32).reshape(N_BRANCH, 1, d), w_out[l].astype(BF16),
            vec(ln1_g[l]), vec(ln1_b[l]), rw, rb)
        xf, xb = _moe(xb, xf, comb, moe_w_gate[l].astype(BF16), moe_w_up[l].astype(BF16),
                      moe_w_down[l].astype(BF16), vec(ln2_g[l]), vec(ln2_b[l]))
    return xf.reshape(b, s, d)
```

```python
import functools
import math

import jax
import jax.numpy as jnp
from jax import lax
from jax.experimental import pallas as pl
from jax.experimental.pallas import tpu as pltpu

F32 = jnp.float32
BF16 = jnp.bfloat16
HIGHEST = lax.Precision.HIGHEST

D_MODEL = 1024
DEPTH = 2
BRANCH_WIDTH = D_MODEL // 2
N_BRANCH = 3

DA_HEADS = 4
DA_HEAD_DIM = BRANCH_WIDTH // (2 * DA_HEADS)
DA_V_DIM = 2 * DA_HEAD_DIM
ALIBI_MAX_EXP = 8.0

S5_GROUP_SIZE = 16
S5_GROUPS = BRANCH_WIDTH // S5_GROUP_SIZE
S5_STATE = 64
S5_CHUNK = 16

GLA_HEADS = 4
GLA_DV = BRANCH_WIDTH // GLA_HEADS
GLA_DK = GLA_DV // 2
GLA_K = GLA_HEADS * GLA_DK
GLA_GATE_RANK = 16
GLA_TAU = 16.0
GLA_CHUNK = 64

N_EXPERTS = 16
EXPERTS_PER_GROUP = 4
EXPERT_FF = D_MODEL // 2

DN_ALPHA = (2.0 * DEPTH) ** 0.25
NORM_EPS = 1e-5
LOG2E = math.log2(math.e)

LANES = 128
W_IN_U = 3 * BRANCH_WIDTH
W_IN_GLA = W_IN_U + BRANCH_WIDTH
W_IN_Z = W_IN_GLA + 2 * GLA_K + 2 * BRANCH_WIDTH
A_QC = W_IN_U
A_KC = A_QC + GLA_K
A_VC = A_KC + GLA_K
A_RC = A_VC + BRANCH_WIDTH
PROJ_A = A_RC + BRANCH_WIDTH
GATE_COLS = N_BRANCH * D_MODEL
PROJ_B = GATE_COLS + 2 * GLA_K
VMEM_LIMIT = 56 * 1024 * 1024


def _params(n_grid_dims):
    return pltpu.CompilerParams(dimension_semantics=("arbitrary",) * n_grid_dims,
                                vmem_limit_bytes=VMEM_LIMIT)


def _layer_norm(xf, g, b):
    mu = jnp.mean(xf, axis=-1, keepdims=True)
    xc = xf - mu
    var = jnp.mean(xc * xc, axis=-1, keepdims=True)
    return xc * lax.rsqrt(var + NORM_EPS) * g + b


def _sigmoid(x):
    return 1.0 / (1.0 + jnp.exp(-x))


def _ln0_kernel(x_ref, g_ref, b_ref, of_ref, ob_ref):
    y = _layer_norm(x_ref[...], g_ref[...], b_ref[...])
    of_ref[...] = y
    ob_ref[...] = y.astype(BF16)


def _ln0(x2, g, b, tm=512):
    t, d = x2.shape
    row = pl.BlockSpec((tm, d), lambda i: (i, 0))
    vec = pl.BlockSpec((1, d), lambda i: (0, 0))
    return pl.pallas_call(
        _ln0_kernel, grid=(t // tm,), in_specs=[row, vec, vec], out_specs=[row, row],
        out_shape=[jax.ShapeDtypeStruct((t, d), F32), jax.ShapeDtypeStruct((t, d), BF16)],
        compiler_params=_params(1), name="ln0")(x2, g.reshape(1, d), b.reshape(1, d))


def _mm_kernel(x_ref, w_ref, o_ref):
    o_ref[...] = jnp.dot(x_ref[...], w_ref[...], preferred_element_type=F32).astype(o_ref.dtype)


def _matmul(x, w, tm, tn, name):
    t, k = x.shape
    n = w.shape[1]
    return pl.pallas_call(
        _mm_kernel, grid=(t // tm, n // tn),
        in_specs=[pl.BlockSpec((tm, k), lambda i, j: (i, 0)), pl.BlockSpec((k, tn), lambda i, j: (0, j))],
        out_specs=pl.BlockSpec((tm, tn), lambda i, j: (i, j)),
        out_shape=jax.ShapeDtypeStruct((t, n), BF16),
        compiler_params=_params(2), name=name)(x, w)


def _split3(x):
    hi = x.astype(BF16).astype(F32)
    mid = (x - hi).astype(BF16).astype(F32)
    lo = (x - hi - mid).astype(BF16).astype(F32)
    return hi, mid, lo


def _attn_kernel(lam_ref, g_ref, q_ref, k_ref, v_ref, o_ref, ka_ref, vb_ref, lhs_ref, fix_ref, m_ref, acc_ref,
                 *, tq, tk, rc, seq, lam_init):
    h = pl.program_id(1)
    qi = pl.program_id(2)
    dh = DA_HEAD_DIM

    slopes = [2.0 ** (-ALIBI_MAX_EXP * (i + 1) / DA_HEADS) for i in range(DA_HEADS)]
    slope = jnp.float32(slopes[-1])
    for i in range(DA_HEADS - 1):
        slope = jnp.where(h == i, jnp.float32(slopes[i]), slope)
    sigma = slope * LOG2E

    @pl.when(qi == 0)
    def _():
        ka_ref[:LANES, :] = k_ref[0].astype(F32).T.astype(BF16)
        t_hi, t_mid, t_lo = _split3(lax.broadcasted_iota(jnp.int32, (1, seq), 1).astype(F32) * sigma)
        sub_k = lax.broadcasted_iota(jnp.int32, (LANES, seq), 0)
        aug_k = jnp.where(sub_k < 3, 1.0, jnp.where(sub_k == 3, t_hi, jnp.where(sub_k == 4, t_mid,
                          jnp.where(sub_k == 5, t_lo, 0.0))))
        ka_ref[LANES:, :] = aug_k.astype(BF16)
        lane_v = lax.broadcasted_iota(jnp.int32, (seq, LANES), 1)
        vb_ref[:, :LANES] = v_ref[0].astype(BF16)
        vb_ref[:, LANES:] = jnp.where(lane_v == 0, 1.0, 0.0).astype(BF16)
        for r in range(tk // rc):
            delta = (r * rc + lax.broadcasted_iota(jnp.int32, (rc, 1), 0)
                     - lax.broadcasted_iota(jnp.int32, (1, tk), 1)).astype(F32)
            fix_ref[r] = (2.0 * sigma) * jnp.minimum(delta, 0.0)

    q = q_ref[0].astype(F32) * (dh ** -0.5 * LOG2E)
    lane = lax.broadcasted_iota(jnp.int32, (tq, LANES), 1)
    row = lax.broadcasted_iota(jnp.int32, (tq, 1), 0)
    u_hi, u_mid, u_lo = _split3((qi * tq + row).astype(F32) * (-sigma))
    aug_q = jnp.where(lane == 0, u_hi, jnp.where(lane == 1, u_mid, jnp.where(lane == 2, u_lo,
                      jnp.where(lane < 6, 1.0, 0.0))))
    for mp in range(2):
        qz = jnp.where(lane < dh, q, 0.0) if mp == 0 else jnp.where(lane >= dh, q, 0.0)
        lhs_ref[mp, 0] = jnp.concatenate([qz, aug_q], axis=-1).astype(BF16)
        lhs_ref[mp, 1] = jnp.concatenate([qz, -aug_q], axis=-1).astype(BF16)
    m_ref[...] = jnp.full(m_ref.shape, -1e30, F32)
    acc_ref[...] = jnp.zeros(acc_ref.shape, F32)

    def block(j, local):
        off = pl.multiple_of(j * tk, tk)
        ka = ka_ref[:, pl.ds(off, tk)]
        vb = vb_ref[pl.ds(off, tk), :]
        for mp in range(2):
            for c in range(tq // rc):
                rows = slice(c * rc, (c + 1) * rc)
                if isinstance(local, str):
                    sign, fixup = (0 if local == "before" else 1), None
                elif (c + 1) * rc <= local * tk:
                    sign, fixup = 1, None
                elif c * rc >= (local + 1) * tk:
                    sign, fixup = 0, None
                else:
                    sign, fixup = 0, fix_ref[(c * rc - local * tk) // rc]
                s = jnp.dot(lhs_ref[mp, sign, rows, :], ka, preferred_element_type=F32)
                if fixup is not None:
                    s = s + fixup
                m_old = m_ref[mp, rows, :]
                m_new = jnp.maximum(m_old, jnp.max(s, axis=-1, keepdims=True))
                alpha = jnp.exp2(m_old - m_new)
                p = jnp.exp2(s - m_new).astype(BF16)
                acc_ref[mp, rows, :] = alpha * acc_ref[mp, rows, :] + jnp.dot(p, vb, preferred_element_type=F32)
                m_ref[mp, rows, :] = m_new

    def run(lo, hi, where):
        def body(j, carry):
            block(j, where)
            return carry
        lax.fori_loop(lo, hi, body, 0)

    j0 = (qi * tq) // tk
    run(0, j0, "before")
    for local in range(tq // tk):
        block(j0 + local, local)
    run(j0 + tq // tk, seq // tk, "after")

    lv = lam_ref[...]
    lam = (jnp.exp(jnp.sum(lv[0:1] * lv[1:2], axis=-1, keepdims=True))
           - jnp.exp(jnp.sum(lv[2:3] * lv[3:4], axis=-1, keepdims=True)) + lam_init)
    o1 = acc_ref[0, :, :LANES] / acc_ref[0, :, LANES:LANES + 1]
    o2 = acc_ref[1, :, :LANES] / acc_ref[1, :, LANES:LANES + 1]
    o = o1 - lam * o2
    ms = jnp.mean(o * o, axis=-1, keepdims=True)
    o_ref[0] = o * lax.rsqrt(ms + NORM_EPS) * g_ref[...] * (1.0 - lam_init)


def _diff_attention(proj_a3, da_lambda, da_norm_g, lam_init, tq=2048, tk=1024, rc=256):
    b, s, _ = proj_a3.shape
    assert tq % tk == 0 and s % tq == 0 and tk % rc == 0
    kern = functools.partial(_attn_kernel, tq=tq, tk=tk, rc=rc, seq=s, lam_init=lam_init)
    return pl.pallas_call(
        kern, grid=(b, DA_HEADS, s // tq),
        in_specs=[
            pl.BlockSpec((4, DA_HEAD_DIM), lambda bi, h, i: (0, 0)),
            pl.BlockSpec((1, DA_V_DIM), lambda bi, h, i: (0, 0)),
            pl.BlockSpec((1, tq, LANES), lambda bi, h, i: (bi, i, h)),
            pl.BlockSpec((1, s, LANES), lambda bi, h, i: (bi, 0, DA_HEADS + h)),
            pl.BlockSpec((1, s, LANES), lambda bi, h, i: (bi, 0, 2 * DA_HEADS + h)),
        ],
        out_specs=pl.BlockSpec((1, tq, LANES), lambda bi, h, i: (bi, i, h)),
        out_shape=jax.ShapeDtypeStruct((b, s, BRANCH_WIDTH), F32),
        scratch_shapes=[pltpu.VMEM((2 * LANES, s), BF16), pltpu.VMEM((s, 2 * LANES), BF16),
                        pltpu.VMEM((2, 2, tq, 2 * LANES), BF16), pltpu.VMEM((tk // rc, rc, tk), F32),
                        pltpu.VMEM((2, tq, 1), F32), pltpu.VMEM((2, tq, 2 * LANES), F32)],
        compiler_params=_params(3), name="diff_attention")(
            da_lambda, da_norm_g.reshape(1, DA_V_DIM), proj_a3, proj_a3, proj_a3)


def _s5_matrices(a_re, a_im, log_dt, b_re, b_im, c_re, c_im):
    L, C, P, G = S5_CHUNK, S5_GROUP_SIZE, S5_STATE, S5_GROUPS
    f = lambda t: t.astype(F32)
    a_re, a_im, b_re, b_im, c_re, c_im = map(f, (a_re, a_im, b_re, b_im, c_re, c_im))
    dt = jnp.exp(f(log_dt))[..., None]
    lr, li = a_re * dt, a_im * dt

    def apow(j):
        jj = j.astype(F32)[:, None, None, None]
        mag = jnp.exp(lr[None] * jj)
        return mag * jnp.cos(li[None] * jj), mag * jnp.sin(li[None] * jj)

    a1r, a1i = apow(jnp.arange(1, 2))
    nr, ni = a1r[0] - 1.0, a1i[0]
    den = a_re * a_re + a_im * a_im
    cr, ci = (nr * a_re + ni * a_im) / den, (ni * a_re - nr * a_im) / den
    bbr = cr[..., None] * b_re - ci[..., None] * b_im
    bbi = cr[..., None] * b_im + ci[..., None] * b_re

    pr, pi = apow(jnp.arange(L))
    wr = pr[..., None] * bbr[None] - pi[..., None] * bbi[None]
    wi = pr[..., None] * bbi[None] + pi[..., None] * bbr[None]
    kern = (jnp.einsum('dgcp,jdgpe->jdgce', c_re, wr, precision=HIGHEST)
            - jnp.einsum('dgcp,jdgpe->jdgce', c_im, wi, precision=HIGHEST))

    s_idx = jnp.arange(L)[:, None]
    t_idx = jnp.arange(L)[None, :]

    def toeplitz(kd, lag, valid):
        m = kd[jnp.clip(lag, 0, L - 1)] * valid[:, :, None, None, None].astype(F32)
        return m.transpose(2, 0, 4, 1, 3).reshape(G, L * C, L * C)

    m_mat = jnp.stack([toeplitz(kern[:, 0], t_idx - s_idx, t_idx >= s_idx),
                       toeplitz(kern[:, 1], s_idx - t_idx, s_idx >= t_idx)])

    def summary(w_dir):
        return w_dir.transpose(1, 0, 3, 2).reshape(G, L * C, P)

    p_re = jnp.stack([summary(wr[::-1, 0]), summary(wr[:, 1])])
    p_im = jnp.stack([summary(wi[::-1, 0]), summary(wi[:, 1])])

    qr, qi = apow(jnp.arange(1, L + 1))
    car = c_re[None] * qr[:, :, :, None, :] - c_im[None] * qi[:, :, :, None, :]
    cai = c_re[None] * qi[:, :, :, None, :] + c_im[None] * qr[:, :, :, None, :]

    def readout(ca_dir):
        return ca_dir.transpose(1, 3, 0, 2).reshape(G, P, L * C)

    q_re = jnp.stack([readout(car[:, 0]), readout(car[::-1, 1])])
    q_im = jnp.stack([-readout(cai[:, 0]), -readout(cai[::-1, 1])])

    half = jax.nn.one_hot(jnp.arange(G) % 2, 2, dtype=F32)
    p_pad = jnp.stack([p_re, p_im], axis=-2)[:, :, :, :, None, :] * half[None, :, None, None, :, None]
    p_pad = p_pad.reshape(2, G, L * C, 4 * P)
    q_pad = jnp.stack([q_re, q_im], axis=2)[:, :, :, None] * half[None, :, None, :, None, None]
    q_pad = q_pad.reshape(2, G, 4 * P, L * C)

    alr, ali = apow(jnp.arange(L, L + 1))
    a_l = jnp.concatenate([alr[0].reshape(2, G // 2, 2 * P), ali[0].reshape(2, G // 2, 2 * P)], axis=-1)
    return m_mat.astype(BF16), p_pad.astype(BF16), q_pad.astype(BF16), a_l.reshape(2, G // 2, 1, 4 * P)


def _s5_kernel(x_ref, m_ref, p_ref, q_ref, al_ref, d_ref, y_ref, s_ref, xin_ref, *, n_chunks, bsz):
    hp = 2 * S5_STATE
    xb = [x_ref[g].astype(BF16) for g in range(2)]
    for d in range(2):
        s = (jnp.dot(xb[0], p_ref[d, 0], preferred_element_type=F32)
             + jnp.dot(xb[1], p_ref[d, 1], preferred_element_type=F32))
        s_ref[2 * d] = s[:, :hp]
        s_ref[2 * d + 1] = s[:, hp:]

    al = [al_ref[d, 0] for d in range(2)]
    alr = [a[:, :hp] for a in al]
    ali = [a[:, hp:] for a in al]

    def body(i, carry):
        new = []
        for d in range(2):
            re, im = carry[2 * d], carry[2 * d + 1]
            n = i if d == 0 else n_chunks - 1 - i
            rows = pl.ds(n, bsz, stride=n_chunks)
            xin_ref[2 * d, rows, :] = re
            xin_ref[2 * d + 1, rows, :] = im
            new.append(alr[d] * re - ali[d] * im + s_ref[2 * d, rows, :])
            new.append(alr[d] * im + ali[d] * re + s_ref[2 * d + 1, rows, :])
        return tuple(new)

    zero = jnp.zeros((bsz, hp), F32)
    lax.fori_loop(0, n_chunks, body, (zero, zero, zero, zero))

    xin = [jnp.concatenate([xin_ref[2 * d], xin_ref[2 * d + 1]], axis=-1).astype(BF16) for d in range(2)]
    for g in range(2):
        y = x_ref[g] * d_ref[g]
        for d in range(2):
            y = y + jnp.dot(xb[g], m_ref[d, g], preferred_element_type=F32)
            y = y + jnp.dot(xin[d], q_ref[d, g], preferred_element_type=F32)
        y_ref[g] = y


S5_PER_VREG = LANES // S5_GROUP_SIZE


def _s5_proj_kernel(x_ref, w_ref, o_ref, tok_ref):
    c = S5_GROUP_SIZE
    rows = o_ref.shape[1]
    res = jnp.dot(x_ref[...], w_ref[...], preferred_element_type=F32)
    for j in range(tok_ref.shape[0]):
        tok_ref[j] = res[:, j * LANES:(j + 1) * LANES]
    for j in range(tok_ref.shape[0]):
        for t in range(S5_CHUNK):
            blk = tok_ref[j, pl.ds(t, rows, stride=S5_CHUNK), :]
            for gg in range(S5_PER_VREG):
                o_ref[j * S5_PER_VREG + gg, :, t * c:(t + 1) * c] = blk[:, gg * c:(gg + 1) * c]


def _s5_unpack_kernel(y_ref, o_ref, tok_ref):
    c = S5_GROUP_SIZE
    rows = y_ref.shape[1]
    for j in range(tok_ref.shape[0]):
        for t in range(S5_CHUNK):
            blk = jnp.concatenate([y_ref[j * S5_PER_VREG + gg, :, t * c:(t + 1) * c] for gg in range(S5_PER_VREG)],
                                  axis=-1)
            tok_ref[j, pl.ds(t, rows, stride=S5_CHUNK), :] = blk
    for j in range(tok_ref.shape[0]):
        o_ref[:, j * LANES:(j + 1) * LANES] = tok_ref[j]


def _s5_proj(xb, w_u, tm=256):
    t, d = xb.shape
    rows = t // S5_CHUNK
    lc = S5_CHUNK * S5_GROUP_SIZE
    w = w_u.shape[1]
    return pl.pallas_call(
        _s5_proj_kernel, grid=(rows // tm,),
        in_specs=[pl.BlockSpec((tm * S5_CHUNK, d), lambda i: (i, 0)), pl.BlockSpec(w_u.shape, lambda i: (0, 0))],
        out_specs=pl.BlockSpec((S5_GROUPS, tm, lc), lambda i: (0, i, 0)),
        out_shape=jax.ShapeDtypeStruct((S5_GROUPS, rows, lc), F32),
        scratch_shapes=[pltpu.VMEM((w // LANES, tm * S5_CHUNK, LANES), F32)],
        compiler_params=_params(1), name="s5_proj")(xb, w_u)


def _s5(x, mats, d_skip, b, s):
    L, C, G = S5_CHUNK, S5_GROUP_SIZE, S5_GROUPS
    w = G * C
    n = s // L
    lc = L * C
    m_mat, p_pad, q_pad, a_l = mats
    d_row = jnp.tile(d_skip.astype(F32).reshape(G, 1, C), (1, L, 1)).reshape(G, 1, lc)
    mat_spec = pl.BlockSpec((2, 2, lc, lc), lambda k: (0, k, 0, 0))
    kern = functools.partial(_s5_kernel, n_chunks=n, bsz=b)
    y = pl.pallas_call(
        kern, grid=(G // 2,),
        in_specs=[pl.BlockSpec((2, n * b, lc), lambda k: (k, 0, 0)), mat_spec, mat_spec, mat_spec,
                  pl.BlockSpec((2, 1, 1, lc), lambda k: (0, k, 0, 0)),
                  pl.BlockSpec((2, 1, lc), lambda k: (k, 0, 0))],
        out_specs=pl.BlockSpec((2, n * b, lc), lambda k: (k, 0, 0)),
        out_shape=jax.ShapeDtypeStruct((G, n * b, lc), F32),
        scratch_shapes=[pltpu.VMEM((4, n * b, LANES), F32), pltpu.VMEM((4, n * b, LANES), F32)],
        compiler_params=_params(1), name="s5")(x, m_mat, p_pad, q_pad, a_l, d_row)
    tm = 256
    return pl.pallas_call(
        _s5_unpack_kernel, grid=(b * n // tm,),
        in_specs=[pl.BlockSpec((G, tm, lc), lambda i: (0, i, 0))],
        out_specs=pl.BlockSpec((tm * L, w), lambda i: (i, 0)),
        out_shape=jax.ShapeDtypeStruct((b * s, w), F32),
        scratch_shapes=[pltpu.VMEM((w // LANES, tm * L, LANES), F32)],
        compiler_params=_params(1), name="s5_unpack")(y)


def _gla_kernel(qf_ref, kf_ref, vf_ref, gf_ref, qb_ref, kb_ref, vb_ref, gb_ref, bg_ref,
                of_ref, ob_ref, st_ref, *, rows):
    c = GLA_CHUNK
    n_sub = rows // c

    @pl.when(pl.program_id(1) == 0)
    def _():
        st_ref[...] = jnp.zeros_like(st_ref)

    nt = (((1,), (1,)), ((), ()))
    tn = (((0,), (0,)), ((), ()))
    log2 = lambda n: int(math.log2(n))
    r_r = lax.broadcasted_iota(jnp.int32, (rows, rows), 0)
    c_r = lax.broadcasted_iota(jnp.int32, (rows, rows), 1)
    same_chunk = (r_r >> log2(c)) == (c_r >> log2(c))
    t_row = lax.broadcasted_iota(jnp.int32, (c, GLA_K), 0)
    t_col = lax.broadcasted_iota(jnp.int32, (c, GLA_K), 1) & (c - 1)
    k_lane = lax.broadcasted_iota(jnp.int32, (1, GLA_K), 1) >> log2(GLA_DK)
    v_lane = lax.broadcasted_iota(jnp.int32, (1, BRANCH_WIDTH), 1) >> log2(GLA_DV)
    st_same_head = ((lax.broadcasted_iota(jnp.int32, (BRANCH_WIDTH, GLA_K), 0) >> log2(GLA_DV))
                    == (lax.broadcasted_iota(jnp.int32, (BRANCH_WIDTH, GLA_K), 1) >> log2(GLA_DK)))

    for d, (q_ref, k_ref, v_ref, g_ref, o_ref) in enumerate(
            ((qf_ref, kf_ref, vf_ref, gf_ref, of_ref), (qb_ref, kb_ref, vb_ref, gb_ref, ob_ref))):
        gate = g_ref[0].astype(F32) + bg_ref[d]
        log_a = (jnp.minimum(gate, 0.0) - jnp.log(1.0 + jnp.exp(-jnp.abs(gate)))) * (1.0 / GLA_TAU)
        if d == 0:
            tri = jnp.where(same_chunk, jnp.where(r_r >= c_r, 1.0, 0.0), 0.0).astype(BF16)
            keep, i_ref, i_last = (t_row >= t_col), c // 2, c - 1
        else:
            tri = jnp.where(same_chunk, jnp.where(c_r >= r_r, 1.0, 0.0), 0.0).astype(BF16)
            keep, i_ref, i_last = (t_col > t_row), c - 1 - c // 2, 0
        la_hi = log_a.astype(BF16)
        la_lo = (log_a - la_hi.astype(F32)).astype(BF16)
        cum_all = (jnp.dot(tri, la_hi, preferred_element_type=F32) + jnp.dot(tri, la_lo, preferred_element_type=F32))
        q = q_ref[0].astype(F32) * (GLA_DK ** -0.5)
        k = k_ref[0].astype(F32)
        v = v_ref[0]
        for ci in (range(n_sub) if d == 0 else range(n_sub - 1, -1, -1)):
            sl = slice(ci * c, (ci + 1) * c)
            cum = cum_all[sl]
            mid = cum[i_ref:i_ref + 1]
            last = cum[i_last:i_last + 1]
            qc, kc, vc = q[sl], k[sl], v[sl]
            qe = (qc * jnp.exp(cum - mid)).astype(BF16)
            ke = kc * jnp.exp(mid - cum)
            kd = (kc * jnp.exp(last - cum)).astype(BF16)
            qs = (qc * jnp.exp(cum)).astype(BF16)
            decay = jnp.exp(last)
            ke4 = jnp.concatenate([jnp.where(k_lane == h, ke, 0.0) for h in range(GLA_HEADS)], axis=0).astype(BF16)
            v4 = jnp.concatenate([jnp.where(v_lane == h, vc, 0.0) for h in range(GLA_HEADS)], axis=0).astype(BF16)
            sc = lax.dot_general(qe, ke4, nt, preferred_element_type=F32)
            sc = jnp.where(keep, sc, 0.0).astype(BF16)
            st = st_ref[d]
            o = (jnp.dot(sc, v4, preferred_element_type=F32)
                 + lax.dot_general(qs, st.astype(BF16), nt, preferred_element_type=F32))
            o_ref[0, sl, :] = o
            ds = lax.dot_general(vc.astype(BF16), kd, tn, preferred_element_type=F32)
            st_ref[d] = decay * st + jnp.where(st_same_head, ds, 0.0)


def _gla(proj_a3, proj_b3, b_gate, rows=128):
    b, s, _ = proj_a3.shape
    n = s // rows
    bg = b_gate.astype(F32).reshape(2, 1, GLA_K)

    def specs(rev):
        blk = (lambda i: n - 1 - i) if rev else (lambda i: i)
        return [pl.BlockSpec((1, rows, GLA_K), lambda bi, i: (bi, blk(i), A_QC // GLA_K)),
                pl.BlockSpec((1, rows, GLA_K), lambda bi, i: (bi, blk(i), A_KC // GLA_K)),
                pl.BlockSpec((1, rows, BRANCH_WIDTH), lambda bi, i: (bi, blk(i), A_VC // BRANCH_WIDTH)),
                pl.BlockSpec((1, rows, GLA_K), lambda bi, i: (bi, blk(i), GATE_COLS // GLA_K + int(rev)))]

    out_f = pl.BlockSpec((1, rows, BRANCH_WIDTH), lambda bi, i: (bi, i, 0))
    out_b = pl.BlockSpec((1, rows, BRANCH_WIDTH), lambda bi, i: (bi, n - 1 - i, 0))
    shape = jax.ShapeDtypeStruct((b, s, BRANCH_WIDTH), F32)
    return pl.pallas_call(
        functools.partial(_gla_kernel, rows=rows), grid=(b, n),
        in_specs=specs(False) + specs(True) + [pl.BlockSpec((2, 1, GLA_K), lambda bi, i: (0, 0, 0))],
        out_specs=[out_f, out_b], out_shape=[shape, shape],
        scratch_shapes=[pltpu.VMEM((2, BRANCH_WIDTH, GLA_K), F32)],
        compiler_params=_params(2), name="gla")(
            proj_a3, proj_a3, proj_a3, proj_b3, proj_a3, proj_a3, proj_a3, proj_b3, bg)


def _ranks(vals):
    ranks = []
    for i, v in enumerate(vals):
        r = jnp.zeros(v.shape, F32)
        for o, other in enumerate(vals):
            if o != i:
                r = r + (jnp.where(other >= v, 1.0, 0.0) if o < i else jnp.where(other > v, 1.0, 0.0))
        ranks.append(r)
    return ranks


def _route(scores_t, bias_col):
    gs = EXPERTS_PER_GROUP
    s = [scores_t[e:e + 1] for e in range(N_EXPERTS)]
    biased = [s[e] + bias_col[e:e + 1] for e in range(N_EXPERTS)]
    top2, group_score = [], []
    for g in range(N_EXPERTS // gs):
        rk = _ranks(biased[g * gs:(g + 1) * gs])
        top2 += [r < 2.0 for r in rk]
        kept = [jnp.where(top2[g * gs + j], biased[g * gs + j], 0.0) for j in range(gs)]
        group_score.append(functools.reduce(lambda a, b: a + b, kept))
    best = [r < 1.0 for r in _ranks(group_score)]
    w = [jnp.where(top2[e], jnp.where(best[e // gs], s[e], 0.0), 0.0) for e in range(N_EXPERTS)]
    total = functools.reduce(lambda a, b: a + b, w)
    return [we / total for we in w]


def _merge_kernel(x_ref, oa_ref, ys_ref, gf_ref, gb_ref, r_ref, gl0_ref, gl1_ref, gl2_ref,
                  wglu_ref, bglu_ref, gng_ref, wup_ref, mb_ref, wout_ref, lng_ref, lnb_ref, rw_ref, rb_ref,
                  xo_ref, xob_ref, comb_ref):
    y = ys_ref[...]
    gelu = 0.5 * y * (1.0 + jnp.tanh(math.sqrt(2.0 / math.pi) * (y + 0.044715 * (y * y * y))))
    o_b = gelu * _sigmoid(jnp.dot(gelu.astype(BF16), wglu_ref[...], preferred_element_type=F32) + bglu_ref[...])

    r = r_ref[...].astype(F32)
    gated = r * _sigmoid(r)
    heads = []
    for h in range(GLA_HEADS):
        sl = slice(h * GLA_DV, (h + 1) * GLA_DV)
        o = gf_ref[:, sl] + gb_ref[:, sl]
        ms = jnp.mean(o * o, axis=-1, keepdims=True)
        heads.append(o * lax.rsqrt(ms + NORM_EPS) * gng_ref[...] * gated[:, sl])
    o_c = jnp.concatenate(heads, axis=-1)

    merged = None
    for n, (o_n, gl_ref) in enumerate(((oa_ref[...], gl0_ref), (o_b, gl1_ref), (o_c, gl2_ref))):
        gate = _sigmoid(gl_ref[...].astype(F32) + mb_ref[n])
        term = gate * jnp.dot(o_n.astype(BF16), wup_ref[n], preferred_element_type=F32)
        merged = term if merged is None else merged + term
    mix = jnp.dot(merged.astype(BF16), wout_ref[...], preferred_element_type=F32)
    x1 = _layer_norm(DN_ALPHA * x_ref[...] + mix, lng_ref[...], lnb_ref[...])
    xo_ref[...] = x1
    x_hi = x1.astype(BF16)
    xob_ref[...] = x_hi

    x_lo = (x1 - x_hi.astype(F32)).astype(BF16)
    logits = (jnp.dot(x_hi, rw_ref[0], preferred_element_type=F32) + jnp.dot(x_lo, rw_ref[0], preferred_element_type=F32)
              + jnp.dot(x_hi, rw_ref[1], preferred_element_type=F32))
    rows = _route(_sigmoid(logits.T[:N_EXPERTS]), rb_ref[...])
    comb_ref[...] = jnp.zeros(comb_ref.shape, F32)
    for e in range(N_EXPERTS):
        comb_ref[e:e + 1, :] = rows[e]


def _merge(x, o_a, y_s5, g_f, g_b, proj_a, proj_b, wglu, bglu, gng, wup, mb, wout, lng, lnb, rw, rb, tm=256):
    t, d = x.shape
    w = BRANCH_WIDTH
    row = lambda width, cb=0: pl.BlockSpec((tm, width), lambda i: (i, cb))
    full = lambda a: pl.BlockSpec(a.shape, lambda i: (0,) * a.ndim)
    weights = (wglu, bglu, gng, wup, mb, wout, lng, lnb, rw, rb)
    return pl.pallas_call(
        _merge_kernel, grid=(t // tm,),
        in_specs=[row(d), row(w), row(w), row(w), row(w), row(w, A_RC // w), row(d, 0), row(d, 1), row(d, 2)]
        + [full(a) for a in weights],
        out_specs=[row(d), row(d), pl.BlockSpec((LANES, tm), lambda i: (0, i))],
        out_shape=[jax.ShapeDtypeStruct((t, d), F32), jax.ShapeDtypeStruct((t, d), BF16),
                   jax.ShapeDtypeStruct((LANES, t), F32)],
        compiler_params=_params(1), name="merge")(
            x, o_a, y_s5, g_f, g_b, proj_a, proj_b, proj_b, proj_b, *weights)


MOE_SUB = 256
MOE_CAP = 64


def _swiglu(x, wg_ref, wu_ref, wd_ref, row_scale):
    g = jnp.dot(x, wg_ref[0], preferred_element_type=F32)
    u = jnp.dot(x, wu_ref[0], preferred_element_type=F32)
    h = (g * _sigmoid(g)) * u * row_scale
    return jnp.dot(h.astype(BF16), wd_ref[0], preferred_element_type=F32)


def _moe_kernel(xb_ref, xf_ref, comb_ref, wg_ref, wu_ref, wd_ref, lng_ref, lnb_ref, of_ref, ob_ref,
                pos_ref, ind_ref, sel_ref, xg_ref, y_ref, cg_ref, acc_ref, flag_ref, *, tm, cap):
    e = pl.program_id(1)
    ns = tm // MOE_SUB
    gs = EXPERTS_PER_GROUP
    sub = lambda k: slice(k * MOE_SUB, (k + 1) * MOE_SUB)
    slots = lambda k: slice(k * cap, (k + 1) * cap)
    r_i = lax.broadcasted_iota(jnp.int32, (MOE_SUB, MOE_SUB), 0)
    c_i = lax.broadcasted_iota(jnp.int32, (MOE_SUB, MOE_SUB), 1)

    @pl.when(e == 0)
    def _():
        ind = jnp.where(comb_ref[...] > 0.0, 1.0, 0.0)
        ind_ref[...] = ind
        before = jnp.where(r_i < c_i, 1.0, 0.0).astype(BF16)
        most = jnp.zeros((LANES, 1), F32)
        for k in range(ns):
            pos_ref[:, sub(k)] = jnp.dot(ind[:, sub(k)].astype(BF16), before, preferred_element_type=F32)
            most = jnp.maximum(most, jnp.sum(ind[:, sub(k)], axis=-1, keepdims=True))
        flag_ref[0] = (jnp.max(most) > cap).astype(jnp.int32)
        acc_ref[...] = jnp.zeros_like(acc_ref)

    slot_id = lax.broadcasted_iota(jnp.int32, (cap, 1), 0).astype(F32)
    for grp in range(N_EXPERTS // gs):
        @pl.when(e == grp * gs)
        def _(grp=grp):
            for k in range(ns):
                pieces = []
                for j in range(gs):
                    ee = grp * gs + j
                    one_hot = jnp.where(pos_ref[ee:ee + 1, sub(k)] == slot_id, ind_ref[ee:ee + 1, sub(k)], 0.0)
                    cg_ref[j, slots(k), :] = jnp.sum(one_hot * comb_ref[ee:ee + 1, sub(k)], axis=-1, keepdims=True)
                    pieces.append(one_hot)
                sel = jnp.concatenate(pieces, axis=0).astype(BF16)
                sel_ref[k] = sel
                xg = jnp.dot(sel, xb_ref[sub(k), :], preferred_element_type=F32).astype(BF16)
                for j in range(gs):
                    xg_ref[j, slots(k), :] = xg[j * cap:(j + 1) * cap]

    j = e % gs
    y_ref[j] = _swiglu(xg_ref[j], wg_ref, wu_ref, wd_ref, cg_ref[j]).astype(BF16)

    @pl.when(j == gs - 1)
    def _():
        for k in range(ns):
            yk = jnp.concatenate([y_ref[jj, slots(k), :] for jj in range(gs)], axis=0)
            acc_ref[sub(k), :] += lax.dot_general(sel_ref[k], yk, (((0,), (0,)), ((), ())),
                                                  preferred_element_type=F32)

    @pl.when(flag_ref[0] > 0)
    def _():
        comb = comb_ref[...].T
        earlier = jnp.where(r_i > c_i, 1.0, 0.0).astype(BF16)
        over = []
        for k in range(ns):
            pos = jnp.dot(earlier, jnp.where(comb[sub(k)] > 0.0, 1.0, 0.0).astype(BF16), preferred_element_type=F32)
            over.append(jnp.where(pos >= cap, comb[sub(k)], 0.0))
        over = jnp.concatenate(over, axis=0)
        lane = lax.broadcasted_iota(jnp.int32, over.shape, 1)
        ce = jnp.sum(jnp.where(lane == e, over, 0.0), axis=-1, keepdims=True)
        acc_ref[...] += _swiglu(xb_ref[...], wg_ref, wu_ref, wd_ref, ce)

    @pl.when(e == N_EXPERTS - 1)
    def _():
        y = _layer_norm(DN_ALPHA * xf_ref[...] + acc_ref[...], lng_ref[...], lnb_ref[...])
        of_ref[...] = y
        ob_ref[...] = y.astype(BF16)


def _moe(xb, xf, comb, wg, wu, wd, lng, lnb, tm=1024, cap=MOE_CAP):
    t, d = xf.shape
    ns = tm // MOE_SUB
    gs = EXPERTS_PER_GROUP
    row = lambda width: pl.BlockSpec((tm, width), lambda i, e: (i, 0))
    vec = pl.BlockSpec((1, d), lambda i, e: (0, 0))
    return pl.pallas_call(
        functools.partial(_moe_kernel, tm=tm, cap=cap), grid=(t // tm, N_EXPERTS),
        in_specs=[row(d), row(d), pl.BlockSpec((LANES, tm), lambda i, e: (0, i)),
                  pl.BlockSpec((1, d, EXPERT_FF), lambda i, e: (e, 0, 0)),
                  pl.BlockSpec((1, d, EXPERT_FF), lambda i, e: (e, 0, 0)),
                  pl.BlockSpec((1, EXPERT_FF, d), lambda i, e: (e, 0, 0)), vec, vec],
        out_specs=[row(d), row(d)],
        out_shape=[jax.ShapeDtypeStruct((t, d), F32), jax.ShapeDtypeStruct((t, d), BF16)],
        scratch_shapes=[pltpu.VMEM((LANES, tm), F32), pltpu.VMEM((LANES, tm), F32),
                        pltpu.VMEM((ns, gs * cap, MOE_SUB), BF16), pltpu.VMEM((gs, ns * cap, d), BF16),
                        pltpu.VMEM((gs, ns * cap, d), BF16), pltpu.VMEM((gs, ns * cap, 1), F32),
                        pltpu.VMEM((tm, d), F32), pltpu.SMEM((1,), jnp.int32)],
        compiler_params=_params(2), name="moe")(xb, xf, comb, wg, wu, wd, lng, lnb)


def _proj_b_weights(w_in_l, w_gate):
    z0 = W_IN_Z
    folded = [jnp.dot(w_in_l[:, z0 + i * GLA_GATE_RANK:z0 + (i + 1) * GLA_GATE_RANK].astype(F32),
                      w_gate[i].astype(F32), precision=HIGHEST) for i in range(2)]
    return jnp.concatenate([w_in_l[:, z0 + 2 * GLA_GATE_RANK:].astype(F32)] + folded, axis=1).astype(BF16)


def kernel(x, ln0_g, ln0_b, w_in, da_lambda, da_norm_g, s5_a_re, s5_a_im, s5_log_dt, s5_b_re, s5_b_im,
           s5_c_re, s5_c_im, s5_d, s5_w_glu, s5_b_glu, gla_w_gate, gla_b_gate, gla_norm_g, merge_w_up,
           merge_b, w_out, ln1_g, ln1_b, router_w, router_bias, moe_w_gate, moe_w_up, moe_w_down,
           ln2_g, ln2_b):
    b, s, d = x.shape
    t = b * s
    vec = lambda a: a.astype(F32).reshape(1, -1)
    xf, xb = _ln0(x.reshape(t, d), ln0_g.astype(F32), ln0_b.astype(F32))
    rw32 = jnp.zeros((d, LANES), F32).at[:, :N_EXPERTS].set(router_w.astype(F32))
    rw_hi = rw32.astype(BF16)
    rw = jnp.stack([rw_hi, (rw32 - rw_hi.astype(F32)).astype(BF16)])
    rb = router_bias.astype(F32).reshape(N_EXPERTS, 1)
    for l in range(DEPTH):
        w_a = jnp.concatenate([w_in[l][:, :W_IN_U], w_in[l][:, W_IN_GLA:W_IN_Z]], axis=1).astype(BF16)
        w_u = w_in[l][:, W_IN_U:W_IN_GLA].astype(BF16)
        w_b = _proj_b_weights(w_in[l], gla_w_gate[l])
        proj_a = _matmul(xb, w_a, 1024, PROJ_A // 2, "in_proj_a")
        proj_b = _matmul(xb, w_b, 512, PROJ_B, "in_proj_b")
        proj_a3 = proj_a.reshape(b, s, PROJ_A)
        proj_b3 = proj_b.reshape(b, s, PROJ_B)

        lam_init = 0.8 - 0.6 * math.exp(-0.3 * l)
        o_a = _diff_attention(proj_a3, da_lambda[l].astype(F32), da_norm_g[l].astype(F32), lam_init)
        mats = _s5_matrices(s5_a_re[l], s5_a_im[l], s5_log_dt[l], s5_b_re[l], s5_b_im[l], s5_c_re[l], s5_c_im[l])
        y_s5 = _s5(_s5_proj(xb, w_u), mats, s5_d[l], b, s)
        g_f, g_b = _gla(proj_a3, proj_b3, gla_b_gate[l])

        xf, xb, comb = _merge(
            xf, o_a.reshape(t, BRANCH_WIDTH), y_s5, g_f.reshape(t, BRANCH_WIDTH), g_b.reshape(t, BRANCH_WIDTH),
            proj_a, proj_b, s5_w_glu[l].astype(BF16), vec(s5_b_glu[l]), vec(gla_norm_g[l]),
            merge_w_up[l].astype(BF16), merge_b[l].astype(F32).reshape(N_BRANCH, 1, d), w_out[l].astype(BF16),
            vec(ln1_g[l]), vec(ln1_b[l]), rw, rb)
        xf, xb = _moe(xb, xf, comb, moe_w_gate[l].astype(BF16), moe_w_up[l].astype(BF16),
                      moe_w_down[l].astype(BF16), vec(ln2_g[l]), vec(ln2_b[l]))
    return xf.reshape(b, s, d)
```

```python
import functools
import math

import jax
import jax.numpy as jnp
from jax import lax
from jax.experimental import pallas as pl
from jax.experimental.pallas import tpu as pltpu

F32 = jnp.float32
BF16 = jnp.bfloat16
HIGHEST = lax.Precision.HIGHEST

D_MODEL = 1024
DEPTH = 2
BRANCH_WIDTH = D_MODEL // 2
N_BRANCH = 3

DA_HEADS = 4
DA_HEAD_DIM = BRANCH_WIDTH // (2 * DA_HEADS)
DA_V_DIM = 2 * DA_HEAD_DIM
ALIBI_MAX_EXP = 8.0

S5_GROUP_SIZE = 16
S5_GROUPS = BRANCH_WIDTH // S5_GROUP_SIZE
S5_STATE = 64
S5_CHUNK = 16

GLA_HEADS = 4
GLA_DV = BRANCH_WIDTH // GLA_HEADS
GLA_DK = GLA_DV // 2
GLA_K = GLA_HEADS * GLA_DK
GLA_GATE_RANK = 16
GLA_TAU = 16.0
GLA_CHUNK = 64

N_EXPERTS = 16
EXPERTS_PER_GROUP = 4
EXPERT_FF = D_MODEL // 2

DN_ALPHA = (2.0 * DEPTH) ** 0.25
NORM_EPS = 1e-5
LOG2E = math.log2(math.e)

LANES = 128
W_IN_U = 3 * BRANCH_WIDTH
W_IN_GLA = W_IN_U + BRANCH_WIDTH
W_IN_Z = W_IN_GLA + 2 * GLA_K + 2 * BRANCH_WIDTH
A_QC = W_IN_U
A_KC = A_QC + GLA_K
A_VC = A_KC + GLA_K
A_RC = A_VC + BRANCH_WIDTH
PROJ_A = A_RC + BRANCH_WIDTH
GATE_COLS = N_BRANCH * D_MODEL
PROJ_B = GATE_COLS + 2 * GLA_K
VMEM_LIMIT = 56 * 1024 * 1024


def _params(n_grid_dims):
    return pltpu.CompilerParams(dimension_semantics=("arbitrary",) * n_grid_dims,
                                vmem_limit_bytes=VMEM_LIMIT)


def _layer_norm(xf, g, b):
    mu = jnp.mean(xf, axis=-1, keepdims=True)
    xc = xf - mu
    var = jnp.mean(xc * xc, axis=-1, keepdims=True)
    return xc * lax.rsqrt(var + NORM_EPS) * g + b


def _sigmoid(x):
    return 1.0 / (1.0 + jnp.exp(-x))


def _ln0_kernel(x_ref, g_ref, b_ref, of_ref, ob_ref):
    y = _layer_norm(x_ref[...], g_ref[...], b_ref[...])
    of_ref[...] = y
    ob_ref[...] = y.astype(BF16)


def _ln0(x2, g, b, tm=512):
    t, d = x2.shape
    row = pl.BlockSpec((tm, d), lambda i: (i, 0))
    vec = pl.BlockSpec((1, d), lambda i: (0, 0))
    return pl.pallas_call(
        _ln0_kernel, grid=(t // tm,), in_specs=[row, vec, vec], out_specs=[row, row],
        out_shape=[jax.ShapeDtypeStruct((t, d), F32), jax.ShapeDtypeStruct((t, d), BF16)],
        compiler_params=_params(1), name="ln0")(x2, g.reshape(1, d), b.reshape(1, d))


def _mm_kernel(x_ref, w_ref, o_ref):
    o_ref[...] = jnp.dot(x_ref[...], w_ref[...], preferred_element_type=F32).astype(o_ref.dtype)


def _matmul(x, w, tm, tn, name):
    t, k = x.shape
    n = w.shape[1]
    return pl.pallas_call(
        _mm_kernel, grid=(t // tm, n // tn),
        in_specs=[pl.BlockSpec((tm, k), lambda i, j: (i, 0)), pl.BlockSpec((k, tn), lambda i, j: (0, j))],
        out_specs=pl.BlockSpec((tm, tn), lambda i, j: (i, j)),
        out_shape=jax.ShapeDtypeStruct((t, n), BF16),
        compiler_params=_params(2), name=name)(x, w)


def _split3(x):
    hi = x.astype(BF16).astype(F32)
    mid = (x - hi).astype(BF16).astype(F32)
    lo = (x - hi - mid).astype(BF16).astype(F32)
    return hi, mid, lo


def _attn_kernel(lam_ref, g_ref, q_ref, k_ref, v_ref, o_ref, ka_ref, vb_ref, lhs_ref, fix_ref, m_ref, acc_ref,
                 *, tq, tk, rc, seq, lam_init):
    h = pl.program_id(1)
    qi = pl.program_id(2)
    dh = DA_HEAD_DIM

    slopes = [2.0 ** (-ALIBI_MAX_EXP * (i + 1) / DA_HEADS) for i in range(DA_HEADS)]
    slope = jnp.float32(slopes[-1])
    for i in range(DA_HEADS - 1):
        slope = jnp.where(h == i, jnp.float32(slopes[i]), slope)
    sigma = slope * LOG2E

    @pl.when(qi == 0)
    def _():
        ka_ref[:LANES, :] = k_ref[0].astype(F32).T.astype(BF16)
        t_hi, t_mid, t_lo = _split3(lax.broadcasted_iota(jnp.int32, (1, seq), 1).astype(F32) * sigma)
        sub_k = lax.broadcasted_iota(jnp.int32, (LANES, seq), 0)
        aug_k = jnp.where(sub_k < 3, 1.0, jnp.where(sub_k == 3, t_hi, jnp.where(sub_k == 4, t_mid,
                          jnp.where(sub_k == 5, t_lo, 0.0))))
        ka_ref[LANES:, :] = aug_k.astype(BF16)
        lane_v = lax.broadcasted_iota(jnp.int32, (seq, LANES), 1)
        vb_ref[:, :LANES] = v_ref[0].astype(BF16)
        vb_ref[:, LANES:] = jnp.where(lane_v == 0, 1.0, 0.0).astype(BF16)
        for r in range(tk // rc):
            delta = (r * rc + lax.broadcasted_iota(jnp.int32, (rc, 1), 0)
                     - lax.broadcasted_iota(jnp.int32, (1, tk), 1)).astype(F32)
            fix_ref[r] = (2.0 * sigma) * jnp.minimum(delta, 0.0)

    q = q_ref[0].astype(F32) * (dh ** -0.5 * LOG2E)
    lane = lax.broadcasted_iota(jnp.int32, (tq, LANES), 1)
    row = lax.broadcasted_iota(jnp.int32, (tq, 1), 0)
    u_hi, u_mid, u_lo = _split3((qi * tq + row).astype(F32) * (-sigma))
    aug_q = jnp.where(lane == 0, u_hi, jnp.where(lane == 1, u_mid, jnp.where(lane == 2, u_lo,
                      jnp.where(lane < 6, 1.0, 0.0))))
    for mp in range(2):
        qz = jnp.where(lane < dh, q, 0.0) if mp == 0 else jnp.where(lane >= dh, q, 0.0)
        lhs_ref[mp, 0] = jnp.concatenate([qz, aug_q], axis=-1).astype(BF16)
        lhs_ref[mp, 1] = jnp.concatenate([qz, -aug_q], axis=-1).astype(BF16)
    m_ref[...] = jnp.full(m_ref.shape, -1e30, F32)
    acc_ref[...] = jnp.zeros(acc_ref.shape, F32)

    def block(j, local):
        off = pl.multiple_of(j * tk, tk)
        ka = ka_ref[:, pl.ds(off, tk)]
        vb = vb_ref[pl.ds(off, tk), :]
        for mp in range(2):
            for c in range(tq // rc):
                rows = slice(c * rc, (c + 1) * rc)
                if isinstance(local, str):
                    sign, fixup = (0 if local == "before" else 1), None
                elif (c + 1) * rc <= local * tk:
                    sign, fixup = 1, None
                elif c * rc >= (local + 1) * tk:
                    sign, fixup = 0, None
                else:
                    sign, fixup = 0, fix_ref[(c * rc - local * tk) // rc]
                s = jnp.dot(lhs_ref[mp, sign, rows, :], ka, preferred_element_type=F32)
                if fixup is not None:
                    s = s + fixup
                m_old = m_ref[mp, rows, :]
                m_new = jnp.maximum(m_old, jnp.max(s, axis=-1, keepdims=True))
                alpha = jnp.exp2(m_old - m_new)
                p = jnp.exp2(s - m_new).astype(BF16)
                acc_ref[mp, rows, :] = alpha * acc_ref[mp, rows, :] + jnp.dot(p, vb, preferred_element_type=F32)
                m_ref[mp, rows, :] = m_new

    def run(lo, hi, where):
        def body(j, carry):
            block(j, where)
            return carry
        lax.fori_loop(lo, hi, body, 0)

    j0 = (qi * tq) // tk
    run(0, j0, "before")
    for local in range(tq // tk):
        block(j0 + local, local)
    run(j0 + tq // tk, seq // tk, "after")

    lv = lam_ref[...]
    lam = (jnp.exp(jnp.sum(lv[0:1] * lv[1:2], axis=-1, keepdims=True))
           - jnp.exp(jnp.sum(lv[2:3] * lv[3:4], axis=-1, keepdims=True)) + lam_init)
    o1 = acc_ref[0, :, :LANES] / acc_ref[0, :, LANES:LANES + 1]
    o2 = acc_ref[1, :, :LANES] / acc_ref[1, :, LANES:LANES + 1]
    o = o1 - lam * o2
    ms = jnp.mean(o * o, axis=-1, keepdims=True)
    o_ref[0] = o * lax.rsqrt(ms + NORM_EPS) * g_ref[...] * (1.0 - lam_init)


def _diff_attention(proj_a3, da_lambda, da_norm_g, lam_init, tq=2048, tk=1024, rc=256):
    b, s, _ = proj_a3.shape
    assert tq % tk == 0 and s % tq == 0 and tk % rc == 0
    kern = functools.partial(_attn_kernel, tq=tq, tk=tk, rc=rc, seq=s, lam_init=lam_init)
    return pl.pallas_call(
        kern, grid=(b, DA_HEADS, s // tq),
        in_specs=[
            pl.BlockSpec((4, DA_HEAD_DIM), lambda bi, h, i: (0, 0)),
            pl.BlockSpec((1, DA_V_DIM), lambda bi, h, i: (0, 0)),
            pl.BlockSpec((1, tq, LANES), lambda bi, h, i: (bi, i, h)),
            pl.BlockSpec((1, s, LANES), lambda bi, h, i: (bi, 0, DA_HEADS + h)),
            pl.BlockSpec((1, s, LANES), lambda bi, h, i: (bi, 0, 2 * DA_HEADS + h)),
        ],
        out_specs=pl.BlockSpec((1, tq, LANES), lambda bi, h, i: (bi, i, h)),
        out_shape=jax.ShapeDtypeStruct((b, s, BRANCH_WIDTH), F32),
        scratch_shapes=[pltpu.VMEM((2 * LANES, s), BF16), pltpu.VMEM((s, 2 * LANES), BF16),
                        pltpu.VMEM((2, 2, tq, 2 * LANES), BF16), pltpu.VMEM((tk // rc, rc, tk), F32),
                        pltpu.VMEM((2, tq, 1), F32), pltpu.VMEM((2, tq, 2 * LANES), F32)],
        compiler_params=_params(3), name="diff_attention")(
            da_lambda, da_norm_g.reshape(1, DA_V_DIM), proj_a3, proj_a3, proj_a3)


def _s5_matrices(a_re, a_im, log_dt, b_re, b_im, c_re, c_im):
    L, C, P, G = S5_CHUNK, S5_GROUP_SIZE, S5_STATE, S5_GROUPS
    f = lambda t: t.astype(F32)
    a_re, a_im, b_re, b_im, c_re, c_im = map(f, (a_re, a_im, b_re, b_im, c_re, c_im))
    dt = jnp.exp(f(log_dt))[..., None]
    lr, li = a_re * dt, a_im * dt

    def apow(j):
        jj = j.astype(F32)[:, None, None, None]
        mag = jnp.exp(lr[None] * jj)
        return mag * jnp.cos(li[None] * jj), mag * jnp.sin(li[None] * jj)

    a1r, a1i = apow(jnp.arange(1, 2))
    nr, ni = a1r[0] - 1.0, a1i[0]
    den = a_re * a_re + a_im * a_im
    cr, ci = (nr * a_re + ni * a_im) / den, (ni * a_re - nr * a_im) / den
    bbr = cr[..., None] * b_re - ci[..., None] * b_im
    bbi = cr[..., None] * b_im + ci[..., None] * b_re

    pr, pi = apow(jnp.arange(L))
    wr = pr[..., None] * bbr[None] - pi[..., None] * bbi[None]
    wi = pr[..., None] * bbi[None] + pi[..., None] * bbr[None]
    kern = (jnp.einsum('dgcp,jdgpe->jdgce', c_re, wr, precision=HIGHEST)
            - jnp.einsum('dgcp,jdgpe->jdgce', c_im, wi, precision=HIGHEST))

    def toeplitz(kd):
        f = jnp.pad(kd.astype(BF16).transpose(1, 3, 2, 0), ((0, 0), (0, 0), (0, 0), (0, L)))
        return jnp.tile(f, (1, 1, 1, L))[..., :L * (2 * L - 1)].reshape(G, C, C, L, 2 * L - 1)[..., :L]

    same = jnp.eye(2, dtype=BF16)[None, None, :, None, None, :, None]

    def pair_matrix(tz, s_axis, t_axis):
        tz = tz.reshape(G // 2, 2, C, C, L, L).transpose(0, 2 + s_axis, 1, 2, 2 + t_axis, 3)
        return (tz[:, :, :, :, :, None, :] * same).reshape(G // 2, 2 * L * C, 2 * L * C)

    m_mat = jnp.stack([pair_matrix(toeplitz(kern[:, 0]), 2, 3),
                       pair_matrix(toeplitz(kern[:, 1]), 3, 2)])

    def summary(w_dir):
        return w_dir.transpose(1, 0, 3, 2).reshape(G, L * C, P)

    p_re = jnp.stack([summary(wr[::-1, 0]), summary(wr[:, 1])])
    p_im = jnp.stack([summary(wi[::-1, 0]), summary(wi[:, 1])])

    qr, qi = apow(jnp.arange(1, L + 1))
    car = c_re[None] * qr[:, :, :, None, :] - c_im[None] * qi[:, :, :, None, :]
    cai = c_re[None] * qi[:, :, :, None, :] + c_im[None] * qr[:, :, :, None, :]

    def readout(ca_dir):
        return ca_dir.transpose(1, 3, 0, 2).reshape(G, P, L * C)

    q_re = jnp.stack([readout(car[:, 0]), readout(car[::-1, 1])])
    q_im = jnp.stack([-readout(cai[:, 0]), -readout(cai[::-1, 1])])

    half = jax.nn.one_hot(jnp.arange(G) % 2, 2, dtype=F32)
    p_pad = jnp.stack([p_re, p_im], axis=-2)[:, :, :, :, None, :] * half[None, :, None, None, :, None]
    p_pad = p_pad.reshape(2, G // 2, 2, L, C, 4 * P).transpose(0, 1, 3, 2, 4, 5).reshape(2, G // 2, 2 * L * C, 4 * P)
    q_pad = jnp.stack([q_re, q_im], axis=2)[:, :, :, None] * half[None, :, None, :, None, None]
    q_pad = q_pad.reshape(2, G // 2, 2, 4 * P, L, C).transpose(0, 1, 3, 4, 2, 5).reshape(2, G // 2, 4 * P, 2 * L * C)

    alr, ali = apow(jnp.arange(L, L + 1))
    a_l = jnp.concatenate([alr[0].reshape(2, G // 2, 2 * P), ali[0].reshape(2, G // 2, 2 * P)], axis=-1)
    return m_mat, p_pad.astype(BF16), q_pad.astype(BF16), a_l.reshape(2, G // 2, 1, 4 * P)


def _s5_kernel(x_ref, m_ref, p_ref, q_ref, al_ref, d_ref, y_ref, s_ref, xin_ref, *, n_chunks, bsz):
    hp = 2 * S5_STATE
    xb = x_ref[0].astype(BF16)
    for d in range(2):
        s = jnp.dot(xb, p_ref[d, 0], preferred_element_type=F32)
        s_ref[2 * d] = s[:, :hp]
        s_ref[2 * d + 1] = s[:, hp:]

    al = [al_ref[d, 0] for d in range(2)]
    alr = [a[:, :hp] for a in al]
    ali = [a[:, hp:] for a in al]

    def body(i, carry):
        new = []
        for d in range(2):
            re, im = carry[2 * d], carry[2 * d + 1]
            n = i if d == 0 else n_chunks - 1 - i
            rows = pl.ds(n, bsz, stride=n_chunks)
            xin_ref[2 * d, rows, :] = re
            xin_ref[2 * d + 1, rows, :] = im
            new.append(alr[d] * re - ali[d] * im + s_ref[2 * d, rows, :])
            new.append(alr[d] * im + ali[d] * re + s_ref[2 * d + 1, rows, :])
        return tuple(new)

    zero = jnp.zeros((bsz, hp), F32)
    lax.fori_loop(0, n_chunks, body, (zero, zero, zero, zero))

    xin = [jnp.concatenate([xin_ref[2 * d], xin_ref[2 * d + 1]], axis=-1).astype(BF16) for d in range(2)]
    y = x_ref[0] * d_ref[0]
    for d in range(2):
        y = y + jnp.dot(xb, m_ref[d, 0], preferred_element_type=F32)
        y = y + jnp.dot(xin[d], q_ref[d, 0], preferred_element_type=F32)
    y_ref[0] = y


S5_PAIR = 2 * S5_GROUP_SIZE
S5_PER_VREG = LANES // S5_PAIR


def _s5_proj_kernel(x_ref, w_ref, o_ref, tok_ref):
    c = S5_PAIR
    rows = o_ref.shape[1]
    res = jnp.dot(x_ref[...], w_ref[...], preferred_element_type=F32)
    for j in range(tok_ref.shape[0]):
        tok_ref[j] = res[:, j * LANES:(j + 1) * LANES]
    for j in range(tok_ref.shape[0]):
        for t in range(S5_CHUNK):
            blk = tok_ref[j, pl.ds(t, rows, stride=S5_CHUNK), :]
            for gg in range(S5_PER_VREG):
                o_ref[j * S5_PER_VREG + gg, :, t * c:(t + 1) * c] = blk[:, gg * c:(gg + 1) * c]


def _s5_unpack_kernel(y_ref, o_ref, tok_ref):
    c = S5_PAIR
    rows = y_ref.shape[1]
    for j in range(tok_ref.shape[0]):
        for t in range(S5_CHUNK):
            blk = jnp.concatenate([y_ref[j * S5_PER_VREG + gg, :, t * c:(t + 1) * c] for gg in range(S5_PER_VREG)],
                                  axis=-1)
            tok_ref[j, pl.ds(t, rows, stride=S5_CHUNK), :] = blk
    for j in range(tok_ref.shape[0]):
        o_ref[:, j * LANES:(j + 1) * LANES] = tok_ref[j]


def _s5_proj(xb, w_u, tm=256):
    t, d = xb.shape
    rows = t // S5_CHUNK
    lc = S5_CHUNK * S5_PAIR
    w = w_u.shape[1]
    return pl.pallas_call(
        _s5_proj_kernel, grid=(rows // tm,),
        in_specs=[pl.BlockSpec((tm * S5_CHUNK, d), lambda i: (i, 0)), pl.BlockSpec(w_u.shape, lambda i: (0, 0))],
        out_specs=pl.BlockSpec((S5_GROUPS // 2, tm, lc), lambda i: (0, i, 0)),
        out_shape=jax.ShapeDtypeStruct((S5_GROUPS // 2, rows, lc), F32),
        scratch_shapes=[pltpu.VMEM((w // LANES, tm * S5_CHUNK, LANES), F32)],
        compiler_params=_params(1), name="s5_proj")(xb, w_u)


def _s5(x, mats, d_skip, b, s):
    L, C, G = S5_CHUNK, S5_GROUP_SIZE, S5_GROUPS
    w = G * C
    n = s // L
    lc = L * S5_PAIR
    st = 4 * S5_STATE
    m_mat, p_pad, q_pad, a_l = mats
    d_row = jnp.tile(d_skip.astype(F32).reshape(G // 2, 1, S5_PAIR), (1, L, 1)).reshape(G // 2, 1, lc)
    spec = lambda r, c_: pl.BlockSpec((2, 1, r, c_), lambda k: (0, k, 0, 0))
    kern = functools.partial(_s5_kernel, n_chunks=n, bsz=b)
    y = pl.pallas_call(
        kern, grid=(G // 2,),
        in_specs=[pl.BlockSpec((1, n * b, lc), lambda k: (k, 0, 0)), spec(lc, lc), spec(lc, st), spec(st, lc),
                  spec(1, st), pl.BlockSpec((1, 1, lc), lambda k: (k, 0, 0))],
        out_specs=pl.BlockSpec((1, n * b, lc), lambda k: (k, 0, 0)),
        out_shape=jax.ShapeDtypeStruct((G // 2, n * b, lc), F32),
        scratch_shapes=[pltpu.VMEM((4, n * b, LANES), F32), pltpu.VMEM((4, n * b, LANES), F32)],
        compiler_params=_params(1), name="s5")(x, m_mat, p_pad, q_pad, a_l, d_row)
    tm = 256
    return pl.pallas_call(
        _s5_unpack_kernel, grid=(b * n // tm,),
        in_specs=[pl.BlockSpec((G // 2, tm, lc), lambda i: (0, i, 0))],
        out_specs=pl.BlockSpec((tm * L, w), lambda i: (i, 0)),
        out_shape=jax.ShapeDtypeStruct((b * s, w), F32),
        scratch_shapes=[pltpu.VMEM((w // LANES, tm * L, LANES), F32)],
        compiler_params=_params(1), name="s5_unpack")(y)


def _gla_kernel(qf_ref, kf_ref, vf_ref, gf_ref, qb_ref, kb_ref, vb_ref, gb_ref, bg_ref,
                of_ref, ob_ref, st_ref, *, rows):
    c = GLA_CHUNK
    n_sub = rows // c

    @pl.when(pl.program_id(1) == 0)
    def _():
        st_ref[...] = jnp.zeros_like(st_ref)

    nt = (((1,), (1,)), ((), ()))
    tn = (((0,), (0,)), ((), ()))
    log2 = lambda n: int(math.log2(n))
    r_r = lax.broadcasted_iota(jnp.int32, (rows, rows), 0)
    c_r = lax.broadcasted_iota(jnp.int32, (rows, rows), 1)
    same_chunk = (r_r >> log2(c)) == (c_r >> log2(c))
    t_row = lax.broadcasted_iota(jnp.int32, (c, GLA_K), 0)
    t_col = lax.broadcasted_iota(jnp.int32, (c, GLA_K), 1) & (c - 1)
    k_lane = lax.broadcasted_iota(jnp.int32, (1, GLA_K), 1) >> log2(GLA_DK)
    v_lane = lax.broadcasted_iota(jnp.int32, (1, BRANCH_WIDTH), 1) >> log2(GLA_DV)
    st_same_head = ((lax.broadcasted_iota(jnp.int32, (BRANCH_WIDTH, GLA_K), 0) >> log2(GLA_DV))
                    == (lax.broadcasted_iota(jnp.int32, (BRANCH_WIDTH, GLA_K), 1) >> log2(GLA_DK)))

    for d, (q_ref, k_ref, v_ref, g_ref, o_ref) in enumerate(
            ((qf_ref, kf_ref, vf_ref, gf_ref, of_ref), (qb_ref, kb_ref, vb_ref, gb_ref, ob_ref))):
        gate = g_ref[0].astype(F32) + bg_ref[d]
        log_a = (jnp.minimum(gate, 0.0) - jnp.log(1.0 + jnp.exp(-jnp.abs(gate)))) * (1.0 / GLA_TAU)
        if d == 0:
            tri = jnp.where(same_chunk, jnp.where(r_r >= c_r, 1.0, 0.0), 0.0).astype(BF16)
            keep, i_ref, i_last = (t_row >= t_col), c // 2, c - 1
        else:
            tri = jnp.where(same_chunk, jnp.where(c_r >= r_r, 1.0, 0.0), 0.0).astype(BF16)
            keep, i_ref, i_last = (t_col > t_row), c - 1 - c // 2, 0
        la_hi = log_a.astype(BF16)
        la_lo = (log_a - la_hi.astype(F32)).astype(BF16)
        cum_all = (jnp.dot(tri, la_hi, preferred_element_type=F32) + jnp.dot(tri, la_lo, preferred_element_type=F32))
        q = q_ref[0].astype(F32) * (GLA_DK ** -0.5)
        k = k_ref[0].astype(F32)
        v = v_ref[0]
        for ci in (range(n_sub) if d == 0 else range(n_sub - 1, -1, -1)):
            sl = slice(ci * c, (ci + 1) * c)
            cum = cum_all[sl]
            mid = cum[i_ref:i_ref + 1]
            last = cum[i_last:i_last + 1]
            qc, kc, vc = q[sl], k[sl], v[sl]
            qe = (qc * jnp.exp(cum - mid)).astype(BF16)
            ke = kc * jnp.exp(mid - cum)
            kd = (kc * jnp.exp(last - cum)).astype(BF16)
            qs = (qc * jnp.exp(cum)).astype(BF16)
            decay = jnp.exp(last)
            ke4 = jnp.concatenate([jnp.where(k_lane == h, ke, 0.0) for h in range(GLA_HEADS)], axis=0).astype(BF16)
            v4 = jnp.concatenate([jnp.where(v_lane == h, vc, 0.0) for h in range(GLA_HEADS)], axis=0).astype(BF16)
            sc = lax.dot_general(qe, ke4, nt, preferred_element_type=F32)
            sc = jnp.where(keep, sc, 0.0).astype(BF16)
            st = st_ref[d]
            o = (jnp.dot(sc, v4, preferred_element_type=F32)
                 + lax.dot_general(qs, st.astype(BF16), nt, preferred_element_type=F32))
            o_ref[0, sl, :] = o
            ds = lax.dot_general(vc.astype(BF16), kd, tn, preferred_element_type=F32)
            for h in range(GLA_HEADS):
                rs = slice(h * GLA_DV, (h + 1) * GLA_DV)
                ls = slice(h * GLA_DK // LANES * LANES, h * GLA_DK // LANES * LANES + LANES)
                st_ref[d, rs, ls] = decay[:, ls] * st[rs, ls] + jnp.where(st_same_head[rs, ls], ds[rs, ls], 0.0)


def _gla(proj_a3, proj_b3, b_gate, rows=128):
    b, s, _ = proj_a3.shape
    n = s // rows
    bg = b_gate.astype(F32).reshape(2, 1, GLA_K)

    def specs(rev):
        blk = (lambda i: n - 1 - i) if rev else (lambda i: i)
        return [pl.BlockSpec((1, rows, GLA_K), lambda bi, i: (bi, blk(i), A_QC // GLA_K)),
                pl.BlockSpec((1, rows, GLA_K), lambda bi, i: (bi, blk(i), A_KC // GLA_K)),
                pl.BlockSpec((1, rows, BRANCH_WIDTH), lambda bi, i: (bi, blk(i), A_VC // BRANCH_WIDTH)),
                pl.BlockSpec((1, rows, GLA_K), lambda bi, i: (bi, blk(i), GATE_COLS // GLA_K + int(rev)))]

    out_f = pl.BlockSpec((1, rows, BRANCH_WIDTH), lambda bi, i: (bi, i, 0))
    out_b = pl.BlockSpec((1, rows, BRANCH_WIDTH), lambda bi, i: (bi, n - 1 - i, 0))
    shape = jax.ShapeDtypeStruct((b, s, BRANCH_WIDTH), F32)
    return pl.pallas_call(
        functools.partial(_gla_kernel, rows=rows), grid=(b, n),
        in_specs=specs(False) + specs(True) + [pl.BlockSpec((2, 1, GLA_K), lambda bi, i: (0, 0, 0))],
        out_specs=[out_f, out_b], out_shape=[shape, shape],
        scratch_shapes=[pltpu.VMEM((2, BRANCH_WIDTH, GLA_K), F32)],
        compiler_params=_params(2), name="gla")(
            proj_a3, proj_a3, proj_a3, proj_b3, proj_a3, proj_a3, proj_a3, proj_b3, bg)


def _ranks(vals):
    ranks = []
    for i, v in enumerate(vals):
        r = jnp.zeros(v.shape, F32)
        for o, other in enumerate(vals):
            if o != i:
                r = r + (jnp.where(other >= v, 1.0, 0.0) if o < i else jnp.where(other > v, 1.0, 0.0))
        ranks.append(r)
    return ranks


def _route(scores_t, bias_col):
    gs = EXPERTS_PER_GROUP
    s = [scores_t[e:e + 1] for e in range(N_EXPERTS)]
    biased = [s[e] + bias_col[e:e + 1] for e in range(N_EXPERTS)]
    top2, group_score = [], []
    for g in range(N_EXPERTS // gs):
        rk = _ranks(biased[g * gs:(g + 1) * gs])
        top2 += [r < 2.0 for r in rk]
        kept = [jnp.where(top2[g * gs + j], biased[g * gs + j], 0.0) for j in range(gs)]
        group_score.append(functools.reduce(lambda a, b: a + b, kept))
    best = [r < 1.0 for r in _ranks(group_score)]
    w = [jnp.where(top2[e], jnp.where(best[e // gs], s[e], 0.0), 0.0) for e in range(N_EXPERTS)]
    total = functools.reduce(lambda a, b: a + b, w)
    return [we / total for we in w]


def _merge_kernel(x_ref, oa_ref, ys_ref, gf_ref, gb_ref, r_ref, gl0_ref, gl1_ref, gl2_ref,
                  wglu_ref, bglu_ref, gng_ref, wup_ref, mb_ref, wout_ref, lng_ref, lnb_ref, rw_ref, rb_ref,
                  xo_ref, xob_ref, comb_ref):
    y = ys_ref[...]
    gelu = 0.5 * y * (1.0 + jnp.tanh(math.sqrt(2.0 / math.pi) * (y + 0.044715 * (y * y * y))))
    o_b = gelu * _sigmoid(jnp.dot(gelu.astype(BF16), wglu_ref[...], preferred_element_type=F32) + bglu_ref[...])

    r = r_ref[...].astype(F32)
    gated = r * _sigmoid(r)
    heads = []
    for h in range(GLA_HEADS):
        sl = slice(h * GLA_DV, (h + 1) * GLA_DV)
        o = gf_ref[:, sl] + gb_ref[:, sl]
        ms = jnp.mean(o * o, axis=-1, keepdims=True)
        heads.append(o * lax.rsqrt(ms + NORM_EPS) * gng_ref[...] * gated[:, sl])
    o_c = jnp.concatenate(heads, axis=-1)

    merged = None
    for n, (o_n, gl_ref) in enumerate(((oa_ref[...], gl0_ref), (o_b, gl1_ref), (o_c, gl2_ref))):
        gate = _sigmoid(gl_ref[...].astype(F32) + mb_ref[n])
        term = gate * jnp.dot(o_n.astype(BF16), wup_ref[n], preferred_element_type=F32)
        merged = term if merged is None else merged + term
    mix = jnp.dot(merged.astype(BF16), wout_ref[...], preferred_element_type=F32)
    x1 = _layer_norm(DN_ALPHA * x_ref[...] + mix, lng_ref[...], lnb_ref[...])
    xo_ref[...] = x1
    x_hi = x1.astype(BF16)
    xob_ref[...] = x_hi

    x_lo = (x1 - x_hi.astype(F32)).astype(BF16)
    logits = (jnp.dot(x_hi, rw_ref[0], preferred_element_type=F32) + jnp.dot(x_lo, rw_ref[0], preferred_element_type=F32)
              + jnp.dot(x_hi, rw_ref[1], preferred_element_type=F32))
    rows = _route(_sigmoid(logits.T[:N_EXPERTS]), rb_ref[...])
    comb_ref[...] = jnp.zeros(comb_ref.shape, F32)
    for e in range(N_EXPERTS):
        comb_ref[e:e + 1, :] = rows[e]


def _merge(x, o_a, y_s5, g_f, g_b, proj_a, proj_b, wglu, bglu, gng, wup, mb, wout, lng, lnb, rw, rb, tm=256):
    t, d = x.shape
    w = BRANCH_WIDTH
    row = lambda width, cb=0: pl.BlockSpec((tm, width), lambda i: (i, cb))
    full = lambda a: pl.BlockSpec(a.shape, lambda i: (0,) * a.ndim)
    weights = (wglu, bglu, gng, wup, mb, wout, lng, lnb, rw, rb)
    return pl.pallas_call(
        _merge_kernel, grid=(t // tm,),
        in_specs=[row(d), row(w), row(w), row(w), row(w), row(w, A_RC // w), row(d, 0), row(d, 1), row(d, 2)]
        + [full(a) for a in weights],
        out_specs=[row(d), row(d), pl.BlockSpec((LANES, tm), lambda i: (0, i))],
        out_shape=[jax.ShapeDtypeStruct((t, d), F32), jax.ShapeDtypeStruct((t, d), BF16),
                   jax.ShapeDtypeStruct((LANES, t), F32)],
        compiler_params=_params(1), name="merge")(
            x, o_a, y_s5, g_f, g_b, proj_a, proj_b, proj_b, proj_b, *weights)


MOE_SUB = 256
MOE_CAP = 64


def _swiglu(x, wg_ref, wu_ref, wd_ref, row_scale):
    g = jnp.dot(x, wg_ref[0], preferred_element_type=F32)
    u = jnp.dot(x, wu_ref[0], preferred_element_type=F32)
    h = (g * _sigmoid(g)) * u * row_scale
    return jnp.dot(h.astype(BF16), wd_ref[0], preferred_element_type=F32)


def _moe_kernel(xb_ref, xf_ref, comb_ref, wg_ref, wu_ref, wd_ref, lng_ref, lnb_ref, of_ref, ob_ref,
                pos_ref, ind_ref, sel_ref, xg_ref, y_ref, cg_ref, acc_ref, flag_ref, *, tm, cap):
    e = pl.program_id(1)
    ns = tm // MOE_SUB
    gs = EXPERTS_PER_GROUP
    sub = lambda k: slice(k * MOE_SUB, (k + 1) * MOE_SUB)
    slots = lambda k: slice(k * cap, (k + 1) * cap)
    r_i = lax.broadcasted_iota(jnp.int32, (MOE_SUB, MOE_SUB), 0)
    c_i = lax.broadcasted_iota(jnp.int32, (MOE_SUB, MOE_SUB), 1)

    @pl.when(e == 0)
    def _():
        ind = jnp.where(comb_ref[...] > 0.0, 1.0, 0.0)
        ind_ref[...] = ind
        before = jnp.where(r_i < c_i, 1.0, 0.0).astype(BF16)
        most = jnp.zeros((LANES, 1), F32)
        for k in range(ns):
            pos_ref[:, sub(k)] = jnp.dot(ind[:, sub(k)].astype(BF16), before, preferred_element_type=F32)
            most = jnp.maximum(most, jnp.sum(ind[:, sub(k)], axis=-1, keepdims=True))
        flag_ref[0] = (jnp.max(most) > cap).astype(jnp.int32)
        acc_ref[...] = jnp.zeros_like(acc_ref)

    slot_id = lax.broadcasted_iota(jnp.int32, (cap, 1), 0).astype(F32)
    for grp in range(N_EXPERTS // gs):
        @pl.when(e == grp * gs)
        def _(grp=grp):
            for k in range(ns):
                pieces = []
                for j in range(gs):
                    ee = grp * gs + j
                    one_hot = jnp.where(pos_ref[ee:ee + 1, sub(k)] == slot_id, ind_ref[ee:ee + 1, sub(k)], 0.0)
                    cg_ref[j, slots(k), :] = jnp.sum(one_hot * comb_ref[ee:ee + 1, sub(k)], axis=-1, keepdims=True)
                    pieces.append(one_hot)
                sel = jnp.concatenate(pieces, axis=0).astype(BF16)
                sel_ref[k] = sel
                xg = jnp.dot(sel, xb_ref[sub(k), :], preferred_element_type=F32).astype(BF16)
                for j in range(gs):
                    xg_ref[j, slots(k), :] = xg[j * cap:(j + 1) * cap]

    j = e % gs
    y_ref[j] = _swiglu(xg_ref[j], wg_ref, wu_ref, wd_ref, cg_ref[j]).astype(BF16)

    @pl.when(j == gs - 1)
    def _():
        for k in range(ns):
            yk = jnp.concatenate([y_ref[jj, slots(k), :] for jj in range(gs)], axis=0)
            acc_ref[sub(k), :] += lax.dot_general(sel_ref[k], yk, (((0,), (0,)), ((), ())),
                                                  preferred_element_type=F32)

    @pl.when(flag_ref[0] > 0)
    def _():
        comb = comb_ref[...].T
        earlier = jnp.where(r_i > c_i, 1.0, 0.0).astype(BF16)
        over = []
        for k in range(ns):
            pos = jnp.dot(earlier, jnp.where(comb[sub(k)] > 0.0, 1.0, 0.0).astype(BF16), preferred_element_type=F32)
            over.append(jnp.where(pos >= cap, comb[sub(k)], 0.0))
        over = jnp.concatenate(over, axis=0)
        lane = lax.broadcasted_iota(jnp.int32, over.shape, 1)
        ce = jnp.sum(jnp.where(lane == e, over, 0.0), axis=-1, keepdims=True)
        acc_ref[...] += _swiglu(xb_ref[...], wg_ref, wu_ref, wd_ref, ce)

    @pl.when(e == N_EXPERTS - 1)
    def _():
        y = _layer_norm(DN_ALPHA * xf_ref[...] + acc_ref[...], lng_ref[...], lnb_ref[...])
        of_ref[...] = y
        ob_ref[...] = y.astype(BF16)


def _moe(xb, xf, comb, wg, wu, wd, lng, lnb, tm=1024, cap=MOE_CAP):
    t, d = xf.shape
    ns = tm // MOE_SUB
    gs = EXPERTS_PER_GROUP
    row = lambda width: pl.BlockSpec((tm, width), lambda i, e: (i, 0))
    vec = pl.BlockSpec((1, d), lambda i, e: (0, 0))
    return pl.pallas_call(
        functools.partial(_moe_kernel, tm=tm, cap=cap), grid=(t // tm, N_EXPERTS),
        in_specs=[row(d), row(d), pl.BlockSpec((LANES, tm), lambda i, e: (0, i)),
                  pl.BlockSpec((1, d, EXPERT_FF), lambda i, e: (e, 0, 0)),
                  pl.BlockSpec((1, d, EXPERT_FF), lambda i, e: (e, 0, 0)),
                  pl.BlockSpec((1, EXPERT_FF, d), lambda i, e: (e, 0, 0)), vec, vec],
        out_specs=[row(d), row(d)],
        out_shape=[jax.ShapeDtypeStruct((t, d), F32), jax.ShapeDtypeStruct((t, d), BF16)],
        scratch_shapes=[pltpu.VMEM((LANES, tm), F32), pltpu.VMEM((LANES, tm), F32),
                        pltpu.VMEM((ns, gs * cap, MOE_SUB), BF16), pltpu.VMEM((gs, ns * cap, d), BF16),
                        pltpu.VMEM((gs, ns * cap, d), BF16), pltpu.VMEM((gs, ns * cap, 1), F32),
                        pltpu.VMEM((tm, d), F32), pltpu.SMEM((1,), jnp.int32)],
        compiler_params=_params(2), name="moe")(xb, xf, comb, wg, wu, wd, lng, lnb)


def _proj_b_weights(w_in_l, w_gate):
    z0 = W_IN_Z
    folded = [jnp.dot(w_in_l[:, z0 + i * GLA_GATE_RANK:z0 + (i + 1) * GLA_GATE_RANK].astype(F32),
                      w_gate[i].astype(F32), precision=HIGHEST) for i in range(2)]
    return jnp.concatenate([w_in_l[:, z0 + 2 * GLA_GATE_RANK:].astype(F32)] + folded, axis=1).astype(BF16)


def kernel(x, ln0_g, ln0_b, w_in, da_lambda, da_norm_g, s5_a_re, s5_a_im, s5_log_dt, s5_b_re, s5_b_im,
           s5_c_re, s5_c_im, s5_d, s5_w_glu, s5_b_glu, gla_w_gate, gla_b_gate, gla_norm_g, merge_w_up,
           merge_b, w_out, ln1_g, ln1_b, router_w, router_bias, moe_w_gate, moe_w_up, moe_w_down,
           ln2_g, ln2_b):
    b, s, d = x.shape
    t = b * s
    vec = lambda a: a.astype(F32).reshape(1, -1)
    xf, xb = _ln0(x.reshape(t, d), ln0_g.astype(F32), ln0_b.astype(F32))
    rw32 = jnp.zeros((d, LANES), F32).at[:, :N_EXPERTS].set(router_w.astype(F32))
    rw_hi = rw32.astype(BF16)
    rw = jnp.stack([rw_hi, (rw32 - rw_hi.astype(F32)).astype(BF16)])
    rb = router_bias.astype(F32).reshape(N_EXPERTS, 1)
    for l in range(DEPTH):
        w_a = jnp.concatenate([w_in[l][:, :W_IN_U], w_in[l][:, W_IN_GLA:W_IN_Z]], axis=1).astype(BF16)
        w_u = w_in[l][:, W_IN_U:W_IN_GLA].astype(BF16)
        w_b = _proj_b_weights(w_in[l], gla_w_gate[l])
        proj_a = _matmul(xb, w_a, 1024, PROJ_A // 2, "in_proj_a")
        proj_b = _matmul(xb, w_b, 512, PROJ_B, "in_proj_b")
        proj_a3 = proj_a.reshape(b, s, PROJ_A)
        proj_b3 = proj_b.reshape(b, s, PROJ_B)

        lam_init = 0.8 - 0.6 * math.exp(-0.3 * l)
        o_a = _diff_attention(proj_a3, da_lambda[l].astype(F32), da_norm_g[l].astype(F32), lam_init)
        mats = _s5_matrices(s5_a_re[l], s5_a_im[l], s5_log_dt[l], s5_b_re[l], s5_b_im[l], s5_c_re[l], s5_c_im[l])
        y_s5 = _s5(_s5_proj(xb, w_u), mats, s5_d[l], b, s)
        g_f, g_b = _gla(proj_a3, proj_b3, gla_b_gate[l])

        xf, xb, comb = _merge(
            xf, o_a.reshape(t, BRANCH_WIDTH), y_s5, g_f.reshape(t, BRANCH_WIDTH), g_b.reshape(t, BRANCH_WIDTH),
            proj_a, proj_b, s5_w_glu[l].astype(BF16), vec(s5_b_glu[l]), vec(gla_norm_g[l]),
            merge_w_up[l].astype(BF16), merge_b[l].astype(F32).reshape(N_BRANCH, 1, d), w_out[l].astype(BF16),
            vec(ln1_g[l]), vec(ln1_b[l]), rw, rb)
        xf, xb = _moe(xb, xf, comb, moe_w_gate[l].astype(BF16), moe_w_up[l].astype(BF16),
                      moe_w_down[l].astype(BF16), vec(ln2_g[l]), vec(ln2_b[l]))
    return xf.reshape(b, s, d)
```

```python
import functools
import math

import jax
import jax.numpy as jnp
from jax import lax
from jax.experimental import pallas as pl
from jax.experimental.pallas import tpu as pltpu

F32 = jnp.float32
BF16 = jnp.bfloat16
HIGHEST = lax.Precision.HIGHEST

D_MODEL = 1024
DEPTH = 2
BRANCH_WIDTH = D_MODEL // 2
N_BRANCH = 3

DA_HEADS = 4
DA_HEAD_DIM = BRANCH_WIDTH // (2 * DA_HEADS)
DA_V_DIM = 2 * DA_HEAD_DIM
ALIBI_MAX_EXP = 8.0

S5_GROUP_SIZE = 16
S5_GROUPS = BRANCH_WIDTH // S5_GROUP_SIZE
S5_STATE = 64
S5_CHUNK = 16

GLA_HEADS = 4
GLA_DV = BRANCH_WIDTH // GLA_HEADS
GLA_DK = GLA_DV // 2
GLA_K = GLA_HEADS * GLA_DK
GLA_GATE_RANK = 16
GLA_TAU = 16.0
GLA_CHUNK = 64

N_EXPERTS = 16
EXPERTS_PER_GROUP = 4
EXPERT_FF = D_MODEL // 2

DN_ALPHA = (2.0 * DEPTH) ** 0.25
NORM_EPS = 1e-5
LOG2E = math.log2(math.e)

LANES = 128
W_IN_U = 3 * BRANCH_WIDTH
W_IN_GLA = W_IN_U + BRANCH_WIDTH
W_IN_Z = W_IN_GLA + 2 * GLA_K + 2 * BRANCH_WIDTH
A_QC = W_IN_U
A_KC = A_QC + GLA_K
A_VC = A_KC + GLA_K
A_RC = A_VC + BRANCH_WIDTH
PROJ_A = A_RC + BRANCH_WIDTH
GATE_COLS = N_BRANCH * D_MODEL
PROJ_B = GATE_COLS + 2 * GLA_K
VMEM_LIMIT = 56 * 1024 * 1024


def _params(n_grid_dims):
    return pltpu.CompilerParams(dimension_semantics=("arbitrary",) * n_grid_dims,
                                vmem_limit_bytes=VMEM_LIMIT)


def _layer_norm(xf, g, b):
    mu = jnp.mean(xf, axis=-1, keepdims=True)
    xc = xf - mu
    var = jnp.mean(xc * xc, axis=-1, keepdims=True)
    return xc * lax.rsqrt(var + NORM_EPS) * g + b


def _sigmoid(x):
    return 1.0 / (1.0 + jnp.exp(-x))


def _ln0_kernel(x_ref, g_ref, b_ref, of_ref, ob_ref):
    y = _layer_norm(x_ref[...], g_ref[...], b_ref[...])
    of_ref[...] = y
    ob_ref[...] = y.astype(BF16)


def _ln0(x2, g, b, tm=512):
    t, d = x2.shape
    row = pl.BlockSpec((tm, d), lambda i: (i, 0))
    vec = pl.BlockSpec((1, d), lambda i: (0, 0))
    return pl.pallas_call(
        _ln0_kernel, grid=(t // tm,), in_specs=[row, vec, vec], out_specs=[row, row],
        out_shape=[jax.ShapeDtypeStruct((t, d), F32), jax.ShapeDtypeStruct((t, d), BF16)],
        compiler_params=_params(1), name="ln0")(x2, g.reshape(1, d), b.reshape(1, d))


def _mm_kernel(x_ref, w_ref, o_ref):
    o_ref[...] = jnp.dot(x_ref[...], w_ref[...], preferred_element_type=F32).astype(o_ref.dtype)


def _matmul(x, w, tm, tn, name):
    t, k = x.shape
    n = w.shape[1]
    return pl.pallas_call(
        _mm_kernel, grid=(t // tm, n // tn),
        in_specs=[pl.BlockSpec((tm, k), lambda i, j: (i, 0)), pl.BlockSpec((k, tn), lambda i, j: (0, j))],
        out_specs=pl.BlockSpec((tm, tn), lambda i, j: (i, j)),
        out_shape=jax.ShapeDtypeStruct((t, n), BF16),
        compiler_params=_params(2), name=name)(x, w)


def _split3(x):
    hi = x.astype(BF16).astype(F32)
    mid = (x - hi).astype(BF16).astype(F32)
    lo = (x - hi - mid).astype(BF16).astype(F32)
    return hi, mid, lo


def _attn_kernel(lam_ref, g_ref, q_ref, k_ref, v_ref, o_ref, ka_ref, vb_ref, lhs_ref, fix_ref, m_ref, acc_ref,
                 *, tq, tk, rc, seq, lam_init):
    h = pl.program_id(1)
    qi = pl.program_id(2)
    dh = DA_HEAD_DIM

    slopes = [2.0 ** (-ALIBI_MAX_EXP * (i + 1) / DA_HEADS) for i in range(DA_HEADS)]
    slope = jnp.float32(slopes[-1])
    for i in range(DA_HEADS - 1):
        slope = jnp.where(h == i, jnp.float32(slopes[i]), slope)
    sigma = slope * LOG2E

    @pl.when(qi == 0)
    def _():
        ka_ref[:LANES, :] = k_ref[0].astype(F32).T.astype(BF16)
        t_hi, t_mid, t_lo = _split3(lax.broadcasted_iota(jnp.int32, (1, seq), 1).astype(F32) * sigma)
        sub_k = lax.broadcasted_iota(jnp.int32, (LANES, seq), 0)
        aug_k = jnp.where(sub_k < 3, 1.0, jnp.where(sub_k == 3, t_hi, jnp.where(sub_k == 4, t_mid,
                          jnp.where(sub_k == 5, t_lo, 0.0))))
        ka_ref[LANES:, :] = aug_k.astype(BF16)
        lane_v = lax.broadcasted_iota(jnp.int32, (seq, LANES), 1)
        vb_ref[:, :LANES] = v_ref[0].astype(BF16)
        vb_ref[:, LANES:] = jnp.where(lane_v == 0, 1.0, 0.0).astype(BF16)
        for r in range(tk // rc):
            delta = (r * rc + lax.broadcasted_iota(jnp.int32, (rc, 1), 0)
                     - lax.broadcasted_iota(jnp.int32, (1, tk), 1)).astype(F32)
            fix_ref[r] = (2.0 * sigma) * jnp.minimum(delta, 0.0)

    q = q_ref[0].astype(F32) * (dh ** -0.5 * LOG2E)
    lane = lax.broadcasted_iota(jnp.int32, (tq, LANES), 1)
    row = lax.broadcasted_iota(jnp.int32, (tq, 1), 0)
    u_hi, u_mid, u_lo = _split3((qi * tq + row).astype(F32) * (-sigma))
    aug_q = jnp.where(lane == 0, u_hi, jnp.where(lane == 1, u_mid, jnp.where(lane == 2, u_lo,
                      jnp.where(lane < 6, 1.0, 0.0))))
    for mp in range(2):
        qz = jnp.where(lane < dh, q, 0.0) if mp == 0 else jnp.where(lane >= dh, q, 0.0)
        lhs_ref[mp, 0] = jnp.concatenate([qz, aug_q], axis=-1).astype(BF16)
        lhs_ref[mp, 1] = jnp.concatenate([qz, -aug_q], axis=-1).astype(BF16)
    m_ref[...] = jnp.full(m_ref.shape, -1e30, F32)
    acc_ref[...] = jnp.zeros(acc_ref.shape, F32)

    def block(j, local):
        off = pl.multiple_of(j * tk, tk)
        ka = ka_ref[:, pl.ds(off, tk)]
        vb = vb_ref[pl.ds(off, tk), :]
        for mp in range(2):
            for c in range(tq // rc):
                rows = slice(c * rc, (c + 1) * rc)
                if isinstance(local, str):
                    sign, fixup = (0 if local == "before" else 1), None
                elif (c + 1) * rc <= local * tk:
                    sign, fixup = 1, None
                elif c * rc >= (local + 1) * tk:
                    sign, fixup = 0, None
                else:
                    sign, fixup = 0, fix_ref[(c * rc - local * tk) // rc]
                s = jnp.dot(lhs_ref[mp, sign, rows, :], ka, preferred_element_type=F32)
                if fixup is not None:
                    s = s + fixup
                m_old = m_ref[mp, rows, :]
                m_new = jnp.maximum(m_old, jnp.max(s, axis=-1, keepdims=True))
                alpha = jnp.exp2(m_old - m_new)
                p = jnp.exp2(s - m_new).astype(BF16)
                acc_ref[mp, rows, :] = alpha * acc_ref[mp, rows, :] + jnp.dot(p, vb, preferred_element_type=F32)
                m_ref[mp, rows, :] = m_new

    def run(lo, hi, where):
        def body(j, carry):
            block(j, where)
            return carry
        lax.fori_loop(lo, hi, body, 0)

    j0 = (qi * tq) // tk
    run(0, j0, "before")
    for local in range(tq // tk):
        block(j0 + local, local)
    run(j0 + tq // tk, seq // tk, "after")

    lv = lam_ref[...]
    lam = (jnp.exp(jnp.sum(lv[0:1] * lv[1:2], axis=-1, keepdims=True))
           - jnp.exp(jnp.sum(lv[2:3] * lv[3:4], axis=-1, keepdims=True)) + lam_init)
    o1 = acc_ref[0, :, :LANES] / acc_ref[0, :, LANES:LANES + 1]
    o2 = acc_ref[1, :, :LANES] / acc_ref[1, :, LANES:LANES + 1]
    o = o1 - lam * o2
    ms = jnp.mean(o * o, axis=-1, keepdims=True)
    o_ref[0] = o * lax.rsqrt(ms + NORM_EPS) * g_ref[...] * (1.0 - lam_init)


def _diff_attention(proj_a3, da_lambda, da_norm_g, lam_init, tq=2048, tk=1024, rc=256):
    b, s, _ = proj_a3.shape
    assert tq % tk == 0 and s % tq == 0 and tk % rc == 0
    kern = functools.partial(_attn_kernel, tq=tq, tk=tk, rc=rc, seq=s, lam_init=lam_init)
    return pl.pallas_call(
        kern, grid=(b, DA_HEADS, s // tq),
        in_specs=[
            pl.BlockSpec((4, DA_HEAD_DIM), lambda bi, h, i: (0, 0)),
            pl.BlockSpec((1, DA_V_DIM), lambda bi, h, i: (0, 0)),
            pl.BlockSpec((1, tq, LANES), lambda bi, h, i: (bi, i, h)),
            pl.BlockSpec((1, s, LANES), lambda bi, h, i: (bi, 0, DA_HEADS + h)),
            pl.BlockSpec((1, s, LANES), lambda bi, h, i: (bi, 0, 2 * DA_HEADS + h)),
        ],
        out_specs=pl.BlockSpec((1, tq, LANES), lambda bi, h, i: (bi, i, h)),
        out_shape=jax.ShapeDtypeStruct((b, s, BRANCH_WIDTH), F32),
        scratch_shapes=[pltpu.VMEM((2 * LANES, s), BF16), pltpu.VMEM((s, 2 * LANES), BF16),
                        pltpu.VMEM((2, 2, tq, 2 * LANES), BF16), pltpu.VMEM((tk // rc, rc, tk), F32),
                        pltpu.VMEM((2, tq, 1), F32), pltpu.VMEM((2, tq, 2 * LANES), F32)],
        compiler_params=_params(3), name="diff_attention")(
            da_lambda, da_norm_g.reshape(1, DA_V_DIM), proj_a3, proj_a3, proj_a3)


def _s5_matrices(a_re, a_im, log_dt, b_re, b_im, c_re, c_im):
    L, C, P, G = S5_CHUNK, S5_GROUP_SIZE, S5_STATE, S5_GROUPS
    f = lambda t: t.astype(F32)
    a_re, a_im, b_re, b_im, c_re, c_im = map(f, (a_re, a_im, b_re, b_im, c_re, c_im))
    dt = jnp.exp(f(log_dt))[..., None]
    lr, li = a_re * dt, a_im * dt

    def apow(j):
        jj = j.astype(F32)[:, None, None, None]
        mag = jnp.exp(lr[None] * jj)
        return mag * jnp.cos(li[None] * jj), mag * jnp.sin(li[None] * jj)

    a1r, a1i = apow(jnp.arange(1, 2))
    nr, ni = a1r[0] - 1.0, a1i[0]
    den = a_re * a_re + a_im * a_im
    cr, ci = (nr * a_re + ni * a_im) / den, (ni * a_re - nr * a_im) / den
    bbr = cr[..., None] * b_re - ci[..., None] * b_im
    bbi = cr[..., None] * b_im + ci[..., None] * b_re

    pr, pi = apow(jnp.arange(L))
    wr = pr[..., None] * bbr[None] - pi[..., None] * bbi[None]
    wi = pr[..., None] * bbi[None] + pi[..., None] * bbr[None]
    kern = (jnp.einsum('dgcp,jdgpe->jdgce', c_re, wr, precision=HIGHEST)
            - jnp.einsum('dgcp,jdgpe->jdgce', c_im, wi, precision=HIGHEST))

    a_idx = jnp.arange(L)[:, None]
    b_idx = jnp.arange(L)[None, :]

    def toeplitz(kd):
        m = kd[jnp.clip(b_idx - a_idx, 0, L - 1)] * (b_idx >= a_idx)[:, :, None, None, None].astype(F32)
        return m.astype(BF16).transpose(2, 4, 3, 0, 1)

    same = jnp.eye(2, dtype=BF16)[None, None, :, None, None, :, None]

    def pair_matrix(tz, s_axis, t_axis):
        tz = tz.reshape(G // 2, 2, C, C, L, L).transpose(0, 2 + s_axis, 1, 2, 2 + t_axis, 3)
        return (tz[:, :, :, :, :, None, :] * same).reshape(G // 2, 2 * L * C, 2 * L * C)

    m_mat = jnp.stack([pair_matrix(toeplitz(kern[:, 0]), 2, 3),
                       pair_matrix(toeplitz(kern[:, 1]), 3, 2)])

    def summary(w_dir):
        return w_dir.transpose(1, 0, 3, 2).reshape(G, L * C, P)

    p_re = jnp.stack([summary(wr[::-1, 0]), summary(wr[:, 1])])
    p_im = jnp.stack([summary(wi[::-1, 0]), summary(wi[:, 1])])

    qr, qi = apow(jnp.arange(1, L + 1))
    car = c_re[None] * qr[:, :, :, None, :] - c_im[None] * qi[:, :, :, None, :]
    cai = c_re[None] * qi[:, :, :, None, :] + c_im[None] * qr[:, :, :, None, :]

    def readout(ca_dir):
        return ca_dir.transpose(1, 3, 0, 2).reshape(G, P, L * C)

    q_re = jnp.stack([readout(car[:, 0]), readout(car[::-1, 1])])
    q_im = jnp.stack([-readout(cai[:, 0]), -readout(cai[::-1, 1])])

    half = jax.nn.one_hot(jnp.arange(G) % 2, 2, dtype=F32)
    p_pad = jnp.stack([p_re, p_im], axis=-2)[:, :, :, :, None, :] * half[None, :, None, None, :, None]
    p_pad = p_pad.reshape(2, G // 2, 2, L, C, 4 * P).transpose(0, 1, 3, 2, 4, 5).reshape(2, G // 2, 2 * L * C, 4 * P)
    q_pad = jnp.stack([q_re, q_im], axis=2)[:, :, :, None] * half[None, :, None, :, None, None]
    q_pad = q_pad.reshape(2, G // 2, 2, 4 * P, L, C).transpose(0, 1, 3, 4, 2, 5).reshape(2, G // 2, 4 * P, 2 * L * C)

    alr, ali = apow(jnp.arange(L, L + 1))
    a_l = jnp.concatenate([alr[0].reshape(2, G // 2, 2 * P), ali[0].reshape(2, G // 2, 2 * P)], axis=-1)
    return m_mat, p_pad.astype(BF16), q_pad.astype(BF16), a_l.reshape(2, G // 2, 1, 4 * P)


def _s5_kernel(x_ref, m_ref, p_ref, q_ref, al_ref, d_ref, y_ref, s_ref, xin_ref, *, n_chunks, bsz):
    hp = 2 * S5_STATE
    xb = x_ref[0].astype(BF16)
    for d in range(2):
        s = jnp.dot(xb, p_ref[d, 0], preferred_element_type=F32)
        s_ref[2 * d] = s[:, :hp]
        s_ref[2 * d + 1] = s[:, hp:]

    al = [al_ref[d, 0] for d in range(2)]
    alr = [a[:, :hp] for a in al]
    ali = [a[:, hp:] for a in al]

    def body(i, carry):
        new = []
        for d in range(2):
            re, im = carry[2 * d], carry[2 * d + 1]
            n = i if d == 0 else n_chunks - 1 - i
            rows = pl.ds(n, bsz, stride=n_chunks)
            xin_ref[2 * d, rows, :] = re
            xin_ref[2 * d + 1, rows, :] = im
            new.append(alr[d] * re - ali[d] * im + s_ref[2 * d, rows, :])
            new.append(alr[d] * im + ali[d] * re + s_ref[2 * d + 1, rows, :])
        return tuple(new)

    zero = jnp.zeros((bsz, hp), F32)
    lax.fori_loop(0, n_chunks, body, (zero, zero, zero, zero))

    xin = [jnp.concatenate([xin_ref[2 * d], xin_ref[2 * d + 1]], axis=-1).astype(BF16) for d in range(2)]
    y = x_ref[0] * d_ref[0]
    for d in range(2):
        y = y + jnp.dot(xb, m_ref[d, 0], preferred_element_type=F32)
        y = y + jnp.dot(xin[d], q_ref[d, 0], preferred_element_type=F32)
    y_ref[0] = y


S5_PAIR = 2 * S5_GROUP_SIZE
S5_PER_VREG = LANES // S5_PAIR


def _s5_proj_kernel(x_ref, w_ref, o_ref, tok_ref):
    c = S5_PAIR
    rows = o_ref.shape[1]
    res = jnp.dot(x_ref[...], w_ref[...], preferred_element_type=F32)
    for j in range(tok_ref.shape[0]):
        tok_ref[j] = res[:, j * LANES:(j + 1) * LANES]
    for j in range(tok_ref.shape[0]):
        for t in range(S5_CHUNK):
            blk = tok_ref[j, pl.ds(t, rows, stride=S5_CHUNK), :]
            for gg in range(S5_PER_VREG):
                o_ref[j * S5_PER_VREG + gg, :, t * c:(t + 1) * c] = blk[:, gg * c:(gg + 1) * c]


def _s5_unpack_kernel(y_ref, o_ref, tok_ref):
    c = S5_PAIR
    rows = y_ref.shape[1]
    for j in range(tok_ref.shape[0]):
        for t in range(S5_CHUNK):
            blk = jnp.concatenate([y_ref[j * S5_PER_VREG + gg, :, t * c:(t + 1) * c] for gg in range(S5_PER_VREG)],
                                  axis=-1)
            tok_ref[j, pl.ds(t, rows, stride=S5_CHUNK), :] = blk
    for j in range(tok_ref.shape[0]):
        o_ref[:, j * LANES:(j + 1) * LANES] = tok_ref[j]


def _s5_proj(xb, w_u, tm=256):
    t, d = xb.shape
    rows = t // S5_CHUNK
    lc = S5_CHUNK * S5_PAIR
    w = w_u.shape[1]
    return pl.pallas_call(
        _s5_proj_kernel, grid=(rows // tm,),
        in_specs=[pl.BlockSpec((tm * S5_CHUNK, d), lambda i: (i, 0)), pl.BlockSpec(w_u.shape, lambda i: (0, 0))],
        out_specs=pl.BlockSpec((S5_GROUPS // 2, tm, lc), lambda i: (0, i, 0)),
        out_shape=jax.ShapeDtypeStruct((S5_GROUPS // 2, rows, lc), F32),
        scratch_shapes=[pltpu.VMEM((w // LANES, tm * S5_CHUNK, LANES), F32)],
        compiler_params=_params(1), name="s5_proj")(xb, w_u)


def _s5(x, mats, d_skip, b, s):
    L, C, G = S5_CHUNK, S5_GROUP_SIZE, S5_GROUPS
    w = G * C
    n = s // L
    lc = L * S5_PAIR
    st = 4 * S5_STATE
    m_mat, p_pad, q_pad, a_l = mats
    d_row = jnp.tile(d_skip.astype(F32).reshape(G // 2, 1, S5_PAIR), (1, L, 1)).reshape(G // 2, 1, lc)
    spec = lambda r, c_: pl.BlockSpec((2, 1, r, c_), lambda k: (0, k, 0, 0))
    kern = functools.partial(_s5_kernel, n_chunks=n, bsz=b)
    y = pl.pallas_call(
        kern, grid=(G // 2,),
        in_specs=[pl.BlockSpec((1, n * b, lc), lambda k: (k, 0, 0)), spec(lc, lc), spec(lc, st), spec(st, lc),
                  spec(1, st), pl.BlockSpec((1, 1, lc), lambda k: (k, 0, 0))],
        out_specs=pl.BlockSpec((1, n * b, lc), lambda k: (k, 0, 0)),
        out_shape=jax.ShapeDtypeStruct((G // 2, n * b, lc), F32),
        scratch_shapes=[pltpu.VMEM((4, n * b, LANES), F32), pltpu.VMEM((4, n * b, LANES), F32)],
        compiler_params=_params(1), name="s5")(x, m_mat, p_pad, q_pad, a_l, d_row)
    tm = 256
    return pl.pallas_call(
        _s5_unpack_kernel, grid=(b * n // tm,),
        in_specs=[pl.BlockSpec((G // 2, tm, lc), lambda i: (0, i, 0))],
        out_specs=pl.BlockSpec((tm * L, w), lambda i: (i, 0)),
        out_shape=jax.ShapeDtypeStruct((b * s, w), F32),
        scratch_shapes=[pltpu.VMEM((w // LANES, tm * L, LANES), F32)],
        compiler_params=_params(1), name="s5_unpack")(y)


def _gla_kernel(qf_ref, kf_ref, vf_ref, gf_ref, qb_ref, kb_ref, vb_ref, gb_ref, bg_ref,
                of_ref, ob_ref, st_ref, *, rows):
    c = GLA_CHUNK
    n_sub = rows // c

    @pl.when(pl.program_id(1) == 0)
    def _():
        st_ref[...] = jnp.zeros_like(st_ref)

    nt = (((1,), (1,)), ((), ()))
    tn = (((0,), (0,)), ((), ()))
    log2 = lambda n: int(math.log2(n))
    r_r = lax.broadcasted_iota(jnp.int32, (rows, rows), 0)
    c_r = lax.broadcasted_iota(jnp.int32, (rows, rows), 1)
    same_chunk = (r_r >> log2(c)) == (c_r >> log2(c))
    t_row = lax.broadcasted_iota(jnp.int32, (c, GLA_K), 0)
    t_col = lax.broadcasted_iota(jnp.int32, (c, GLA_K), 1) & (c - 1)
    k_lane = lax.broadcasted_iota(jnp.int32, (1, GLA_K), 1) >> log2(GLA_DK)
    v_lane = lax.broadcasted_iota(jnp.int32, (1, BRANCH_WIDTH), 1) >> log2(GLA_DV)
    st_same_head = ((lax.broadcasted_iota(jnp.int32, (BRANCH_WIDTH, GLA_K), 0) >> log2(GLA_DV))
                    == (lax.broadcasted_iota(jnp.int32, (BRANCH_WIDTH, GLA_K), 1) >> log2(GLA_DK)))

    for d, (q_ref, k_ref, v_ref, g_ref, o_ref) in enumerate(
            ((qf_ref, kf_ref, vf_ref, gf_ref, of_ref), (qb_ref, kb_ref, vb_ref, gb_ref, ob_ref))):
        gate = g_ref[0].astype(F32) + bg_ref[d]
        log_a = (jnp.minimum(gate, 0.0) - jnp.log(1.0 + jnp.exp(-jnp.abs(gate)))) * (1.0 / GLA_TAU)
        if d == 0:
            tri = jnp.where(same_chunk, jnp.where(r_r >= c_r, 1.0, 0.0), 0.0).astype(BF16)
            keep, i_ref, i_last = (t_row >= t_col), c // 2, c - 1
        else:
            tri = jnp.where(same_chunk, jnp.where(c_r >= r_r, 1.0, 0.0), 0.0).astype(BF16)
            keep, i_ref, i_last = (t_col > t_row), c - 1 - c // 2, 0
        la_hi = log_a.astype(BF16)
        la_lo = (log_a - la_hi.astype(F32)).astype(BF16)
        cum_all = (jnp.dot(tri, la_hi, preferred_element_type=F32) + jnp.dot(tri, la_lo, preferred_element_type=F32))
        q = q_ref[0].astype(F32) * (GLA_DK ** -0.5)
        k = k_ref[0].astype(F32)
        v = v_ref[0]
        for ci in (range(n_sub) if d == 0 else range(n_sub - 1, -1, -1)):
            sl = slice(ci * c, (ci + 1) * c)
            cum = cum_all[sl]
            mid = cum[i_ref:i_ref + 1]
            last = cum[i_last:i_last + 1]
            qc, kc, vc = q[sl], k[sl], v[sl]
            qe = (qc * jnp.exp(cum - mid)).astype(BF16)
            ke = kc * jnp.exp(mid - cum)
            kd = (kc * jnp.exp(last - cum)).astype(BF16)
            qs = (qc * jnp.exp(cum)).astype(BF16)
            decay = jnp.exp(last)
            ke4 = jnp.concatenate([jnp.where(k_lane == h, ke, 0.0) for h in range(GLA_HEADS)], axis=0).astype(BF16)
            v4 = jnp.concatenate([jnp.where(v_lane == h, vc, 0.0) for h in range(GLA_HEADS)], axis=0).astype(BF16)
            sc = lax.dot_general(qe, ke4, nt, preferred_element_type=F32)
            sc = jnp.where(keep, sc, 0.0).astype(BF16)
            st = st_ref[d]
            o = (jnp.dot(sc, v4, preferred_element_type=F32)
                 + lax.dot_general(qs, st.astype(BF16), nt, preferred_element_type=F32))
            o_ref[0, sl, :] = o
            ds = lax.dot_general(vc.astype(BF16), kd, tn, preferred_element_type=F32)
            for h in range(GLA_HEADS):
                rs = slice(h * GLA_DV, (h + 1) * GLA_DV)
                ls = slice(h * GLA_DK // LANES * LANES, h * GLA_DK // LANES * LANES + LANES)
                st_ref[d, rs, ls] = decay[:, ls] * st[rs, ls] + jnp.where(st_same_head[rs, ls], ds[rs, ls], 0.0)


def _gla(proj_a3, proj_b3, b_gate, rows=128):
    b, s, _ = proj_a3.shape
    n = s // rows
    bg = b_gate.astype(F32).reshape(2, 1, GLA_K)

    def specs(rev):
        blk = (lambda i: n - 1 - i) if rev else (lambda i: i)
        return [pl.BlockSpec((1, rows, GLA_K), lambda bi, i: (bi, blk(i), A_QC // GLA_K)),
                pl.BlockSpec((1, rows, GLA_K), lambda bi, i: (bi, blk(i), A_KC // GLA_K)),
                pl.BlockSpec((1, rows, BRANCH_WIDTH), lambda bi, i: (bi, blk(i), A_VC // BRANCH_WIDTH)),
                pl.BlockSpec((1, rows, GLA_K), lambda bi, i: (bi, blk(i), GATE_COLS // GLA_K + int(rev)))]

    out_f = pl.BlockSpec((1, rows, BRANCH_WIDTH), lambda bi, i: (bi, i, 0))
    out_b = pl.BlockSpec((1, rows, BRANCH_WIDTH), lambda bi, i: (bi, n - 1 - i, 0))
    shape = jax.ShapeDtypeStruct((b, s, BRANCH_WIDTH), F32)
    return pl.pallas_call(
        functools.partial(_gla_kernel, rows=rows), grid=(b, n),
        in_specs=specs(False) + specs(True) + [pl.BlockSpec((2, 1, GLA_K), lambda bi, i: (0, 0, 0))],
        out_specs=[out_f, out_b], out_shape=[shape, shape],
        scratch_shapes=[pltpu.VMEM((2, BRANCH_WIDTH, GLA_K), F32)],
        compiler_params=_params(2), name="gla")(
            proj_a3, proj_a3, proj_a3, proj_b3, proj_a3, proj_a3, proj_a3, proj_b3, bg)


def _ranks(vals):
    ranks = []
    for i, v in enumerate(vals):
        r = jnp.zeros(v.shape, F32)
        for o, other in enumerate(vals):
            if o != i:
                r = r + (jnp.where(other >= v, 1.0, 0.0) if o < i else jnp.where(other > v, 1.0, 0.0))
        ranks.append(r)
    return ranks


def _route(scores_t, bias_col):
    gs = EXPERTS_PER_GROUP
    s = [scores_t[e:e + 1] for e in range(N_EXPERTS)]
    biased = [s[e] + bias_col[e:e + 1] for e in range(N_EXPERTS)]
    top2, group_score = [], []
    for g in range(N_EXPERTS // gs):
        rk = _ranks(biased[g * gs:(g + 1) * gs])
        top2 += [r < 2.0 for r in rk]
        kept = [jnp.where(top2[g * gs + j], biased[g * gs + j], 0.0) for j in range(gs)]
        group_score.append(functools.reduce(lambda a, b: a + b, kept))
    best = [r < 1.0 for r in _ranks(group_score)]
    w = [jnp.where(top2[e], jnp.where(best[e // gs], s[e], 0.0), 0.0) for e in range(N_EXPERTS)]
    total = functools.reduce(lambda a, b: a + b, w)
    return [we / total for we in w]


def _merge_kernel(x_ref, oa_ref, ys_ref, gf_ref, gb_ref, r_ref, gl0_ref, gl1_ref, gl2_ref,
                  wglu_ref, bglu_ref, gng_ref, wup_ref, mb_ref, wout_ref, lng_ref, lnb_ref, rw_ref, rb_ref,
                  xo_ref, xob_ref, comb_ref):
    y = ys_ref[...]
    gelu = 0.5 * y * (1.0 + jnp.tanh(math.sqrt(2.0 / math.pi) * (y + 0.044715 * (y * y * y))))
    o_b = gelu * _sigmoid(jnp.dot(gelu.astype(BF16), wglu_ref[...], preferred_element_type=F32) + bglu_ref[...])

    r = r_ref[...].astype(F32)
    gated = r * _sigmoid(r)
    heads = []
    for h in range(GLA_HEADS):
        sl = slice(h * GLA_DV, (h + 1) * GLA_DV)
        o = gf_ref[:, sl] + gb_ref[:, sl]
        ms = jnp.mean(o * o, axis=-1, keepdims=True)
        heads.append(o * lax.rsqrt(ms + NORM_EPS) * gng_ref[...] * gated[:, sl])
    o_c = jnp.concatenate(heads, axis=-1)

    merged = None
    for n, (o_n, gl_ref) in enumerate(((oa_ref[...], gl0_ref), (o_b, gl1_ref), (o_c, gl2_ref))):
        gate = _sigmoid(gl_ref[...].astype(F32) + mb_ref[n])
        term = gate * jnp.dot(o_n.astype(BF16), wup_ref[n], preferred_element_type=F32)
        merged = term if merged is None else merged + term
    mix = jnp.dot(merged.astype(BF16), wout_ref[...], preferred_element_type=F32)
    x1 = _layer_norm(DN_ALPHA * x_ref[...] + mix, lng_ref[...], lnb_ref[...])
    xo_ref[...] = x1
    x_hi = x1.astype(BF16)
    xob_ref[...] = x_hi

    x_lo = (x1 - x_hi.astype(F32)).astype(BF16)
    logits = (jnp.dot(x_hi, rw_ref[0], preferred_element_type=F32) + jnp.dot(x_lo, rw_ref[0], preferred_element_type=F32)
              + jnp.dot(x_hi, rw_ref[1], preferred_element_type=F32))
    rows = _route(_sigmoid(logits.T[:N_EXPERTS]), rb_ref[...])
    comb_ref[...] = jnp.zeros(comb_ref.shape, F32)
    for e in range(N_EXPERTS):
        comb_ref[e:e + 1, :] = rows[e]


def _merge(x, o_a, y_s5, g_f, g_b, proj_a, proj_b, wglu, bglu, gng, wup, mb, wout, lng, lnb, rw, rb, tm=256):
    t, d = x.shape
    w = BRANCH_WIDTH
    row = lambda width, cb=0: pl.BlockSpec((tm, width), lambda i: (i, cb))
    full = lambda a: pl.BlockSpec(a.shape, lambda i: (0,) * a.ndim)
    weights = (wglu, bglu, gng, wup, mb, wout, lng, lnb, rw, rb)
    return pl.pallas_call(
        _merge_kernel, grid=(t // tm,),
        in_specs=[row(d), row(w), row(w), row(w), row(w), row(w, A_RC // w), row(d, 0), row(d, 1), row(d, 2)]
        + [full(a) for a in weights],
        out_specs=[row(d), row(d), pl.BlockSpec((LANES, tm), lambda i: (0, i))],
        out_shape=[jax.ShapeDtypeStruct((t, d), F32), jax.ShapeDtypeStruct((t, d), BF16),
                   jax.ShapeDtypeStruct((LANES, t), F32)],
        compiler_params=_params(1), name="merge")(
            x, o_a, y_s5, g_f, g_b, proj_a, proj_b, proj_b, proj_b, *weights)


MOE_SUB = 256
MOE_CAP = 64


def _swiglu(x, wg_ref, wu_ref, wd_ref, row_scale):
    g = jnp.dot(x, wg_ref[0], preferred_element_type=F32)
    u = jnp.dot(x, wu_ref[0], preferred_element_type=F32)
    h = (g * _sigmoid(g)) * u * row_scale
    return jnp.dot(h.astype(BF16), wd_ref[0], preferred_element_type=F32)


def _moe_kernel(xb_ref, xf_ref, comb_ref, wg_ref, wu_ref, wd_ref, lng_ref, lnb_ref, of_ref, ob_ref,
                pos_ref, ind_ref, sel_ref, xg_ref, y_ref, cg_ref, acc_ref, flag_ref, *, tm, cap):
    e = pl.program_id(1)
    ns = tm // MOE_SUB
    gs = EXPERTS_PER_GROUP
    sub = lambda k: slice(k * MOE_SUB, (k + 1) * MOE_SUB)
    slots = lambda k: slice(k * cap, (k + 1) * cap)
    r_i = lax.broadcasted_iota(jnp.int32, (MOE_SUB, MOE_SUB), 0)
    c_i = lax.broadcasted_iota(jnp.int32, (MOE_SUB, MOE_SUB), 1)

    @pl.when(e == 0)
    def _():
        ind = jnp.where(comb_ref[...] > 0.0, 1.0, 0.0)
        ind_ref[...] = ind
        before = jnp.where(r_i < c_i, 1.0, 0.0).astype(BF16)
        most = jnp.zeros((LANES, 1), F32)
        for k in range(ns):
            pos_ref[:, sub(k)] = jnp.dot(ind[:, sub(k)].astype(BF16), before, preferred_element_type=F32)
            most = jnp.maximum(most, jnp.sum(ind[:, sub(k)], axis=-1, keepdims=True))
        flag_ref[0] = (jnp.max(most) > cap).astype(jnp.int32)
        acc_ref[...] = jnp.zeros_like(acc_ref)

    slot_id = lax.broadcasted_iota(jnp.int32, (cap, 1), 0).astype(F32)
    for grp in range(N_EXPERTS // gs):
        @pl.when(e == grp * gs)
        def _(grp=grp):
            for k in range(ns):
                pieces = []
                for j in range(gs):
                    ee = grp * gs + j
                    one_hot = jnp.where(pos_ref[ee:ee + 1, sub(k)] == slot_id, ind_ref[ee:ee + 1, sub(k)], 0.0)
                    cg_ref[j, slots(k), :] = jnp.sum(one_hot * comb_ref[ee:ee + 1, sub(k)], axis=-1, keepdims=True)
                    pieces.append(one_hot)
                sel = jnp.concatenate(pieces, axis=0).astype(BF16)
                sel_ref[k] = sel
                xg = jnp.dot(sel, xb_ref[sub(k), :], preferred_element_type=F32).astype(BF16)
                for j in range(gs):
                    xg_ref[j, slots(k), :] = xg[j * cap:(j + 1) * cap]

    j = e % gs
    y_ref[j] = _swiglu(xg_ref[j], wg_ref, wu_ref, wd_ref, cg_ref[j]).astype(BF16)

    @pl.when(j == gs - 1)
    def _():
        for k in range(ns):
            yk = jnp.concatenate([y_ref[jj, slots(k), :] for jj in range(gs)], axis=0)
            acc_ref[sub(k), :] += lax.dot_general(sel_ref[k], yk, (((0,), (0,)), ((), ())),
                                                  preferred_element_type=F32)

    @pl.when(flag_ref[0] > 0)
    def _():
        comb = comb_ref[...].T
        earlier = jnp.where(r_i > c_i, 1.0, 0.0).astype(BF16)
        over = []
        for k in range(ns):
            pos = jnp.dot(earlier, jnp.where(comb[sub(k)] > 0.0, 1.0, 0.0).astype(BF16), preferred_element_type=F32)
            over.append(jnp.where(pos >= cap, comb[sub(k)], 0.0))
        over = jnp.concatenate(over, axis=0)
        lane = lax.broadcasted_iota(jnp.int32, over.shape, 1)
        ce = jnp.sum(jnp.where(lane == e, over, 0.0), axis=-1, keepdims=True)
        acc_ref[...] += _swiglu(xb_ref[...], wg_ref, wu_ref, wd_ref, ce)

    @pl.when(e == N_EXPERTS - 1)
    def _():
        y = _layer_norm(DN_ALPHA * xf_ref[...] + acc_ref[...], lng_ref[...], lnb_ref[...])
        of_ref[...] = y
        ob_ref[...] = y.astype(BF16)


def _moe(xb, xf, comb, wg, wu, wd, lng, lnb, tm=1024, cap=MOE_CAP):
    t, d = xf.shape
    ns = tm // MOE_SUB
    gs = EXPERTS_PER_GROUP
    row = lambda width: pl.BlockSpec((tm, width), lambda i, e: (i, 0))
    vec = pl.BlockSpec((1, d), lambda i, e: (0, 0))
    return pl.pallas_call(
        functools.partial(_moe_kernel, tm=tm, cap=cap), grid=(t // tm, N_EXPERTS),
        in_specs=[row(d), row(d), pl.BlockSpec((LANES, tm), lambda i, e: (0, i)),
                  pl.BlockSpec((1, d, EXPERT_FF), lambda i, e: (e, 0, 0)),
                  pl.BlockSpec((1, d, EXPERT_FF), lambda i, e: (e, 0, 0)),
                  pl.BlockSpec((1, EXPERT_FF, d), lambda i, e: (e, 0, 0)), vec, vec],
        out_specs=[row(d), row(d)],
        out_shape=[jax.ShapeDtypeStruct((t, d), F32), jax.ShapeDtypeStruct((t, d), BF16)],
        scratch_shapes=[pltpu.VMEM((LANES, tm), F32), pltpu.VMEM((LANES, tm), F32),
                        pltpu.VMEM((ns, gs * cap, MOE_SUB), BF16), pltpu.VMEM((gs, ns * cap, d), BF16),
                        pltpu.VMEM((gs, ns * cap, d), BF16), pltpu.VMEM((gs, ns * cap, 1), F32),
                        pltpu.VMEM((tm, d), F32), pltpu.SMEM((1,), jnp.int32)],
        compiler_params=_params(2), name="moe")(xb, xf, comb, wg, wu, wd, lng, lnb)


def _proj_b_weights(w_in_l, w_gate):
    z0 = W_IN_Z
    folded = [jnp.dot(w_in_l[:, z0 + i * GLA_GATE_RANK:z0 + (i + 1) * GLA_GATE_RANK].astype(F32),
                      w_gate[i].astype(F32), precision=HIGHEST) for i in range(2)]
    return jnp.concatenate([w_in_l[:, z0 + 2 * GLA_GATE_RANK:].astype(F32)] + folded, axis=1).astype(BF16)


def kernel(x, ln0_g, ln0_b, w_in, da_lambda, da_norm_g, s5_a_re, s5_a_im, s5_log_dt, s5_b_re, s5_b_im,
           s5_c_re, s5_c_im, s5_d, s5_w_glu, s5_b_glu, gla_w_gate, gla_b_gate, gla_norm_g, merge_w_up,
           merge_b, w_out, ln1_g, ln1_b, router_w, router_bias, moe_w_gate, moe_w_up, moe_w_down,
           ln2_g, ln2_b):
    b, s, d = x.shape
    t = b * s
    vec = lambda a: a.astype(F32).reshape(1, -1)
    xf, xb = _ln0(x.reshape(t, d), ln0_g.astype(F32), ln0_b.astype(F32))
    rw32 = jnp.zeros((d, LANES), F32).at[:, :N_EXPERTS].set(router_w.astype(F32))
    rw_hi = rw32.astype(BF16)
    rw = jnp.stack([rw_hi, (rw32 - rw_hi.astype(F32)).astype(BF16)])
    rb = router_bias.astype(F32).reshape(N_EXPERTS, 1)
    for l in range(DEPTH):
        w_a = jnp.concatenate([w_in[l][:, :W_IN_U], w_in[l][:, W_IN_GLA:W_IN_Z]], axis=1).astype(BF16)
        w_u = w_in[l][:, W_IN_U:W_IN_GLA].astype(BF16)
        w_b = _proj_b_weights(w_in[l], gla_w_gate[l])
        proj_a = _matmul(xb, w_a, 1024, PROJ_A // 2, "in_proj_a")
        proj_b = _matmul(xb, w_b, 512, PROJ_B, "in_proj_b")
        proj_a3 = proj_a.reshape(b, s, PROJ_A)
        proj_b3 = proj_b.reshape(b, s, PROJ_B)

        lam_init = 0.8 - 0.6 * math.exp(-0.3 * l)
        o_a = _diff_attention(proj_a3, da_lambda[l].astype(F32), da_norm_g[l].astype(F32), lam_init)
        mats = _s5_matrices(s5_a_re[l], s5_a_im[l], s5_log_dt[l], s5_b_re[l], s5_b_im[l], s5_c_re[l], s5_c_im[l])
        y_s5 = _s5(_s5_proj(xb, w_u), mats, s5_d[l], b, s)
        g_f, g_b = _gla(proj_a3, proj_b3, gla_b_gate[l])

        xf, xb, comb = _merge(
            xf, o_a.reshape(t, BRANCH_WIDTH), y_s5, g_f.reshape(t, BRANCH_WIDTH), g_b.reshape(t, BRANCH_WIDTH),
            proj_a, proj_b, s5_w_glu[l].astype(BF16), vec(s5_b_glu[l]), vec(gla_norm_g[l]),
            merge_w_up[l].astype(BF16), merge_b[l].astype(F32).reshape(N_BRANCH, 1, d), w_out[l].astype(BF16),
            vec(ln1_g[l]), vec(ln1_b[l]), rw, rb)
        xf, xb = _moe(xb, xf, comb, moe_w_gate[l].astype(BF16), moe_w_up[l].astype(BF16),
                      moe_w_down[l].astype(BF16), vec(ln2_g[l]), vec(ln2_b[l]))
    return xf.reshape(b, s, d)
```

```python
import functools
import math

import jax
import jax.numpy as jnp
from jax import lax
from jax.experimental import pallas as pl
from jax.experimental.pallas import tpu as pltpu

F32 = jnp.float32
BF16 = jnp.bfloat16
HIGHEST = lax.Precision.HIGHEST

D_MODEL = 1024
DEPTH = 2
BRANCH_WIDTH = D_MODEL // 2
N_BRANCH = 3

DA_HEADS = 4
DA_HEAD_DIM = BRANCH_WIDTH // (2 * DA_HEADS)
DA_V_DIM = 2 * DA_HEAD_DIM
ALIBI_MAX_EXP = 8.0

S5_GROUP_SIZE = 16
S5_GROUPS = BRANCH_WIDTH // S5_GROUP_SIZE
S5_STATE = 64
S5_CHUNK = 16

GLA_HEADS = 4
GLA_DV = BRANCH_WIDTH // GLA_HEADS
GLA_DK = GLA_DV // 2
GLA_K = GLA_HEADS * GLA_DK
GLA_GATE_RANK = 16
GLA_TAU = 16.0
GLA_CHUNK = 64

N_EXPERTS = 16
EXPERTS_PER_GROUP = 4
EXPERT_FF = D_MODEL // 2

DN_ALPHA = (2.0 * DEPTH) ** 0.25
NORM_EPS = 1e-5
LOG2E = math.log2(math.e)

LANES = 128
W_IN_U = 3 * BRANCH_WIDTH
W_IN_GLA = W_IN_U + BRANCH_WIDTH
W_IN_Z = W_IN_GLA + 2 * GLA_K + 2 * BRANCH_WIDTH
A_QC = W_IN_U
A_KC = A_QC + GLA_K
A_VC = A_KC + GLA_K
A_RC = A_VC + BRANCH_WIDTH
PROJ_A = A_RC + BRANCH_WIDTH
GATE_COLS = N_BRANCH * D_MODEL
PROJ_B = GATE_COLS + 2 * GLA_K
VMEM_LIMIT = 56 * 1024 * 1024


def _params(n_grid_dims):
    return pltpu.CompilerParams(dimension_semantics=("arbitrary",) * n_grid_dims,
                                vmem_limit_bytes=VMEM_LIMIT)


def _layer_norm(xf, g, b):
    mu = jnp.mean(xf, axis=-1, keepdims=True)
    xc = xf - mu
    var = jnp.mean(xc * xc, axis=-1, keepdims=True)
    return xc * lax.rsqrt(var + NORM_EPS) * g + b


def _sigmoid(x):
    return 1.0 / (1.0 + jnp.exp(-x))


def _ln0_kernel(x_ref, g_ref, b_ref, of_ref, ob_ref):
    y = _layer_norm(x_ref[...], g_ref[...], b_ref[...])
    of_ref[...] = y
    ob_ref[...] = y.astype(BF16)


def _ln0(x2, g, b, tm=512):
    t, d = x2.shape
    row = pl.BlockSpec((tm, d), lambda i: (i, 0))
    vec = pl.BlockSpec((1, d), lambda i: (0, 0))
    return pl.pallas_call(
        _ln0_kernel, grid=(t // tm,), in_specs=[row, vec, vec], out_specs=[row, row],
        out_shape=[jax.ShapeDtypeStruct((t, d), F32), jax.ShapeDtypeStruct((t, d), BF16)],
        compiler_params=_params(1), name="ln0")(x2, g.reshape(1, d), b.reshape(1, d))


def _mm_kernel(x_ref, w_ref, o_ref):
    o_ref[...] = jnp.dot(x_ref[...], w_ref[...], preferred_element_type=F32).astype(o_ref.dtype)


def _matmul(x, w, tm, tn, name):
    t, k = x.shape
    n = w.shape[1]
    return pl.pallas_call(
        _mm_kernel, grid=(t // tm, n // tn),
        in_specs=[pl.BlockSpec((tm, k), lambda i, j: (i, 0)), pl.BlockSpec((k, tn), lambda i, j: (0, j))],
        out_specs=pl.BlockSpec((tm, tn), lambda i, j: (i, j)),
        out_shape=jax.ShapeDtypeStruct((t, n), BF16),
        compiler_params=_params(2), name=name)(x, w)


def _split3(x):
    hi = x.astype(BF16).astype(F32)
    mid = (x - hi).astype(BF16).astype(F32)
    lo = (x - hi - mid).astype(BF16).astype(F32)
    return hi, mid, lo


def _attn_kernel(lam_ref, g_ref, q_ref, k_ref, v_ref, o_ref, ka_ref, vb_ref, lhs_ref, fix_ref, m_ref, acc_ref,
                 *, tq, tk, rc, seq, lam_init):
    h = pl.program_id(1)
    qi = pl.program_id(2)
    dh = DA_HEAD_DIM

    slopes = [2.0 ** (-ALIBI_MAX_EXP * (i + 1) / DA_HEADS) for i in range(DA_HEADS)]
    slope = jnp.float32(slopes[-1])
    for i in range(DA_HEADS - 1):
        slope = jnp.where(h == i, jnp.float32(slopes[i]), slope)
    sigma = slope * LOG2E

    @pl.when(qi == 0)
    def _():
        ka_ref[:LANES, :] = k_ref[0].astype(F32).T.astype(BF16)
        t_hi, t_mid, t_lo = _split3(lax.broadcasted_iota(jnp.int32, (1, seq), 1).astype(F32) * sigma)
        sub_k = lax.broadcasted_iota(jnp.int32, (LANES, seq), 0)
        aug_k = jnp.where(sub_k < 3, 1.0, jnp.where(sub_k == 3, t_hi, jnp.where(sub_k == 4, t_mid,
                          jnp.where(sub_k == 5, t_lo, 0.0))))
        ka_ref[LANES:, :] = aug_k.astype(BF16)
        lane_v = lax.broadcasted_iota(jnp.int32, (seq, LANES), 1)
        vb_ref[:, :LANES] = v_ref[0].astype(BF16)
        vb_ref[:, LANES:] = jnp.where(lane_v == 0, 1.0, 0.0).astype(BF16)
        for r in range(tk // rc):
            delta = (r * rc + lax.broadcasted_iota(jnp.int32, (rc, 1), 0)
                     - lax.broadcasted_iota(jnp.int32, (1, tk), 1)).astype(F32)
            fix_ref[r] = (2.0 * sigma) * jnp.minimum(delta, 0.0)

    q = q_ref[0].astype(F32) * (dh ** -0.5 * LOG2E)
    lane = lax.broadcasted_iota(jnp.int32, (tq, LANES), 1)
    row = lax.broadcasted_iota(jnp.int32, (tq, 1), 0)
    u_hi, u_mid, u_lo = _split3((qi * tq + row).astype(F32) * (-sigma))
    aug_q = jnp.where(lane == 0, u_hi, jnp.where(lane == 1, u_mid, jnp.where(lane == 2, u_lo,
                      jnp.where(lane < 6, 1.0, 0.0))))
    for mp in range(2):
        qz = jnp.where(lane < dh, q, 0.0) if mp == 0 else jnp.where(lane >= dh, q, 0.0)
        lhs_ref[mp, 0] = jnp.concatenate([qz, aug_q], axis=-1).astype(BF16)
        lhs_ref[mp, 1] = jnp.concatenate([qz, -aug_q], axis=-1).astype(BF16)
    m_ref[...] = jnp.full(m_ref.shape, -1e30, F32)
    acc_ref[...] = jnp.zeros(acc_ref.shape, F32)

    def block(j, local):
        off = pl.multiple_of(j * tk, tk)
        ka = ka_ref[:, pl.ds(off, tk)]
        vb = vb_ref[pl.ds(off, tk), :]
        for mp in range(2):
            for c in range(tq // rc):
                rows = slice(c * rc, (c + 1) * rc)
                if isinstance(local, str):
                    sign, fixup = (0 if local == "before" else 1), None
                elif (c + 1) * rc <= local * tk:
                    sign, fixup = 1, None
                elif c * rc >= (local + 1) * tk:
                    sign, fixup = 0, None
                else:
                    sign, fixup = 0, fix_ref[(c * rc - local * tk) // rc]
                s = jnp.dot(lhs_ref[mp, sign, rows, :], ka, preferred_element_type=F32)
                if fixup is not None:
                    s = s + fixup
                m_old = m_ref[mp, rows, :]
                m_new = jnp.maximum(m_old, jnp.max(s, axis=-1, keepdims=True))
                alpha = jnp.exp2(m_old - m_new)
                p = jnp.exp2(s - m_new).astype(BF16)
                acc_ref[mp, rows, :] = alpha * acc_ref[mp, rows, :] + jnp.dot(p, vb, preferred_element_type=F32)
                m_ref[mp, rows, :] = m_new

    def run(lo, hi, where):
        def body(j, carry):
            block(j, where)
            return carry
        lax.fori_loop(lo, hi, body, 0)

    j0 = (qi * tq) // tk
    run(0, j0, "before")
    for local in range(tq // tk):
        block(j0 + local, local)
    run(j0 + tq // tk, seq // tk, "after")

    lv = lam_ref[...]
    lam = (jnp.exp(jnp.sum(lv[0:1] * lv[1:2], axis=-1, keepdims=True))
           - jnp.exp(jnp.sum(lv[2:3] * lv[3:4], axis=-1, keepdims=True)) + lam_init)
    o1 = acc_ref[0, :, :LANES] / acc_ref[0, :, LANES:LANES + 1]
    o2 = acc_ref[1, :, :LANES] / acc_ref[1, :, LANES:LANES + 1]
    o = o1 - lam * o2
    ms = jnp.mean(o * o, axis=-1, keepdims=True)
    o_ref[0] = o * lax.rsqrt(ms + NORM_EPS) * g_ref[...] * (1.0 - lam_init)


def _diff_attention(proj_a3, da_lambda, da_norm_g, lam_init, tq=2048, tk=1024, rc=256):
    b, s, _ = proj_a3.shape
    assert tq % tk == 0 and s % tq == 0 and tk % rc == 0
    kern = functools.partial(_attn_kernel, tq=tq, tk=tk, rc=rc, seq=s, lam_init=lam_init)
    return pl.pallas_call(
        kern, grid=(b, DA_HEADS, s // tq),
        in_specs=[
            pl.BlockSpec((4, DA_HEAD_DIM), lambda bi, h, i: (0, 0)),
            pl.BlockSpec((1, DA_V_DIM), lambda bi, h, i: (0, 0)),
            pl.BlockSpec((1, tq, LANES), lambda bi, h, i: (bi, i, h)),
            pl.BlockSpec((1, s, LANES), lambda bi, h, i: (bi, 0, DA_HEADS + h)),
            pl.BlockSpec((1, s, LANES), lambda bi, h, i: (bi, 0, 2 * DA_HEADS + h)),
        ],
        out_specs=pl.BlockSpec((1, tq, LANES), lambda bi, h, i: (bi, i, h)),
        out_shape=jax.ShapeDtypeStruct((b, s, BRANCH_WIDTH), F32),
        scratch_shapes=[pltpu.VMEM((2 * LANES, s), BF16), pltpu.VMEM((s, 2 * LANES), BF16),
                        pltpu.VMEM((2, 2, tq, 2 * LANES), BF16), pltpu.VMEM((tk // rc, rc, tk), F32),
                        pltpu.VMEM((2, tq, 1), F32), pltpu.VMEM((2, tq, 2 * LANES), F32)],
        compiler_params=_params(3), name="diff_attention")(
            da_lambda, da_norm_g.reshape(1, DA_V_DIM), proj_a3, proj_a3, proj_a3)


def _s5_matrices(a_re, a_im, log_dt, b_re, b_im, c_re, c_im):
    L, C, P, G = S5_CHUNK, S5_GROUP_SIZE, S5_STATE, S5_GROUPS
    f = lambda t: t.astype(F32)
    a_re, a_im, b_re, b_im, c_re, c_im = map(f, (a_re, a_im, b_re, b_im, c_re, c_im))
    dt = jnp.exp(f(log_dt))[..., None]
    lr, li = a_re * dt, a_im * dt

    def apow(j):
        jj = j.astype(F32)[:, None, None, None]
        mag = jnp.exp(lr[None] * jj)
        return mag * jnp.cos(li[None] * jj), mag * jnp.sin(li[None] * jj)

    a1r, a1i = apow(jnp.arange(1, 2))
    nr, ni = a1r[0] - 1.0, a1i[0]
    den = a_re * a_re + a_im * a_im
    cr, ci = (nr * a_re + ni * a_im) / den, (ni * a_re - nr * a_im) / den
    bbr = cr[..., None] * b_re - ci[..., None] * b_im
    bbi = cr[..., None] * b_im + ci[..., None] * b_re

    pr, pi = apow(jnp.arange(L))
    wr = pr[..., None] * bbr[None] - pi[..., None] * bbi[None]
    wi = pr[..., None] * bbi[None] + pi[..., None] * bbr[None]
    kern = (jnp.einsum('dgcp,jdgpe->jdgce', c_re, wr, precision=HIGHEST)
            - jnp.einsum('dgcp,jdgpe->jdgce', c_im, wi, precision=HIGHEST))

    s_idx = jnp.arange(L)[:, None]
    t_idx = jnp.arange(L)[None, :]

    def toeplitz(kd, lag, valid):
        m = kd[jnp.clip(lag, 0, L - 1)] * valid[:, :, None, None, None].astype(F32)
        return m.transpose(2, 0, 4, 1, 3).reshape(G, L * C, L * C)

    m_grp = jnp.stack([toeplitz(kern[:, 0], t_idx - s_idx, t_idx >= s_idx),
                       toeplitz(kern[:, 1], s_idx - t_idx, s_idx >= t_idx)]).astype(BF16)

    def summary(w_dir):
        return w_dir.transpose(1, 0, 3, 2).reshape(G, L * C, P)

    p_re = jnp.stack([summary(wr[::-1, 0]), summary(wr[:, 1])])
    p_im = jnp.stack([summary(wi[::-1, 0]), summary(wi[:, 1])])

    qr, qi = apow(jnp.arange(1, L + 1))
    car = c_re[None] * qr[:, :, :, None, :] - c_im[None] * qi[:, :, :, None, :]
    cai = c_re[None] * qi[:, :, :, None, :] + c_im[None] * qr[:, :, :, None, :]

    def readout(ca_dir):
        return ca_dir.transpose(1, 3, 0, 2).reshape(G, P, L * C)

    q_re = jnp.stack([readout(car[:, 0]), readout(car[::-1, 1])])
    q_im = jnp.stack([-readout(cai[:, 0]), -readout(cai[::-1, 1])])

    half = jax.nn.one_hot(jnp.arange(G) % 2, 2, dtype=F32)
    p_pad = jnp.stack([p_re, p_im], axis=-2)[:, :, :, :, None, :] * half[None, :, None, None, :, None]
    p_grp = p_pad.reshape(2, G, L * C, 4 * P).astype(BF16)
    q_grp = jnp.stack([q_re, q_im], axis=2)[:, :, :, None] * half[None, :, None, :, None, None]
    q_grp = q_grp.reshape(2, G, 4 * P, L * C).astype(BF16)

    lc = L * C
    i_new = jnp.arange(2 * lc)
    i_old = (i_new // C % 2) * lc + (i_new // (2 * C)) * C + i_new % C
    perm = (i_old[:, None] == i_new[None, :]).astype(BF16)
    zero = jnp.zeros((2, G // 2, lc, lc), BF16)
    m4 = m_grp.reshape(2, G // 2, 2, lc, lc)
    m_bd = jnp.concatenate([jnp.concatenate([m4[:, :, 0], zero], axis=-1),
                            jnp.concatenate([zero, m4[:, :, 1]], axis=-1)], axis=-2)
    dot = functools.partial(jnp.einsum, preferred_element_type=BF16)
    m_mat = dot('ij,dpjk,lk->dpil', perm, m_bd, perm)
    p_mat = dot('ij,dpjk->dpik', perm, p_grp.reshape(2, G // 2, 2 * lc, 4 * P))
    q_mat = dot('dpkj,lj->dpkl', q_grp.reshape(2, G // 2, 2, 4 * P, lc).transpose(0, 1, 3, 2, 4)
                .reshape(2, G // 2, 4 * P, 2 * lc), perm)

    alr, ali = apow(jnp.arange(L, L + 1))
    a_l = jnp.concatenate([alr[0].reshape(2, G // 2, 2 * P), ali[0].reshape(2, G // 2, 2 * P)], axis=-1)
    return m_mat, p_mat, q_mat, a_l.reshape(2, G // 2, 1, 4 * P)


def _s5_kernel(x_ref, m_ref, p_ref, q_ref, al_ref, d_ref, y_ref, s_ref, xin_ref, *, n_chunks, bsz):
    hp = 2 * S5_STATE
    xb = x_ref[0].astype(BF16)
    for d in range(2):
        s = jnp.dot(xb, p_ref[d, 0], preferred_element_type=F32)
        s_ref[2 * d] = s[:, :hp]
        s_ref[2 * d + 1] = s[:, hp:]

    al = [al_ref[d, 0] for d in range(2)]
    alr = [a[:, :hp] for a in al]
    ali = [a[:, hp:] for a in al]

    def body(i, carry):
        new = []
        for d in range(2):
            re, im = carry[2 * d], carry[2 * d + 1]
            n = i if d == 0 else n_chunks - 1 - i
            rows = pl.ds(n, bsz, stride=n_chunks)
            xin_ref[2 * d, rows, :] = re
            xin_ref[2 * d + 1, rows, :] = im
            new.append(alr[d] * re - ali[d] * im + s_ref[2 * d, rows, :])
            new.append(alr[d] * im + ali[d] * re + s_ref[2 * d + 1, rows, :])
        return tuple(new)

    zero = jnp.zeros((bsz, hp), F32)
    lax.fori_loop(0, n_chunks, body, (zero, zero, zero, zero))

    xin = [jnp.concatenate([xin_ref[2 * d], xin_ref[2 * d + 1]], axis=-1).astype(BF16) for d in range(2)]
    y = x_ref[0] * d_ref[0]
    for d in range(2):
        y = y + jnp.dot(xb, m_ref[d, 0], preferred_element_type=F32)
        y = y + jnp.dot(xin[d], q_ref[d, 0], preferred_element_type=F32)
    y_ref[0] = y


S5_PAIR = 2 * S5_GROUP_SIZE
S5_PER_VREG = LANES // S5_PAIR


def _s5_proj_kernel(x_ref, w_ref, o_ref, tok_ref):
    c = S5_PAIR
    rows = o_ref.shape[1]
    res = jnp.dot(x_ref[...], w_ref[...], preferred_element_type=F32)
    for j in range(tok_ref.shape[0]):
        tok_ref[j] = res[:, j * LANES:(j + 1) * LANES]
    for j in range(tok_ref.shape[0]):
        for t in range(S5_CHUNK):
            blk = tok_ref[j, pl.ds(t, rows, stride=S5_CHUNK), :]
            for gg in range(S5_PER_VREG):
                o_ref[j * S5_PER_VREG + gg, :, t * c:(t + 1) * c] = blk[:, gg * c:(gg + 1) * c]


def _s5_unpack_kernel(y_ref, o_ref, tok_ref):
    c = S5_PAIR
    rows = y_ref.shape[1]
    for j in range(tok_ref.shape[0]):
        for t in range(S5_CHUNK):
            blk = jnp.concatenate([y_ref[j * S5_PER_VREG + gg, :, t * c:(t + 1) * c] for gg in range(S5_PER_VREG)],
                                  axis=-1)
            tok_ref[j, pl.ds(t, rows, stride=S5_CHUNK), :] = blk
    for j in range(tok_ref.shape[0]):
        o_ref[:, j * LANES:(j + 1) * LANES] = tok_ref[j]


def _s5_proj(xb, w_u, tm=256):
    t, d = xb.shape
    rows = t // S5_CHUNK
    lc = S5_CHUNK * S5_PAIR
    w = w_u.shape[1]
    return pl.pallas_call(
        _s5_proj_kernel, grid=(rows // tm,),
        in_specs=[pl.BlockSpec((tm * S5_CHUNK, d), lambda i: (i, 0)), pl.BlockSpec(w_u.shape, lambda i: (0, 0))],
        out_specs=pl.BlockSpec((S5_GROUPS // 2, tm, lc), lambda i: (0, i, 0)),
        out_shape=jax.ShapeDtypeStruct((S5_GROUPS // 2, rows, lc), F32),
        scratch_shapes=[pltpu.VMEM((w // LANES, tm * S5_CHUNK, LANES), F32)],
        compiler_params=_params(1), name="s5_proj")(xb, w_u)


def _s5(x, mats, d_skip, b, s):
    L, C, G = S5_CHUNK, S5_GROUP_SIZE, S5_GROUPS
    w = G * C
    n = s // L
    lc = L * S5_PAIR
    st = 4 * S5_STATE
    m_mat, p_pad, q_pad, a_l = mats
    d_row = jnp.tile(d_skip.astype(F32).reshape(G // 2, 1, S5_PAIR), (1, L, 1)).reshape(G // 2, 1, lc)
    spec = lambda r, c_: pl.BlockSpec((2, 1, r, c_), lambda k: (0, k, 0, 0))
    kern = functools.partial(_s5_kernel, n_chunks=n, bsz=b)
    y = pl.pallas_call(
        kern, grid=(G // 2,),
        in_specs=[pl.BlockSpec((1, n * b, lc), lambda k: (k, 0, 0)), spec(lc, lc), spec(lc, st), spec(st, lc),
                  spec(1, st), pl.BlockSpec((1, 1, lc), lambda k: (k, 0, 0))],
        out_specs=pl.BlockSpec((1, n * b, lc), lambda k: (k, 0, 0)),
        out_shape=jax.ShapeDtypeStruct((G // 2, n * b, lc), F32),
        scratch_shapes=[pltpu.VMEM((4, n * b, LANES), F32), pltpu.VMEM((4, n * b, LANES), F32)],
        compiler_params=_params(1), name="s5")(x, m_mat, p_pad, q_pad, a_l, d_row)
    tm = 256
    return pl.pallas_call(
        _s5_unpack_kernel, grid=(b * n // tm,),
        in_specs=[pl.BlockSpec((G // 2, tm, lc), lambda i: (0, i, 0))],
        out_specs=pl.BlockSpec((tm * L, w), lambda i: (i, 0)),
        out_shape=jax.ShapeDtypeStruct((b * s, w), F32),
        scratch_shapes=[pltpu.VMEM((w // LANES, tm * L, LANES), F32)],
        compiler_params=_params(1), name="s5_unpack")(y)


def _gla_kernel(qf_ref, kf_ref, vf_ref, gf_ref, qb_ref, kb_ref, vb_ref, gb_ref, bg_ref,
                of_ref, ob_ref, st_ref, *, rows):
    c = GLA_CHUNK
    n_sub = rows // c

    @pl.when(pl.program_id(1) == 0)
    def _():
        st_ref[...] = jnp.zeros_like(st_ref)

    nt = (((1,), (1,)), ((), ()))
    tn = (((0,), (0,)), ((), ()))
    log2 = lambda n: int(math.log2(n))
    r_r = lax.broadcasted_iota(jnp.int32, (rows, rows), 0)
    c_r = lax.broadcasted_iota(jnp.int32, (rows, rows), 1)
    same_chunk = (r_r >> log2(c)) == (c_r >> log2(c))
    t_row = lax.broadcasted_iota(jnp.int32, (c, GLA_K), 0)
    t_col = lax.broadcasted_iota(jnp.int32, (c, GLA_K), 1) & (c - 1)
    k_lane = lax.broadcasted_iota(jnp.int32, (1, GLA_K), 1) >> log2(GLA_DK)
    v_lane = lax.broadcasted_iota(jnp.int32, (1, BRANCH_WIDTH), 1) >> log2(GLA_DV)
    st_same_head = ((lax.broadcasted_iota(jnp.int32, (BRANCH_WIDTH, GLA_K), 0) >> log2(GLA_DV))
                    == (lax.broadcasted_iota(jnp.int32, (BRANCH_WIDTH, GLA_K), 1) >> log2(GLA_DK)))

    for d, (q_ref, k_ref, v_ref, g_ref, o_ref) in enumerate(
            ((qf_ref, kf_ref, vf_ref, gf_ref, of_ref), (qb_ref, kb_ref, vb_ref, gb_ref, ob_ref))):
        gate = g_ref[0].astype(F32) + bg_ref[d]
        log_a = (jnp.minimum(gate, 0.0) - jnp.log(1.0 + jnp.exp(-jnp.abs(gate)))) * (1.0 / GLA_TAU)
        if d == 0:
            tri = jnp.where(same_chunk, jnp.where(r_r >= c_r, 1.0, 0.0), 0.0).astype(BF16)
            keep, i_ref, i_last = (t_row >= t_col), c // 2, c - 1
        else:
            tri = jnp.where(same_chunk, jnp.where(c_r >= r_r, 1.0, 0.0), 0.0).astype(BF16)
            keep, i_ref, i_last = (t_col > t_row), c - 1 - c // 2, 0
        la_hi = log_a.astype(BF16)
        la_lo = (log_a - la_hi.astype(F32)).astype(BF16)
        cum_all = (jnp.dot(tri, la_hi, preferred_element_type=F32) + jnp.dot(tri, la_lo, preferred_element_type=F32))
        q = q_ref[0].astype(F32) * (GLA_DK ** -0.5)
        k = k_ref[0].astype(F32)
        v = v_ref[0]
        for ci in (range(n_sub) if d == 0 else range(n_sub - 1, -1, -1)):
            sl = slice(ci * c, (ci + 1) * c)
            cum = cum_all[sl]
            mid = cum[i_ref:i_ref + 1]
            last = cum[i_last:i_last + 1]
            qc, kc, vc = q[sl], k[sl], v[sl]
            qe = (qc * jnp.exp(cum - mid)).astype(BF16)
            ke = kc * jnp.exp(mid - cum)
            kd = (kc * jnp.exp(last - cum)).astype(BF16)
            qs = (qc * jnp.exp(cum)).astype(BF16)
            decay = jnp.exp(last)
            ke4 = jnp.concatenate([jnp.where(k_lane == h, ke, 0.0) for h in range(GLA_HEADS)], axis=0).astype(BF16)
            v4 = jnp.concatenate([jnp.where(v_lane == h, vc, 0.0) for h in range(GLA_HEADS)], axis=0).astype(BF16)
            sc = lax.dot_general(qe, ke4, nt, preferred_element_type=F32)
            sc = jnp.where(keep, sc, 0.0).astype(BF16)
            st = st_ref[d]
            o = (jnp.dot(sc, v4, preferred_element_type=F32)
                 + lax.dot_general(qs, st.astype(BF16), nt, preferred_element_type=F32))
            o_ref[0, sl, :] = o
            ds = lax.dot_general(vc.astype(BF16), kd, tn, preferred_element_type=F32)
            for h in range(GLA_HEADS):
                rs = slice(h * GLA_DV, (h + 1) * GLA_DV)
                ls = slice(h * GLA_DK // LANES * LANES, h * GLA_DK // LANES * LANES + LANES)
                st_ref[d, rs, ls] = decay[:, ls] * st[rs, ls] + jnp.where(st_same_head[rs, ls], ds[rs, ls], 0.0)


def _gla(proj_a3, proj_b3, b_gate, rows=128):
    b, s, _ = proj_a3.shape
    n = s // rows
    bg = b_gate.astype(F32).reshape(2, 1, GLA_K)

    def specs(rev):
        blk = (lambda i: n - 1 - i) if rev else (lambda i: i)
        return [pl.BlockSpec((1, rows, GLA_K), lambda bi, i: (bi, blk(i), A_QC // GLA_K)),
                pl.BlockSpec((1, rows, GLA_K), lambda bi, i: (bi, blk(i), A_KC // GLA_K)),
                pl.BlockSpec((1, rows, BRANCH_WIDTH), lambda bi, i: (bi, blk(i), A_VC // BRANCH_WIDTH)),
                pl.BlockSpec((1, rows, GLA_K), lambda bi, i: (bi, blk(i), GATE_COLS // GLA_K + int(rev)))]

    out_f = pl.BlockSpec((1, rows, BRANCH_WIDTH), lambda bi, i: (bi, i, 0))
    out_b = pl.BlockSpec((1, rows, BRANCH_WIDTH), lambda bi, i: (bi, n - 1 - i, 0))
    shape = jax.ShapeDtypeStruct((b, s, BRANCH_WIDTH), F32)
    return pl.pallas_call(
        functools.partial(_gla_kernel, rows=rows), grid=(b, n),
        in_specs=specs(False) + specs(True) + [pl.BlockSpec((2, 1, GLA_K), lambda bi, i: (0, 0, 0))],
        out_specs=[out_f, out_b], out_shape=[shape, shape],
        scratch_shapes=[pltpu.VMEM((2, BRANCH_WIDTH, GLA_K), F32)],
        compiler_params=_params(2), name="gla")(
            proj_a3, proj_a3, proj_a3, proj_b3, proj_a3, proj_a3, proj_a3, proj_b3, bg)


def _ranks(vals):
    ranks = []
    for i, v in enumerate(vals):
        r = jnp.zeros(v.shape, F32)
        for o, other in enumerate(vals):
            if o != i:
                r = r + (jnp.where(other >= v, 1.0, 0.0) if o < i else jnp.where(other > v, 1.0, 0.0))
        ranks.append(r)
    return ranks


def _route(scores_t, bias_col):
    gs = EXPERTS_PER_GROUP
    s = [scores_t[e:e + 1] for e in range(N_EXPERTS)]
    biased = [s[e] + bias_col[e:e + 1] for e in range(N_EXPERTS)]
    top2, group_score = [], []
    for g in range(N_EXPERTS // gs):
        rk = _ranks(biased[g * gs:(g + 1) * gs])
        top2 += [r < 2.0 for r in rk]
        kept = [jnp.where(top2[g * gs + j], biased[g * gs + j], 0.0) for j in range(gs)]
        group_score.append(functools.reduce(lambda a, b: a + b, kept))
    best = [r < 1.0 for r in _ranks(group_score)]
    w = [jnp.where(top2[e], jnp.where(best[e // gs], s[e], 0.0), 0.0) for e in range(N_EXPERTS)]
    total = functools.reduce(lambda a, b: a + b, w)
    return [we / total for we in w]


def _merge_kernel(x_ref, oa_ref, ys_ref, gf_ref, gb_ref, r_ref, gl0_ref, gl1_ref, gl2_ref,
                  wglu_ref, bglu_ref, gng_ref, wup_ref, mb_ref, wout_ref, lng_ref, lnb_ref, rw_ref, rb_ref,
                  xo_ref, xob_ref, comb_ref):
    y = ys_ref[...]
    gelu = 0.5 * y * (1.0 + jnp.tanh(math.sqrt(2.0 / math.pi) * (y + 0.044715 * (y * y * y))))
    o_b = gelu * _sigmoid(jnp.dot(gelu.astype(BF16), wglu_ref[...], preferred_element_type=F32) + bglu_ref[...])

    r = r_ref[...].astype(F32)
    gated = r * _sigmoid(r)
    heads = []
    for h in range(GLA_HEADS):
        sl = slice(h * GLA_DV, (h + 1) * GLA_DV)
        o = gf_ref[:, sl] + gb_ref[:, sl]
        ms = jnp.mean(o * o, axis=-1, keepdims=True)
        heads.append(o * lax.rsqrt(ms + NORM_EPS) * gng_ref[...] * gated[:, sl])
    o_c = jnp.concatenate(heads, axis=-1)

    merged = None
    for n, (o_n, gl_ref) in enumerate(((oa_ref[...], gl0_ref), (o_b, gl1_ref), (o_c, gl2_ref))):
        gate = _sigmoid(gl_ref[...].astype(F32) + mb_ref[n])
        term = gate * jnp.dot(o_n.astype(BF16), wup_ref[n], preferred_element_type=F32)
        merged = term if merged is None else merged + term
    mix = jnp.dot(merged.astype(BF16), wout_ref[...], preferred_element_type=F32)
    x1 = _layer_norm(DN_ALPHA * x_ref[...] + mix, lng_ref[...], lnb_ref[...])
    xo_ref[...] = x1
    x_hi = x1.astype(BF16)
    xob_ref[...] = x_hi

    x_lo = (x1 - x_hi.astype(F32)).astype(BF16)
    logits = (jnp.dot(x_hi, rw_ref[0], preferred_element_type=F32) + jnp.dot(x_lo, rw_ref[0], preferred_element_type=F32)
              + jnp.dot(x_hi, rw_ref[1], preferred_element_type=F32))
    rows = _route(_sigmoid(logits.T[:N_EXPERTS]), rb_ref[...])
    comb_ref[...] = jnp.zeros(comb_ref.shape, F32)
    for e in range(N_EXPERTS):
        comb_ref[e:e + 1, :] = rows[e]


def _merge(x, o_a, y_s5, g_f, g_b, proj_a, proj_b, wglu, bglu, gng, wup, mb, wout, lng, lnb, rw, rb, tm=256):
    t, d = x.shape
    w = BRANCH_WIDTH
    row = lambda width, cb=0: pl.BlockSpec((tm, width), lambda i: (i, cb))
    full = lambda a: pl.BlockSpec(a.shape, lambda i: (0,) * a.ndim)
    weights = (wglu, bglu, gng, wup, mb, wout, lng, lnb, rw, rb)
    return pl.pallas_call(
        _merge_kernel, grid=(t // tm,),
        in_specs=[row(d), row(w), row(w), row(w), row(w), row(w, A_RC // w), row(d, 0), row(d, 1), row(d, 2)]
        + [full(a) for a in weights],
        out_specs=[row(d), row(d), pl.BlockSpec((LANES, tm), lambda i: (0, i))],
        out_shape=[jax.ShapeDtypeStruct((t, d), F32), jax.ShapeDtypeStruct((t, d), BF16),
                   jax.ShapeDtypeStruct((LANES, t), F32)],
        compiler_params=_params(1), name="merge")(
            x, o_a, y_s5, g_f, g_b, proj_a, proj_b, proj_b, proj_b, *weights)


MOE_SUB = 256
MOE_CAP = 64


def _swiglu(x, wg_ref, wu_ref, wd_ref, row_scale):
    g = jnp.dot(x, wg_ref[0], preferred_element_type=F32)
    u = jnp.dot(x, wu_ref[0], preferred_element_type=F32)
    h = (g * _sigmoid(g)) * u * row_scale
    return jnp.dot(h.astype(BF16), wd_ref[0], preferred_element_type=F32)


def _moe_kernel(xb_ref, xf_ref, comb_ref, wg_ref, wu_ref, wd_ref, lng_ref, lnb_ref, of_ref, ob_ref,
                pos_ref, ind_ref, sel_ref, xg_ref, y_ref, cg_ref, acc_ref, flag_ref, *, tm, cap):
    e = pl.program_id(1)
    ns = tm // MOE_SUB
    gs = EXPERTS_PER_GROUP
    sub = lambda k: slice(k * MOE_SUB, (k + 1) * MOE_SUB)
    slots = lambda k: slice(k * cap, (k + 1) * cap)
    r_i = lax.broadcasted_iota(jnp.int32, (MOE_SUB, MOE_SUB), 0)
    c_i = lax.broadcasted_iota(jnp.int32, (MOE_SUB, MOE_SUB), 1)

    @pl.when(e == 0)
    def _():
        ind = jnp.where(comb_ref[...] > 0.0, 1.0, 0.0)
        ind_ref[...] = ind
        before = jnp.where(r_i < c_i, 1.0, 0.0).astype(BF16)
        most = jnp.zeros((LANES, 1), F32)
        for k in range(ns):
            pos_ref[:, sub(k)] = jnp.dot(ind[:, sub(k)].astype(BF16), before, preferred_element_type=F32)
            most = jnp.maximum(most, jnp.sum(ind[:, sub(k)], axis=-1, keepdims=True))
        flag_ref[0] = (jnp.max(most) > cap).astype(jnp.int32)
        acc_ref[...] = jnp.zeros_like(acc_ref)

    slot_id = lax.broadcasted_iota(jnp.int32, (cap, 1), 0).astype(F32)
    for grp in range(N_EXPERTS // gs):
        @pl.when(e == grp * gs)
        def _(grp=grp):
            for k in range(ns):
                pieces = []
                for j in range(gs):
                    ee = grp * gs + j
                    one_hot = jnp.where(pos_ref[ee:ee + 1, sub(k)] == slot_id, ind_ref[ee:ee + 1, sub(k)], 0.0)
                    cg_ref[j, slots(k), :] = jnp.sum(one_hot * comb_ref[ee:ee + 1, sub(k)], axis=-1, keepdims=True)
                    pieces.append(one_hot)
                sel = jnp.concatenate(pieces, axis=0).astype(BF16)
                sel_ref[k] = sel
                xg = jnp.dot(sel, xb_ref[sub(k), :], preferred_element_type=F32).astype(BF16)
                for j in range(gs):
                    xg_ref[j, slots(k), :] = xg[j * cap:(j + 1) * cap]

    j = e % gs
    y_ref[j] = _swiglu(xg_ref[j], wg_ref, wu_ref, wd_ref, cg_ref[j]).astype(BF16)

    @pl.when(j == gs - 1)
    def _():
        for k in range(ns):
            yk = jnp.concatenate([y_ref[jj, slots(k), :] for jj in range(gs)], axis=0)
            acc_ref[sub(k), :] += lax.dot_general(sel_ref[k], yk, (((0,), (0,)), ((), ())),
                                                  preferred_element_type=F32)

    @pl.when(flag_ref[0] > 0)
    def _():
        comb = comb_ref[...].T
        earlier = jnp.where(r_i > c_i, 1.0, 0.0).astype(BF16)
        over = []
        for k in range(ns):
            pos = jnp.dot(earlier, jnp.where(comb[sub(k)] > 0.0, 1.0, 0.0).astype(BF16), preferred_element_type=F32)
            over.append(jnp.where(pos >= cap, comb[sub(k)], 0.0))
        over = jnp.concatenate(over, axis=0)
        lane = lax.broadcasted_iota(jnp.int32, over.shape, 1)
        ce = jnp.sum(jnp.where(lane == e, over, 0.0), axis=-1, keepdims=True)
        acc_ref[...] += _swiglu(xb_ref[...], wg_ref, wu_ref, wd_ref, ce)

    @pl.when(e == N_EXPERTS - 1)
    def _():
        y = _layer_norm(DN_ALPHA * xf_ref[...] + acc_ref[...], lng_ref[...], lnb_ref[...])
        of_ref[...] = y
        ob_ref[...] = y.astype(BF16)


def _moe(xb, xf, comb, wg, wu, wd, lng, lnb, tm=1024, cap=MOE_CAP):
    t, d = xf.shape
    ns = tm // MOE_SUB
    gs = EXPERTS_PER_GROUP
    row = lambda width: pl.BlockSpec((tm, width), lambda i, e: (i, 0))
    vec = pl.BlockSpec((1, d), lambda i, e: (0, 0))
    return pl.pallas_call(
        functools.partial(_moe_kernel, tm=tm, cap=cap), grid=(t // tm, N_EXPERTS),
        in_specs=[row(d), row(d), pl.BlockSpec((LANES, tm), lambda i, e: (0, i)),
                  pl.BlockSpec((1, d, EXPERT_FF), lambda i, e: (e, 0, 0)),
                  pl.BlockSpec((1, d, EXPERT_FF), lambda i, e: (e, 0, 0)),
                  pl.BlockSpec((1, EXPERT_FF, d), lambda i, e: (e, 0, 0)), vec, vec],
        out_specs=[row(d), row(d)],
        out_shape=[jax.ShapeDtypeStruct((t, d), F32), jax.ShapeDtypeStruct((t, d), BF16)],
        scratch_shapes=[pltpu.VMEM((LANES, tm), F32), pltpu.VMEM((LANES, tm), F32),
                        pltpu.VMEM((ns, gs * cap, MOE_SUB), BF16), pltpu.VMEM((gs, ns * cap, d), BF16),
                        pltpu.VMEM((gs, ns * cap, d), BF16), pltpu.VMEM((gs, ns * cap, 1), F32),
                        pltpu.VMEM((tm, d), F32), pltpu.SMEM((1,), jnp.int32)],
        compiler_params=_params(2), name="moe")(xb, xf, comb, wg, wu, wd, lng, lnb)


def _proj_b_weights(w_in_l, w_gate):
    z0 = W_IN_Z
    folded = [jnp.dot(w_in_l[:, z0 + i * GLA_GATE_RANK:z0 + (i + 1) * GLA_GATE_RANK].astype(F32),
                      w_gate[i].astype(F32), precision=HIGHEST) for i in range(2)]
    return jnp.concatenate([w_in_l[:, z0 + 2 * GLA_GATE_RANK:].astype(F32)] + folded, axis=1).astype(BF16)


def kernel(x, ln0_g, ln0_b, w_in, da_lambda, da_norm_g, s5_a_re, s5_a_im, s5_log_dt, s5_b_re, s5_b_im,
           s5_c_re, s5_c_im, s5_d, s5_w_glu, s5_b_glu, gla_w_gate, gla_b_gate, gla_norm_g, merge_w_up,
           merge_b, w_out, ln1_g, ln1_b, router_w, router_bias, moe_w_gate, moe_w_up, moe_w_down,
           ln2_g, ln2_b):
    b, s, d = x.shape
    t = b * s
    vec = lambda a: a.astype(F32).reshape(1, -1)
    xf, xb = _ln0(x.reshape(t, d), ln0_g.astype(F32), ln0_b.astype(F32))
    rw32 = jnp.zeros((d, LANES), F32).at[:, :N_EXPERTS].set(router_w.astype(F32))
    rw_hi = rw32.astype(BF16)
    rw = jnp.stack([rw_hi, (rw32 - rw_hi.astype(F32)).astype(BF16)])
    rb = router_bias.astype(F32).reshape(N_EXPERTS, 1)
    for l in range(DEPTH):
        w_a = jnp.concatenate([w_in[l][:, :W_IN_U], w_in[l][:, W_IN_GLA:W_IN_Z]], axis=1).astype(BF16)
        w_u = w_in[l][:, W_IN_U:W_IN_GLA].astype(BF16)
        w_b = _proj_b_weights(w_in[l], gla_w_gate[l])
        proj_a = _matmul(xb, w_a, 1024, PROJ_A // 2, "in_proj_a")
        proj_b = _matmul(xb, w_b, 512, PROJ_B, "in_proj_b")
        proj_a3 = proj_a.reshape(b, s, PROJ_A)
        proj_b3 = proj_b.reshape(b, s, PROJ_B)

        lam_init = 0.8 - 0.6 * math.exp(-0.3 * l)
        o_a = _diff_attention(proj_a3, da_lambda[l].astype(F32), da_norm_g[l].astype(F32), lam_init)
        mats = _s5_matrices(s5_a_re[l], s5_a_im[l], s5_log_dt[l], s5_b_re[l], s5_b_im[l], s5_c_re[l], s5_c_im[l])
        y_s5 = _s5(_s5_proj(xb, w_u), mats, s5_d[l], b, s)
        g_f, g_b = _gla(proj_a3, proj_b3, gla_b_gate[l])

        xf, xb, comb = _merge(
            xf, o_a.reshape(t, BRANCH_WIDTH), y_s5, g_f.reshape(t, BRANCH_WIDTH), g_b.reshape(t, BRANCH_WIDTH),
            proj_a, proj_b, s5_w_glu[l].astype(BF16), vec(s5_b_glu[l]), vec(gla_norm_g[l]),
            merge_w_up[l].astype(BF16), merge_b[l].astype(F32).reshape(N_BRANCH, 1, d), w_out[l].astype(BF16),
            vec(ln1_g[l]), vec(ln1_b[l]), rw, rb)
        xf, xb = _moe(xb, xf, comb, moe_w_gate[l].astype(BF16), moe_w_up[l].astype(BF16),
                      moe_w_down[l].astype(BF16), vec(ln2_g[l]), vec(ln2_b[l]))
    return xf.reshape(b, s, d)
```

```python
import functools
import math

import jax
import jax.numpy as jnp
from jax import lax
from jax.experimental import pallas as pl
from jax.experimental.pallas import tpu as pltpu

F32 = jnp.float32
BF16 = jnp.bfloat16
HIGHEST = lax.Precision.HIGHEST

D_MODEL = 1024
DEPTH = 2
BRANCH_WIDTH = D_MODEL // 2
N_BRANCH = 3

DA_HEADS = 4
DA_HEAD_DIM = BRANCH_WIDTH // (2 * DA_HEADS)
DA_V_DIM = 2 * DA_HEAD_DIM
ALIBI_MAX_EXP = 8.0

S5_GROUP_SIZE = 16
S5_GROUPS = BRANCH_WIDTH // S5_GROUP_SIZE
S5_STATE = 64
S5_CHUNK = 16

GLA_HEADS = 4
GLA_DV = BRANCH_WIDTH // GLA_HEADS
GLA_DK = GLA_DV // 2
GLA_K = GLA_HEADS * GLA_DK
GLA_GATE_RANK = 16
GLA_TAU = 16.0
GLA_CHUNK = 64

N_EXPERTS = 16
EXPERTS_PER_GROUP = 4
EXPERT_FF = D_MODEL // 2

DN_ALPHA = (2.0 * DEPTH) ** 0.25
NORM_EPS = 1e-5
LOG2E = math.log2(math.e)

LANES = 128
W_IN_U = 3 * BRANCH_WIDTH
W_IN_GLA = W_IN_U + BRANCH_WIDTH
W_IN_Z = W_IN_GLA + 2 * GLA_K + 2 * BRANCH_WIDTH
A_QC = W_IN_U
A_KC = A_QC + GLA_K
A_VC = A_KC + GLA_K
A_RC = A_VC + BRANCH_WIDTH
PROJ_A = A_RC + BRANCH_WIDTH
GATE_COLS = N_BRANCH * D_MODEL
PROJ_B = GATE_COLS + 2 * GLA_K
VMEM_LIMIT = 56 * 1024 * 1024


def _params(n_grid_dims):
    return pltpu.CompilerParams(dimension_semantics=("arbitrary",) * n_grid_dims,
                                vmem_limit_bytes=VMEM_LIMIT)


def _layer_norm(xf, g, b):
    mu = jnp.mean(xf, axis=-1, keepdims=True)
    xc = xf - mu
    var = jnp.mean(xc * xc, axis=-1, keepdims=True)
    return xc * lax.rsqrt(var + NORM_EPS) * g + b


def _sigmoid(x):
    return 1.0 / (1.0 + jnp.exp(-x))


def _ln0_kernel(x_ref, g_ref, b_ref, of_ref, ob_ref):
    y = _layer_norm(x_ref[...], g_ref[...], b_ref[...])
    of_ref[...] = y
    ob_ref[...] = y.astype(BF16)


def _ln0(x2, g, b, tm=512):
    t, d = x2.shape
    row = pl.BlockSpec((tm, d), lambda i: (i, 0))
    vec = pl.BlockSpec((1, d), lambda i: (0, 0))
    return pl.pallas_call(
        _ln0_kernel, grid=(t // tm,), in_specs=[row, vec, vec], out_specs=[row, row],
        out_shape=[jax.ShapeDtypeStruct((t, d), F32), jax.ShapeDtypeStruct((t, d), BF16)],
        compiler_params=_params(1), name="ln0")(x2, g.reshape(1, d), b.reshape(1, d))


def _mm_kernel(x_ref, w_ref, o_ref):
    o_ref[...] = jnp.dot(x_ref[...], w_ref[...], preferred_element_type=F32).astype(o_ref.dtype)


def _matmul(x, w, tm, tn, name):
    t, k = x.shape
    n = w.shape[1]
    return pl.pallas_call(
        _mm_kernel, grid=(t // tm, n // tn),
        in_specs=[pl.BlockSpec((tm, k), lambda i, j: (i, 0)), pl.BlockSpec((k, tn), lambda i, j: (0, j))],
        out_specs=pl.BlockSpec((tm, tn), lambda i, j: (i, j)),
        out_shape=jax.ShapeDtypeStruct((t, n), BF16),
        compiler_params=_params(2), name=name)(x, w)


def _split3(x):
    hi = x.astype(BF16).astype(F32)
    mid = (x - hi).astype(BF16).astype(F32)
    lo = (x - hi - mid).astype(BF16).astype(F32)
    return hi, mid, lo


def _attn_kernel(lam_ref, g_ref, q_ref, k_ref, v_ref, o_ref, ka_ref, vb_ref, lhs_ref, fix_ref, m_ref, acc_ref,
                 *, tq, tk, rc, seq, lam_init):
    h = pl.program_id(1)
    qi = pl.program_id(2)
    dh = DA_HEAD_DIM

    slopes = [2.0 ** (-ALIBI_MAX_EXP * (i + 1) / DA_HEADS) for i in range(DA_HEADS)]
    slope = jnp.float32(slopes[-1])
    for i in range(DA_HEADS - 1):
        slope = jnp.where(h == i, jnp.float32(slopes[i]), slope)
    sigma = slope * LOG2E

    @pl.when(qi == 0)
    def _():
        ka_ref[:LANES, :] = k_ref[0].astype(F32).T.astype(BF16)
        t_hi, t_mid, t_lo = _split3(lax.broadcasted_iota(jnp.int32, (1, seq), 1).astype(F32) * sigma)
        sub_k = lax.broadcasted_iota(jnp.int32, (LANES, seq), 0)
        aug_k = jnp.where(sub_k < 3, 1.0, jnp.where(sub_k == 3, t_hi, jnp.where(sub_k == 4, t_mid,
                          jnp.where(sub_k == 5, t_lo, 0.0))))
        ka_ref[LANES:, :] = aug_k.astype(BF16)
        lane_v = lax.broadcasted_iota(jnp.int32, (seq, LANES), 1)
        vb_ref[:, :LANES] = v_ref[0].astype(BF16)
        vb_ref[:, LANES:] = jnp.where(lane_v == 0, 1.0, 0.0).astype(BF16)
        for r in range(tk // rc):
            delta = (r * rc + lax.broadcasted_iota(jnp.int32, (rc, 1), 0)
                     - lax.broadcasted_iota(jnp.int32, (1, tk), 1)).astype(F32)
            fix_ref[r] = (2.0 * sigma) * jnp.minimum(delta, 0.0)

    q = q_ref[0].astype(F32) * (dh ** -0.5 * LOG2E)
    lane = lax.broadcasted_iota(jnp.int32, (tq, LANES), 1)
    row = lax.broadcasted_iota(jnp.int32, (tq, 1), 0)
    u_hi, u_mid, u_lo = _split3((qi * tq + row).astype(F32) * (-sigma))
    aug_q = jnp.where(lane == 0, u_hi, jnp.where(lane == 1, u_mid, jnp.where(lane == 2, u_lo,
                      jnp.where(lane < 6, 1.0, 0.0))))
    for mp in range(2):
        qz = jnp.where(lane < dh, q, 0.0) if mp == 0 else jnp.where(lane >= dh, q, 0.0)
        lhs_ref[mp, 0] = jnp.concatenate([qz, aug_q], axis=-1).astype(BF16)
        lhs_ref[mp, 1] = jnp.concatenate([qz, -aug_q], axis=-1).astype(BF16)
    m_ref[...] = jnp.full(m_ref.shape, -1e30, F32)
    acc_ref[...] = jnp.zeros(acc_ref.shape, F32)

    def block(j, local):
        off = pl.multiple_of(j * tk, tk)
        ka = ka_ref[:, pl.ds(off, tk)]
        vb = vb_ref[pl.ds(off, tk), :]
        for mp in range(2):
            for c in range(tq // rc):
                rows = slice(c * rc, (c + 1) * rc)
                if isinstance(local, str):
                    sign, fixup = (0 if local == "before" else 1), None
                elif (c + 1) * rc <= local * tk:
                    sign, fixup = 1, None
                elif c * rc >= (local + 1) * tk:
                    sign, fixup = 0, None
                else:
                    sign, fixup = 0, fix_ref[(c * rc - local * tk) // rc]
                s = jnp.dot(lhs_ref[mp, sign, rows, :], ka, preferred_element_type=F32)
                if fixup is not None:
                    s = s + fixup
                m_old = m_ref[mp, rows, :]
                m_new = jnp.maximum(m_old, jnp.max(s, axis=-1, keepdims=True))
                alpha = jnp.exp2(m_old - m_new)
                p = jnp.exp2(s - m_new).astype(BF16)
                acc_ref[mp, rows, :] = alpha * acc_ref[mp, rows, :] + jnp.dot(p, vb, preferred_element_type=F32)
                m_ref[mp, rows, :] = m_new

    def run(lo, hi, where):
        def body(j, carry):
            block(j, where)
            return carry
        lax.fori_loop(lo, hi, body, 0)

    j0 = (qi * tq) // tk
    run(0, j0, "before")
    for local in range(tq // tk):
        block(j0 + local, local)
    run(j0 + tq // tk, seq // tk, "after")

    lv = lam_ref[...]
    lam = (jnp.exp(jnp.sum(lv[0:1] * lv[1:2], axis=-1, keepdims=True))
           - jnp.exp(jnp.sum(lv[2:3] * lv[3:4], axis=-1, keepdims=True)) + lam_init)
    o1 = acc_ref[0, :, :LANES] / acc_ref[0, :, LANES:LANES + 1]
    o2 = acc_ref[1, :, :LANES] / acc_ref[1, :, LANES:LANES + 1]
    o = o1 - lam * o2
    ms = jnp.mean(o * o, axis=-1, keepdims=True)
    o_ref[0] = o * lax.rsqrt(ms + NORM_EPS) * g_ref[...] * (1.0 - lam_init)


def _diff_attention(proj_a3, da_lambda, da_norm_g, lam_init, tq=2048, tk=1024, rc=256):
    b, s, _ = proj_a3.shape
    assert tq % tk == 0 and s % tq == 0 and tk % rc == 0
    kern = functools.partial(_attn_kernel, tq=tq, tk=tk, rc=rc, seq=s, lam_init=lam_init)
    return pl.pallas_call(
        kern, grid=(b, DA_HEADS, s // tq),
        in_specs=[
            pl.BlockSpec((4, DA_HEAD_DIM), lambda bi, h, i: (0, 0)),
            pl.BlockSpec((1, DA_V_DIM), lambda bi, h, i: (0, 0)),
            pl.BlockSpec((1, tq, LANES), lambda bi, h, i: (bi, i, h)),
            pl.BlockSpec((1, s, LANES), lambda bi, h, i: (bi, 0, DA_HEADS + h)),
            pl.BlockSpec((1, s, LANES), lambda bi, h, i: (bi, 0, 2 * DA_HEADS + h)),
        ],
        out_specs=pl.BlockSpec((1, tq, LANES), lambda bi, h, i: (bi, i, h)),
        out_shape=jax.ShapeDtypeStruct((b, s, BRANCH_WIDTH), F32),
        scratch_shapes=[pltpu.VMEM((2 * LANES, s), BF16), pltpu.VMEM((s, 2 * LANES), BF16),
                        pltpu.VMEM((2, 2, tq, 2 * LANES), BF16), pltpu.VMEM((tk // rc, rc, tk), F32),
                        pltpu.VMEM((2, tq, 1), F32), pltpu.VMEM((2, tq, 2 * LANES), F32)],
        compiler_params=_params(3), name="diff_attention")(
            da_lambda, da_norm_g.reshape(1, DA_V_DIM), proj_a3, proj_a3, proj_a3)


def _s5_matrices(a_re, a_im, log_dt, b_re, b_im, c_re, c_im):
    L, C, P, G = S5_CHUNK, S5_GROUP_SIZE, S5_STATE, S5_GROUPS
    f = lambda t: t.astype(F32)
    a_re, a_im, b_re, b_im, c_re, c_im = map(f, (a_re, a_im, b_re, b_im, c_re, c_im))
    dt = jnp.exp(f(log_dt))[..., None]
    lr, li = a_re * dt, a_im * dt

    def apow(j):
        jj = j.astype(F32)[:, None, None, None]
        mag = jnp.exp(lr[None] * jj)
        return mag * jnp.cos(li[None] * jj), mag * jnp.sin(li[None] * jj)

    a1r, a1i = apow(jnp.arange(1, 2))
    nr, ni = a1r[0] - 1.0, a1i[0]
    den = a_re * a_re + a_im * a_im
    cr, ci = (nr * a_re + ni * a_im) / den, (ni * a_re - nr * a_im) / den
    bbr = cr[..., None] * b_re - ci[..., None] * b_im
    bbi = cr[..., None] * b_im + ci[..., None] * b_re

    pr, pi = apow(jnp.arange(L))
    wr = pr[..., None] * bbr[None] - pi[..., None] * bbi[None]
    wi = pr[..., None] * bbi[None] + pi[..., None] * bbr[None]
    kern = (jnp.einsum('dgcp,jdgpe->jdgce', c_re, wr, precision=HIGHEST)
            - jnp.einsum('dgcp,jdgpe->jdgce', c_im, wi, precision=HIGHEST))

    s_idx = jnp.arange(L)[:, None]
    t_idx = jnp.arange(L)[None, :]

    def toeplitz(kd, lag, valid):
        m = kd[jnp.clip(lag, 0, L - 1)] * valid[:, :, None, None, None].astype(F32)
        return m.transpose(2, 0, 4, 1, 3).reshape(G, L * C, L * C)

    m_grp = jnp.stack([toeplitz(kern[:, 0], t_idx - s_idx, t_idx >= s_idx),
                       toeplitz(kern[:, 1], s_idx - t_idx, s_idx >= t_idx)]).astype(BF16)

    def summary(w_dir):
        return w_dir.transpose(1, 0, 3, 2).reshape(G, L * C, P)

    p_re = jnp.stack([summary(wr[::-1, 0]), summary(wr[:, 1])])
    p_im = jnp.stack([summary(wi[::-1, 0]), summary(wi[:, 1])])

    qr, qi = apow(jnp.arange(1, L + 1))
    car = c_re[None] * qr[:, :, :, None, :] - c_im[None] * qi[:, :, :, None, :]
    cai = c_re[None] * qi[:, :, :, None, :] + c_im[None] * qr[:, :, :, None, :]

    def readout(ca_dir):
        return ca_dir.transpose(1, 3, 0, 2).reshape(G, P, L * C)

    q_re = jnp.stack([readout(car[:, 0]), readout(car[::-1, 1])])
    q_im = jnp.stack([-readout(cai[:, 0]), -readout(cai[::-1, 1])])

    half = jax.nn.one_hot(jnp.arange(G) % 2, 2, dtype=F32)
    p_pad = jnp.stack([p_re, p_im], axis=-2)[:, :, :, :, None, :] * half[None, :, None, None, :, None]
    p_grp = p_pad.reshape(2, G, L * C, 4 * P).astype(BF16)
    q_grp = jnp.stack([q_re, q_im], axis=2)[:, :, :, None] * half[None, :, None, :, None, None]
    q_grp = q_grp.reshape(2, G, 4 * P, L * C).astype(BF16)

    lc = L * C
    i_new = jnp.arange(2 * lc)
    i_old = (i_new // C % 2) * lc + (i_new // (2 * C)) * C + i_new % C
    perm = (i_old[:, None] == i_new[None, :]).astype(BF16)
    zero = jnp.zeros((2, G // 2, lc, lc), BF16)
    m4 = m_grp.reshape(2, G // 2, 2, lc, lc)
    m_bd = jnp.concatenate([jnp.concatenate([m4[:, :, 0], zero], axis=-1),
                            jnp.concatenate([zero, m4[:, :, 1]], axis=-1)], axis=-2)
    p_stack = p_grp.reshape(2, G // 2, 2 * lc, 4 * P)
    q_stack = q_grp.reshape(2, G // 2, 2, 4 * P, lc).transpose(0, 1, 3, 2, 4).reshape(2, G // 2, 4 * P, 2 * lc)

    alr, ali = apow(jnp.arange(L, L + 1))
    a_l = jnp.concatenate([alr[0].reshape(2, G // 2, 2 * P), ali[0].reshape(2, G // 2, 2 * P)], axis=-1)
    return m_bd, p_stack, q_stack, a_l.reshape(2, G // 2, 1, 4 * P), perm


def _s5_kernel(x_ref, m_ref, p_ref, q_ref, al_ref, d_ref, perm_ref, y_ref, s_ref, xin_ref, *, n_chunks, bsz):
    hp = 2 * S5_STATE
    xb = x_ref[0].astype(BF16)
    perm = perm_ref[...]
    nt = (((1,), (1,)), ((), ()))
    m_mat, p_mat, q_mat = [], [], []
    for d in range(2):
        cols = lax.dot_general(m_ref[d, 0], perm, nt, preferred_element_type=F32).astype(BF16)
        m_mat.append(jnp.dot(perm, cols, preferred_element_type=F32).astype(BF16))
        p_mat.append(jnp.dot(perm, p_ref[d, 0], preferred_element_type=F32).astype(BF16))
        q_mat.append(lax.dot_general(q_ref[d, 0], perm, nt, preferred_element_type=F32).astype(BF16))
    for d in range(2):
        s = jnp.dot(xb, p_mat[d], preferred_element_type=F32)
        s_ref[2 * d] = s[:, :hp]
        s_ref[2 * d + 1] = s[:, hp:]

    al = [al_ref[d, 0] for d in range(2)]
    alr = [a[:, :hp] for a in al]
    ali = [a[:, hp:] for a in al]

    def body(i, carry):
        new = []
        for d in range(2):
            re, im = carry[2 * d], carry[2 * d + 1]
            n = i if d == 0 else n_chunks - 1 - i
            rows = pl.ds(n, bsz, stride=n_chunks)
            xin_ref[2 * d, rows, :] = re
            xin_ref[2 * d + 1, rows, :] = im
            new.append(alr[d] * re - ali[d] * im + s_ref[2 * d, rows, :])
            new.append(alr[d] * im + ali[d] * re + s_ref[2 * d + 1, rows, :])
        return tuple(new)

    zero = jnp.zeros((bsz, hp), F32)
    lax.fori_loop(0, n_chunks, body, (zero, zero, zero, zero))

    xin = [jnp.concatenate([xin_ref[2 * d], xin_ref[2 * d + 1]], axis=-1).astype(BF16) for d in range(2)]
    y = x_ref[0] * d_ref[0]
    for d in range(2):
        y = y + jnp.dot(xb, m_mat[d], preferred_element_type=F32)
        y = y + jnp.dot(xin[d], q_mat[d], preferred_element_type=F32)
    y_ref[0] = y


S5_PAIR = 2 * S5_GROUP_SIZE
S5_PER_VREG = LANES // S5_PAIR


def _s5_proj_kernel(x_ref, w_ref, o_ref, tok_ref):
    c = S5_PAIR
    rows = o_ref.shape[1]
    res = jnp.dot(x_ref[...], w_ref[...], preferred_element_type=F32)
    for j in range(tok_ref.shape[0]):
        tok_ref[j] = res[:, j * LANES:(j + 1) * LANES]
    for j in range(tok_ref.shape[0]):
        for t in range(S5_CHUNK):
            blk = tok_ref[j, pl.ds(t, rows, stride=S5_CHUNK), :]
            for gg in range(S5_PER_VREG):
                o_ref[j * S5_PER_VREG + gg, :, t * c:(t + 1) * c] = blk[:, gg * c:(gg + 1) * c]


def _s5_unpack_kernel(y_ref, o_ref, tok_ref):
    c = S5_PAIR
    rows = y_ref.shape[1]
    for j in range(tok_ref.shape[0]):
        for t in range(S5_CHUNK):
            blk = jnp.concatenate([y_ref[j * S5_PER_VREG + gg, :, t * c:(t + 1) * c] for gg in range(S5_PER_VREG)],
                                  axis=-1)
            tok_ref[j, pl.ds(t, rows, stride=S5_CHUNK), :] = blk
    for j in range(tok_ref.shape[0]):
        o_ref[:, j * LANES:(j + 1) * LANES] = tok_ref[j]


def _s5_proj(xb, w_u, tm=256):
    t, d = xb.shape
    rows = t // S5_CHUNK
    lc = S5_CHUNK * S5_PAIR
    w = w_u.shape[1]
    return pl.pallas_call(
        _s5_proj_kernel, grid=(rows // tm,),
        in_specs=[pl.BlockSpec((tm * S5_CHUNK, d), lambda i: (i, 0)), pl.BlockSpec(w_u.shape, lambda i: (0, 0))],
        out_specs=pl.BlockSpec((S5_GROUPS // 2, tm, lc), lambda i: (0, i, 0)),
        out_shape=jax.ShapeDtypeStruct((S5_GROUPS // 2, rows, lc), F32),
        scratch_shapes=[pltpu.VMEM((w // LANES, tm * S5_CHUNK, LANES), F32)],
        compiler_params=_params(1), name="s5_proj")(xb, w_u)


def _s5(x, mats, d_skip, b, s):
    L, C, G = S5_CHUNK, S5_GROUP_SIZE, S5_GROUPS
    w = G * C
    n = s // L
    lc = L * S5_PAIR
    st = 4 * S5_STATE
    m_mat, p_pad, q_pad, a_l, perm = mats
    d_row = jnp.tile(d_skip.astype(F32).reshape(G // 2, 1, S5_PAIR), (1, L, 1)).reshape(G // 2, 1, lc)
    spec = lambda r, c_: pl.BlockSpec((2, 1, r, c_), lambda k: (0, k, 0, 0))
    kern = functools.partial(_s5_kernel, n_chunks=n, bsz=b)
    y = pl.pallas_call(
        kern, grid=(G // 2,),
        in_specs=[pl.BlockSpec((1, n * b, lc), lambda k: (k, 0, 0)), spec(lc, lc), spec(lc, st), spec(st, lc),
                  spec(1, st), pl.BlockSpec((1, 1, lc), lambda k: (k, 0, 0)),
                  pl.BlockSpec((lc, lc), lambda k: (0, 0))],
        out_specs=pl.BlockSpec((1, n * b, lc), lambda k: (k, 0, 0)),
        out_shape=jax.ShapeDtypeStruct((G // 2, n * b, lc), F32),
        scratch_shapes=[pltpu.VMEM((4, n * b, LANES), F32), pltpu.VMEM((4, n * b, LANES), F32)],
        compiler_params=_params(1), name="s5")(x, m_mat, p_pad, q_pad, a_l, d_row, perm)
    tm = 256
    return pl.pallas_call(
        _s5_unpack_kernel, grid=(b * n // tm,),
        in_specs=[pl.BlockSpec((G // 2, tm, lc), lambda i: (0, i, 0))],
        out_specs=pl.BlockSpec((tm * L, w), lambda i: (i, 0)),
        out_shape=jax.ShapeDtypeStruct((b * s, w), F32),
        scratch_shapes=[pltpu.VMEM((w // LANES, tm * L, LANES), F32)],
        compiler_params=_params(1), name="s5_unpack")(y)


def _gla_kernel(qf_ref, kf_ref, vf_ref, gf_ref, qb_ref, kb_ref, vb_ref, gb_ref, bg_ref,
                of_ref, ob_ref, st_ref, *, rows):
    c = GLA_CHUNK
    n_sub = rows // c

    @pl.when(pl.program_id(1) == 0)
    def _():
        st_ref[...] = jnp.zeros_like(st_ref)

    nt = (((1,), (1,)), ((), ()))
    tn = (((0,), (0,)), ((), ()))
    log2 = lambda n: int(math.log2(n))
    r_r = lax.broadcasted_iota(jnp.int32, (rows, rows), 0)
    c_r = lax.broadcasted_iota(jnp.int32, (rows, rows), 1)
    same_chunk = (r_r >> log2(c)) == (c_r >> log2(c))
    t_row = lax.broadcasted_iota(jnp.int32, (c, GLA_K), 0)
    t_col = lax.broadcasted_iota(jnp.int32, (c, GLA_K), 1) & (c - 1)
    k_lane = lax.broadcasted_iota(jnp.int32, (1, GLA_K), 1) >> log2(GLA_DK)
    v_lane = lax.broadcasted_iota(jnp.int32, (1, BRANCH_WIDTH), 1) >> log2(GLA_DV)
    st_same_head = ((lax.broadcasted_iota(jnp.int32, (BRANCH_WIDTH, GLA_K), 0) >> log2(GLA_DV))
                    == (lax.broadcasted_iota(jnp.int32, (BRANCH_WIDTH, GLA_K), 1) >> log2(GLA_DK)))

    for d, (q_ref, k_ref, v_ref, g_ref, o_ref) in enumerate(
            ((qf_ref, kf_ref, vf_ref, gf_ref, of_ref), (qb_ref, kb_ref, vb_ref, gb_ref, ob_ref))):
        gate = g_ref[0].astype(F32) + bg_ref[d]
        log_a = (jnp.minimum(gate, 0.0) - jnp.log(1.0 + jnp.exp(-jnp.abs(gate)))) * (1.0 / GLA_TAU)
        if d == 0:
            tri = jnp.where(same_chunk, jnp.where(r_r >= c_r, 1.0, 0.0), 0.0).astype(BF16)
            keep, i_ref, i_last = (t_row >= t_col), c // 2, c - 1
        else:
            tri = jnp.where(same_chunk, jnp.where(c_r >= r_r, 1.0, 0.0), 0.0).astype(BF16)
            keep, i_ref, i_last = (t_col > t_row), c - 1 - c // 2, 0
        la_hi = log_a.astype(BF16)
        la_lo = (log_a - la_hi.astype(F32)).astype(BF16)
        cum_all = (jnp.dot(tri, la_hi, preferred_element_type=F32) + jnp.dot(tri, la_lo, preferred_element_type=F32))
        q = q_ref[0].astype(F32) * (GLA_DK ** -0.5)
        k = k_ref[0].astype(F32)
        v = v_ref[0]
        for ci in (range(n_sub) if d == 0 else range(n_sub - 1, -1, -1)):
            sl = slice(ci * c, (ci + 1) * c)
            cum = cum_all[sl]
            mid = cum[i_ref:i_ref + 1]
            last = cum[i_last:i_last + 1]
            qc, kc, vc = q[sl], k[sl], v[sl]
            qe = (qc * jnp.exp(cum - mid)).astype(BF16)
            ke = kc * jnp.exp(mid - cum)
            kd = (kc * jnp.exp(last - cum)).astype(BF16)
            qs = (qc * jnp.exp(cum)).astype(BF16)
            decay = jnp.exp(last)
            ke4 = jnp.concatenate([jnp.where(k_lane == h, ke, 0.0) for h in range(GLA_HEADS)], axis=0).astype(BF16)
            v4 = jnp.concatenate([jnp.where(v_lane == h, vc, 0.0) for h in range(GLA_HEADS)], axis=0).astype(BF16)
            sc = lax.dot_general(qe, ke4, nt, preferred_element_type=F32)
            sc = jnp.where(keep, sc, 0.0).astype(BF16)
            st = st_ref[d]
            o = (jnp.dot(sc, v4, preferred_element_type=F32)
                 + lax.dot_general(qs, st.astype(BF16), nt, preferred_element_type=F32))
            o_ref[0, sl, :] = o
            ds = lax.dot_general(vc.astype(BF16), kd, tn, preferred_element_type=F32)
            for h in range(GLA_HEADS):
                rs = slice(h * GLA_DV, (h + 1) * GLA_DV)
                ls = slice(h * GLA_DK // LANES * LANES, h * GLA_DK // LANES * LANES + LANES)
                st_ref[d, rs, ls] = decay[:, ls] * st[rs, ls] + jnp.where(st_same_head[rs, ls], ds[rs, ls], 0.0)


def _gla(proj_a3, proj_b3, b_gate, rows=128):
    b, s, _ = proj_a3.shape
    n = s // rows
    bg = b_gate.astype(F32).reshape(2, 1, GLA_K)

    def specs(rev):
        blk = (lambda i: n - 1 - i) if rev else (lambda i: i)
        return [pl.BlockSpec((1, rows, GLA_K), lambda bi, i: (bi, blk(i), A_QC // GLA_K)),
                pl.BlockSpec((1, rows, GLA_K), lambda bi, i: (bi, blk(i), A_KC // GLA_K)),
                pl.BlockSpec((1, rows, BRANCH_WIDTH), lambda bi, i: (bi, blk(i), A_VC // BRANCH_WIDTH)),
                pl.BlockSpec((1, rows, GLA_K), lambda bi, i: (bi, blk(i), GATE_COLS // GLA_K + int(rev)))]

    out_f = pl.BlockSpec((1, rows, BRANCH_WIDTH), lambda bi, i: (bi, i, 0))
    out_b = pl.BlockSpec((1, rows, BRANCH_WIDTH), lambda bi, i: (bi, n - 1 - i, 0))
    shape = jax.ShapeDtypeStruct((b, s, BRANCH_WIDTH), F32)
    return pl.pallas_call(
        functools.partial(_gla_kernel, rows=rows), grid=(b, n),
        in_specs=specs(False) + specs(True) + [pl.BlockSpec((2, 1, GLA_K), lambda bi, i: (0, 0, 0))],
        out_specs=[out_f, out_b], out_shape=[shape, shape],
        scratch_shapes=[pltpu.VMEM((2, BRANCH_WIDTH, GLA_K), F32)],
        compiler_params=_params(2), name="gla")(
            proj_a3, proj_a3, proj_a3, proj_b3, proj_a3, proj_a3, proj_a3, proj_b3, bg)


def _ranks(vals):
    ranks = []
    for i, v in enumerate(vals):
        r = jnp.zeros(v.shape, F32)
        for o, other in enumerate(vals):
            if o != i:
                r = r + (jnp.where(other >= v, 1.0, 0.0) if o < i else jnp.where(other > v, 1.0, 0.0))
        ranks.append(r)
    return ranks


def _route(scores_t, bias_col):
    gs = EXPERTS_PER_GROUP
    s = [scores_t[e:e + 1] for e in range(N_EXPERTS)]
    biased = [s[e] + bias_col[e:e + 1] for e in range(N_EXPERTS)]
    top2, group_score = [], []
    for g in range(N_EXPERTS // gs):
        rk = _ranks(biased[g * gs:(g + 1) * gs])
        top2 += [r < 2.0 for r in rk]
        kept = [jnp.where(top2[g * gs + j], biased[g * gs + j], 0.0) for j in range(gs)]
        group_score.append(functools.reduce(lambda a, b: a + b, kept))
    best = [r < 1.0 for r in _ranks(group_score)]
    w = [jnp.where(top2[e], jnp.where(best[e // gs], s[e], 0.0), 0.0) for e in range(N_EXPERTS)]
    total = functools.reduce(lambda a, b: a + b, w)
    return [we / total for we in w]


def _merge_kernel(x_ref, oa_ref, ys_ref, gf_ref, gb_ref, r_ref, gl0_ref, gl1_ref, gl2_ref,
                  wglu_ref, bglu_ref, gng_ref, wup_ref, mb_ref, wout_ref, lng_ref, lnb_ref, rw_ref, rb_ref,
                  xo_ref, xob_ref, comb_ref):
    y = ys_ref[...]
    gelu = 0.5 * y * (1.0 + jnp.tanh(math.sqrt(2.0 / math.pi) * (y + 0.044715 * (y * y * y))))
    o_b = gelu * _sigmoid(jnp.dot(gelu.astype(BF16), wglu_ref[...], preferred_element_type=F32) + bglu_ref[...])

    r = r_ref[...].astype(F32)
    gated = r * _sigmoid(r)
    heads = []
    for h in range(GLA_HEADS):
        sl = slice(h * GLA_DV, (h + 1) * GLA_DV)
        o = gf_ref[:, sl] + gb_ref[:, sl]
        ms = jnp.mean(o * o, axis=-1, keepdims=True)
        heads.append(o * lax.rsqrt(ms + NORM_EPS) * gng_ref[...] * gated[:, sl])
    o_c = jnp.concatenate(heads, axis=-1)

    merged = None
    for n, (o_n, gl_ref) in enumerate(((oa_ref[...], gl0_ref), (o_b, gl1_ref), (o_c, gl2_ref))):
        gate = _sigmoid(gl_ref[...].astype(F32) + mb_ref[n])
        term = gate * jnp.dot(o_n.astype(BF16), wup_ref[n], preferred_element_type=F32)
        merged = term if merged is None else merged + term
    mix = jnp.dot(merged.astype(BF16), wout_ref[...], preferred_element_type=F32)
    x1 = _layer_norm(DN_ALPHA * x_ref[...] + mix, lng_ref[...], lnb_ref[...])
    xo_ref[...] = x1
    x_hi = x1.astype(BF16)
    xob_ref[...] = x_hi

    x_lo = (x1 - x_hi.astype(F32)).astype(BF16)
    logits = (jnp.dot(x_hi, rw_ref[0], preferred_element_type=F32) + jnp.dot(x_lo, rw_ref[0], preferred_element_type=F32)
              + jnp.dot(x_hi, rw_ref[1], preferred_element_type=F32))
    rows = _route(_sigmoid(logits.T[:N_EXPERTS]), rb_ref[...])
    comb_ref[...] = jnp.zeros(comb_ref.shape, F32)
    for e in range(N_EXPERTS):
        comb_ref[e:e + 1, :] = rows[e]


def _merge(x, o_a, y_s5, g_f, g_b, proj_a, proj_b, wglu, bglu, gng, wup, mb, wout, lng, lnb, rw, rb, tm=256):
    t, d = x.shape
    w = BRANCH_WIDTH
    row = lambda width, cb=0: pl.BlockSpec((tm, width), lambda i: (i, cb))
    full = lambda a: pl.BlockSpec(a.shape, lambda i: (0,) * a.ndim)
    weights = (wglu, bglu, gng, wup, mb, wout, lng, lnb, rw, rb)
    return pl.pallas_call(
        _merge_kernel, grid=(t // tm,),
        in_specs=[row(d), row(w), row(w), row(w), row(w), row(w, A_RC // w), row(d, 0), row(d, 1), row(d, 2)]
        + [full(a) for a in weights],
        out_specs=[row(d), row(d), pl.BlockSpec((LANES, tm), lambda i: (0, i))],
        out_shape=[jax.ShapeDtypeStruct((t, d), F32), jax.ShapeDtypeStruct((t, d), BF16),
                   jax.ShapeDtypeStruct((LANES, t), F32)],
        compiler_params=_params(1), name="merge")(
            x, o_a, y_s5, g_f, g_b, proj_a, proj_b, proj_b, proj_b, *weights)


MOE_SUB = 256
MOE_CAP = 64


def _swiglu(x, wg_ref, wu_ref, wd_ref, row_scale):
    g = jnp.dot(x, wg_ref[0], preferred_element_type=F32)
    u = jnp.dot(x, wu_ref[0], preferred_element_type=F32)
    h = (g * _sigmoid(g)) * u * row_scale
    return jnp.dot(h.astype(BF16), wd_ref[0], preferred_element_type=F32)


def _moe_kernel(xb_ref, xf_ref, comb_ref, wg_ref, wu_ref, wd_ref, lng_ref, lnb_ref, of_ref, ob_ref,
                pos_ref, ind_ref, sel_ref, xg_ref, y_ref, cg_ref, acc_ref, flag_ref, *, tm, cap):
    e = pl.program_id(1)
    ns = tm // MOE_SUB
    gs = EXPERTS_PER_GROUP
    sub = lambda k: slice(k * MOE_SUB, (k + 1) * MOE_SUB)
    slots = lambda k: slice(k * cap, (k + 1) * cap)
    r_i = lax.broadcasted_iota(jnp.int32, (MOE_SUB, MOE_SUB), 0)
    c_i = lax.broadcasted_iota(jnp.int32, (MOE_SUB, MOE_SUB), 1)

    @pl.when(e == 0)
    def _():
        ind = jnp.where(comb_ref[...] > 0.0, 1.0, 0.0)
        ind_ref[...] = ind
        before = jnp.where(r_i < c_i, 1.0, 0.0).astype(BF16)
        most = jnp.zeros((LANES, 1), F32)
        for k in range(ns):
            pos_ref[:, sub(k)] = jnp.dot(ind[:, sub(k)].astype(BF16), before, preferred_element_type=F32)
            most = jnp.maximum(most, jnp.sum(ind[:, sub(k)], axis=-1, keepdims=True))
        flag_ref[0] = (jnp.max(most) > cap).astype(jnp.int32)
        acc_ref[...] = jnp.zeros_like(acc_ref)

    slot_id = lax.broadcasted_iota(jnp.int32, (cap, 1), 0).astype(F32)
    for grp in range(N_EXPERTS // gs):
        @pl.when(e == grp * gs)
        def _(grp=grp):
            for k in range(ns):
                pieces = []
                for j in range(gs):
                    ee = grp * gs + j
                    one_hot = jnp.where(pos_ref[ee:ee + 1, sub(k)] == slot_id, ind_ref[ee:ee + 1, sub(k)], 0.0)
                    cg_ref[j, slots(k), :] = jnp.sum(one_hot * comb_ref[ee:ee + 1, sub(k)], axis=-1, keepdims=True)
                    pieces.append(one_hot)
                sel = jnp.concatenate(pieces, axis=0).astype(BF16)
                sel_ref[k] = sel
                xg = jnp.dot(sel, xb_ref[sub(k), :], preferred_element_type=F32).astype(BF16)
                for j in range(gs):
                    xg_ref[j, slots(k), :] = xg[j * cap:(j + 1) * cap]

    j = e % gs
    y_ref[j] = _swiglu(xg_ref[j], wg_ref, wu_ref, wd_ref, cg_ref[j]).astype(BF16)

    @pl.when(j == gs - 1)
    def _():
        for k in range(ns):
            yk = jnp.concatenate([y_ref[jj, slots(k), :] for jj in range(gs)], axis=0)
            acc_ref[sub(k), :] += lax.dot_general(sel_ref[k], yk, (((0,), (0,)), ((), ())),
                                                  preferred_element_type=F32)

    @pl.when(flag_ref[0] > 0)
    def _():
        comb = comb_ref[...].T
        earlier = jnp.where(r_i > c_i, 1.0, 0.0).astype(BF16)
        over = []
        for k in range(ns):
            pos = jnp.dot(earlier, jnp.where(comb[sub(k)] > 0.0, 1.0, 0.0).astype(BF16), preferred_element_type=F32)
            over.append(jnp.where(pos >= cap, comb[sub(k)], 0.0))
        over = jnp.concatenate(over, axis=0)
        lane = lax.broadcasted_iota(jnp.int32, over.shape, 1)
        ce = jnp.sum(jnp.where(lane == e, over, 0.0), axis=-1, keepdims=True)
        acc_ref[...] += _swiglu(xb_ref[...], wg_ref, wu_ref, wd_ref, ce)

    @pl.when(e == N_EXPERTS - 1)
    def _():
        y = _layer_norm(DN_ALPHA * xf_ref[...] + acc_ref[...], lng_ref[...], lnb_ref[...])
        of_ref[...] = y
        ob_ref[...] = y.astype(BF16)


def _moe(xb, xf, comb, wg, wu, wd, lng, lnb, tm=1024, cap=MOE_CAP):
    t, d = xf.shape
    ns = tm // MOE_SUB
    gs = EXPERTS_PER_GROUP
    row = lambda width: pl.BlockSpec((tm, width), lambda i, e: (i, 0))
    vec = pl.BlockSpec((1, d), lambda i, e: (0, 0))
    return pl.pallas_call(
        functools.partial(_moe_kernel, tm=tm, cap=cap), grid=(t // tm, N_EXPERTS),
        in_specs=[row(d), row(d), pl.BlockSpec((LANES, tm), lambda i, e: (0, i)),
                  pl.BlockSpec((1, d, EXPERT_FF), lambda i, e: (e, 0, 0)),
                  pl.BlockSpec((1, d, EXPERT_FF), lambda i, e: (e, 0, 0)),
                  pl.BlockSpec((1, EXPERT_FF, d), lambda i, e: (e, 0, 0)), vec, vec],
        out_specs=[row(d), row(d)],
        out_shape=[jax.ShapeDtypeStruct((t, d), F32), jax.ShapeDtypeStruct((t, d), BF16)],
        scratch_shapes=[pltpu.VMEM((LANES, tm), F32), pltpu.VMEM((LANES, tm), F32),
                        pltpu.VMEM((ns, gs * cap, MOE_SUB), BF16), pltpu.VMEM((gs, ns * cap, d), BF16),
                        pltpu.VMEM((gs, ns * cap, d), BF16), pltpu.VMEM((gs, ns * cap, 1), F32),
                        pltpu.VMEM((tm, d), F32), pltpu.SMEM((1,), jnp.int32)],
        compiler_params=_params(2), name="moe")(xb, xf, comb, wg, wu, wd, lng, lnb)


def _proj_b_weights(w_in_l, w_gate):
    z0 = W_IN_Z
    folded = [jnp.dot(w_in_l[:, z0 + i * GLA_GATE_RANK:z0 + (i + 1) * GLA_GATE_RANK].astype(F32),
                      w_gate[i].astype(F32), precision=HIGHEST) for i in range(2)]
    return jnp.concatenate([w_in_l[:, z0 + 2 * GLA_GATE_RANK:].astype(F32)] + folded, axis=1).astype(BF16)


def kernel(x, ln0_g, ln0_b, w_in, da_lambda, da_norm_g, s5_a_re, s5_a_im, s5_log_dt, s5_b_re, s5_b_im,
           s5_c_re, s5_c_im, s5_d, s5_w_glu, s5_b_glu, gla_w_gate, gla_b_gate, gla_norm_g, merge_w_up,
           merge_b, w_out, ln1_g, ln1_b, router_w, router_bias, moe_w_gate, moe_w_up, moe_w_down,
           ln2_g, ln2_b):
    b, s, d = x.shape
    t = b * s
    vec = lambda a: a.astype(F32).reshape(1, -1)
    xf, xb = _ln0(x.reshape(t, d), ln0_g.astype(F32), ln0_b.astype(F32))
    rw32 = jnp.zeros((d, LANES), F32).at[:, :N_EXPERTS].set(router_w.astype(F32))
    rw_hi = rw32.astype(BF16)
    rw = jnp.stack([rw_hi, (rw32 - rw_hi.astype(F32)).astype(BF16)])
    rb = router_bias.astype(F32).reshape(N_EXPERTS, 1)
    for l in range(DEPTH):
        w_a = jnp.concatenate([w_in[l][:, :W_IN_U], w_in[l][:, W_IN_GLA:W_IN_Z]], axis=1).astype(BF16)
        w_u = w_in[l][:, W_IN_U:W_IN_GLA].astype(BF16)
        w_b = _proj_b_weights(w_in[l], gla_w_gate[l])
        proj_a = _matmul(xb, w_a, 1024, PROJ_A // 2, "in_proj_a")
        proj_b = _matmul(xb, w_b, 512, PROJ_B, "in_proj_b")
        proj_a3 = proj_a.reshape(b, s, PROJ_A)
        proj_b3 = proj_b.reshape(b, s, PROJ_B)

        lam_init = 0.8 - 0.6 * math.exp(-0.3 * l)
        o_a = _diff_attention(proj_a3, da_lambda[l].astype(F32), da_norm_g[l].astype(F32), lam_init)
        mats = _s5_matrices(s5_a_re[l], s5_a_im[l], s5_log_dt[l], s5_b_re[l], s5_b_im[l], s5_c_re[l], s5_c_im[l])
        y_s5 = _s5(_s5_proj(xb, w_u), mats, s5_d[l], b, s)
        g_f, g_b = _gla(proj_a3, proj_b3, gla_b_gate[l])

        xf, xb, comb = _merge(
            xf, o_a.reshape(t, BRANCH_WIDTH), y_s5, g_f.reshape(t, BRANCH_WIDTH), g_b.reshape(t, BRANCH_WIDTH),
            proj_a, proj_b, s5_w_glu[l].astype(BF16), vec(s5_b_glu[l]), vec(gla_norm_g[l]),
            merge_w_up[l].astype(BF16), merge_b[l].astype(F32).reshape(N_BRANCH, 1, d), w_out[l].astype(BF16),
            vec(ln1_g[l]), vec(ln1_b[l]), rw, rb)
        xf, xb = _moe(xb, xf, comb, moe_w_gate[l].astype(BF16), moe_w_up[l].astype(BF16),
                      moe_w_down[l].astype(BF16), vec(ln2_g[l]), vec(ln2_b[l]))
    return xf.reshape(b, s, d)
```

```python
import functools
import math

import jax
import jax.numpy as jnp
from jax import lax
from jax.experimental import pallas as pl
from jax.experimental.pallas import tpu as pltpu

F32 = jnp.float32
BF16 = jnp.bfloat16
HIGHEST = lax.Precision.HIGHEST

D_MODEL = 1024
DEPTH = 2
BRANCH_WIDTH = D_MODEL // 2
N_BRANCH = 3

DA_HEADS = 4
DA_HEAD_DIM = BRANCH_WIDTH // (2 * DA_HEADS)
DA_V_DIM = 2 * DA_HEAD_DIM
ALIBI_MAX_EXP = 8.0

S5_GROUP_SIZE = 16
S5_GROUPS = BRANCH_WIDTH // S5_GROUP_SIZE
S5_STATE = 64
S5_CHUNK = 16

GLA_HEADS = 4
GLA_DV = BRANCH_WIDTH // GLA_HEADS
GLA_DK = GLA_DV // 2
GLA_K = GLA_HEADS * GLA_DK
GLA_GATE_RANK = 16
GLA_TAU = 16.0
GLA_CHUNK = 64

N_EXPERTS = 16
EXPERTS_PER_GROUP = 4
EXPERT_FF = D_MODEL // 2

DN_ALPHA = (2.0 * DEPTH) ** 0.25
NORM_EPS = 1e-5
LOG2E = math.log2(math.e)

LANES = 128
W_IN_U = 3 * BRANCH_WIDTH
W_IN_GLA = W_IN_U + BRANCH_WIDTH
W_IN_Z = W_IN_GLA + 2 * GLA_K + 2 * BRANCH_WIDTH
A_QC = W_IN_U
A_KC = A_QC + GLA_K
A_VC = A_KC + GLA_K
A_RC = A_VC + BRANCH_WIDTH
PROJ_A = A_RC + BRANCH_WIDTH
GATE_COLS = N_BRANCH * D_MODEL
PROJ_B = GATE_COLS + 2 * GLA_K
VMEM_LIMIT = 56 * 1024 * 1024


def _params(n_grid_dims):
    return pltpu.CompilerParams(dimension_semantics=("arbitrary",) * n_grid_dims,
                                vmem_limit_bytes=VMEM_LIMIT)


def _layer_norm(xf, g, b):
    mu = jnp.mean(xf, axis=-1, keepdims=True)
    xc = xf - mu
    var = jnp.mean(xc * xc, axis=-1, keepdims=True)
    return xc * lax.rsqrt(var + NORM_EPS) * g + b


def _sigmoid(x):
    return 1.0 / (1.0 + jnp.exp(-x))


def _ln0_kernel(x_ref, g_ref, b_ref, of_ref, ob_ref):
    y = _layer_norm(x_ref[...], g_ref[...], b_ref[...])
    of_ref[...] = y
    ob_ref[...] = y.astype(BF16)


def _ln0(x2, g, b, tm=512):
    t, d = x2.shape
    row = pl.BlockSpec((tm, d), lambda i: (i, 0))
    vec = pl.BlockSpec((1, d), lambda i: (0, 0))
    return pl.pallas_call(
        _ln0_kernel, grid=(t // tm,), in_specs=[row, vec, vec], out_specs=[row, row],
        out_shape=[jax.ShapeDtypeStruct((t, d), F32), jax.ShapeDtypeStruct((t, d), BF16)],
        compiler_params=_params(1), name="ln0")(x2, g.reshape(1, d), b.reshape(1, d))


def _mm_kernel(x_ref, w_ref, o_ref):
    o_ref[...] = jnp.dot(x_ref[...], w_ref[...], preferred_element_type=F32).astype(o_ref.dtype)


def _matmul(x, w, tm, tn, name):
    t, k = x.shape
    n = w.shape[1]
    return pl.pallas_call(
        _mm_kernel, grid=(t // tm, n // tn),
        in_specs=[pl.BlockSpec((tm, k), lambda i, j: (i, 0)), pl.BlockSpec((k, tn), lambda i, j: (0, j))],
        out_specs=pl.BlockSpec((tm, tn), lambda i, j: (i, j)),
        out_shape=jax.ShapeDtypeStruct((t, n), BF16),
        compiler_params=_params(2), name=name)(x, w)


def _split3(x):
    hi = x.astype(BF16).astype(F32)
    mid = (x - hi).astype(BF16).astype(F32)
    lo = (x - hi - mid).astype(BF16).astype(F32)
    return hi, mid, lo


def _attn_kernel(lam_ref, g_ref, q_ref, k_ref, v_ref, o_ref, ka_ref, vb_ref, lhs_ref, fix_ref, m_ref, acc_ref,
                 *, tq, tk, rc, seq, lam_init):
    h = pl.program_id(1)
    qi = pl.program_id(2)
    dh = DA_HEAD_DIM

    slopes = [2.0 ** (-ALIBI_MAX_EXP * (i + 1) / DA_HEADS) for i in range(DA_HEADS)]
    slope = jnp.float32(slopes[-1])
    for i in range(DA_HEADS - 1):
        slope = jnp.where(h == i, jnp.float32(slopes[i]), slope)
    sigma = slope * LOG2E

    @pl.when(qi == 0)
    def _():
        ka_ref[:LANES, :] = k_ref[0].astype(F32).T.astype(BF16)
        t_hi, t_mid, t_lo = _split3(lax.broadcasted_iota(jnp.int32, (1, seq), 1).astype(F32) * sigma)
        sub_k = lax.broadcasted_iota(jnp.int32, (LANES, seq), 0)
        aug_k = jnp.where(sub_k < 3, 1.0, jnp.where(sub_k == 3, t_hi, jnp.where(sub_k == 4, t_mid,
                          jnp.where(sub_k == 5, t_lo, 0.0))))
        ka_ref[LANES:, :] = aug_k.astype(BF16)
        lane_v = lax.broadcasted_iota(jnp.int32, (seq, LANES), 1)
        vb_ref[:, :LANES] = v_ref[0].astype(BF16)
        vb_ref[:, LANES:] = jnp.where(lane_v == 0, 1.0, 0.0).astype(BF16)
        for r in range(tk // rc):
            delta = (r * rc + lax.broadcasted_iota(jnp.int32, (rc, 1), 0)
                     - lax.broadcasted_iota(jnp.int32, (1, tk), 1)).astype(F32)
            fix_ref[r] = (2.0 * sigma) * jnp.minimum(delta, 0.0)

    q = q_ref[0].astype(F32) * (dh ** -0.5 * LOG2E)
    lane = lax.broadcasted_iota(jnp.int32, (tq, LANES), 1)
    row = lax.broadcasted_iota(jnp.int32, (tq, 1), 0)
    u_hi, u_mid, u_lo = _split3((qi * tq + row).astype(F32) * (-sigma))
    aug_q = jnp.where(lane == 0, u_hi, jnp.where(lane == 1, u_mid, jnp.where(lane == 2, u_lo,
                      jnp.where(lane < 6, 1.0, 0.0))))
    for mp in range(2):
        qz = jnp.where(lane < dh, q, 0.0) if mp == 0 else jnp.where(lane >= dh, q, 0.0)
        lhs_ref[mp, 0] = jnp.concatenate([qz, aug_q], axis=-1).astype(BF16)
        lhs_ref[mp, 1] = jnp.concatenate([qz, -aug_q], axis=-1).astype(BF16)
    m_ref[...] = jnp.full(m_ref.shape, -1e30, F32)
    acc_ref[...] = jnp.zeros(acc_ref.shape, F32)

    def block(j, local):
        off = pl.multiple_of(j * tk, tk)
        ka = ka_ref[:, pl.ds(off, tk)]
        vb = vb_ref[pl.ds(off, tk), :]
        for mp in range(2):
            for c in range(tq // rc):
                rows = slice(c * rc, (c + 1) * rc)
                if isinstance(local, str):
                    sign, fixup = (0 if local == "before" else 1), None
                elif (c + 1) * rc <= local * tk:
                    sign, fixup = 1, None
                elif c * rc >= (local + 1) * tk:
                    sign, fixup = 0, None
                else:
                    sign, fixup = 0, fix_ref[(c * rc - local * tk) // rc]
                s = jnp.dot(lhs_ref[mp, sign, rows, :], ka, preferred_element_type=F32)
                if fixup is not None:
                    s = s + fixup
                m_old = m_ref[mp, rows, :]
                m_new = jnp.maximum(m_old, jnp.max(s, axis=-1, keepdims=True))
                alpha = jnp.exp2(m_old - m_new)
                p = jnp.exp2(s - m_new).astype(BF16)
                acc_ref[mp, rows, :] = alpha * acc_ref[mp, rows, :] + jnp.dot(p, vb, preferred_element_type=F32)
                m_ref[mp, rows, :] = m_new

    def run(lo, hi, where):
        def body(j, carry):
            block(j, where)
            return carry
        lax.fori_loop(lo, hi, body, 0)

    j0 = (qi * tq) // tk
    run(0, j0, "before")
    for local in range(tq // tk):
        block(j0 + local, local)
    run(j0 + tq // tk, seq // tk, "after")

    lv = lam_ref[...]
    lam = (jnp.exp(jnp.sum(lv[0:1] * lv[1:2], axis=-1, keepdims=True))
           - jnp.exp(jnp.sum(lv[2:3] * lv[3:4], axis=-1, keepdims=True)) + lam_init)
    o1 = acc_ref[0, :, :LANES] / acc_ref[0, :, LANES:LANES + 1]
    o2 = acc_ref[1, :, :LANES] / acc_ref[1, :, LANES:LANES + 1]
    o = o1 - lam * o2
    ms = jnp.mean(o * o, axis=-1, keepdims=True)
    o_ref[0] = o * lax.rsqrt(ms + NORM_EPS) * g_ref[...] * (1.0 - lam_init)


def _diff_attention(proj_a3, da_lambda, da_norm_g, lam_init, tq=2048, tk=1024, rc=256):
    b, s, _ = proj_a3.shape
    assert tq % tk == 0 and s % tq == 0 and tk % rc == 0
    kern = functools.partial(_attn_kernel, tq=tq, tk=tk, rc=rc, seq=s, lam_init=lam_init)
    return pl.pallas_call(
        kern, grid=(b, DA_HEADS, s // tq),
        in_specs=[
            pl.BlockSpec((4, DA_HEAD_DIM), lambda bi, h, i: (0, 0)),
            pl.BlockSpec((1, DA_V_DIM), lambda bi, h, i: (0, 0)),
            pl.BlockSpec((1, tq, LANES), lambda bi, h, i: (bi, i, h)),
            pl.BlockSpec((1, s, LANES), lambda bi, h, i: (bi, 0, DA_HEADS + h)),
            pl.BlockSpec((1, s, LANES), lambda bi, h, i: (bi, 0, 2 * DA_HEADS + h)),
        ],
        out_specs=pl.BlockSpec((1, tq, LANES), lambda bi, h, i: (bi, i, h)),
        out_shape=jax.ShapeDtypeStruct((b, s, BRANCH_WIDTH), F32),
        scratch_shapes=[pltpu.VMEM((2 * LANES, s), BF16), pltpu.VMEM((s, 2 * LANES), BF16),
                        pltpu.VMEM((2, 2, tq, 2 * LANES), BF16), pltpu.VMEM((tk // rc, rc, tk), F32),
                        pltpu.VMEM((2, tq, 1), F32), pltpu.VMEM((2, tq, 2 * LANES), F32)],
        compiler_params=_params(3), name="diff_attention")(
            da_lambda, da_norm_g.reshape(1, DA_V_DIM), proj_a3, proj_a3, proj_a3)


def _s5_matrices(a_re, a_im, log_dt, b_re, b_im, c_re, c_im):
    L, C, P, G = S5_CHUNK, S5_GROUP_SIZE, S5_STATE, S5_GROUPS
    f = lambda t: t.astype(F32)
    a_re, a_im, b_re, b_im, c_re, c_im = map(f, (a_re, a_im, b_re, b_im, c_re, c_im))
    dt = jnp.exp(f(log_dt))[..., None]
    lr, li = a_re * dt, a_im * dt

    def apow(j):
        jj = j.astype(F32)[:, None, None, None]
        mag = jnp.exp(lr[None] * jj)
        return mag * jnp.cos(li[None] * jj), mag * jnp.sin(li[None] * jj)

    a1r, a1i = apow(jnp.arange(1, 2))
    nr, ni = a1r[0] - 1.0, a1i[0]
    den = a_re * a_re + a_im * a_im
    cr, ci = (nr * a_re + ni * a_im) / den, (ni * a_re - nr * a_im) / den
    bbr = cr[..., None] * b_re - ci[..., None] * b_im
    bbi = cr[..., None] * b_im + ci[..., None] * b_re

    pr, pi = apow(jnp.arange(L))
    wr = pr[..., None] * bbr[None] - pi[..., None] * bbi[None]
    wi = pr[..., None] * bbi[None] + pi[..., None] * bbr[None]
    kern = (jnp.einsum('dgcp,jdgpe->jdgce', c_re, wr, precision=HIGHEST)
            - jnp.einsum('dgcp,jdgpe->jdgce', c_im, wi, precision=HIGHEST))

    s_idx = jnp.arange(L)[:, None]
    t_idx = jnp.arange(L)[None, :]

    def toeplitz(kd, lag, valid):
        m = kd[jnp.clip(lag, 0, L - 1)] * valid[:, :, None, None, None].astype(F32)
        return m.transpose(2, 0, 4, 1, 3).reshape(G, L * C, L * C)

    m_grp = jnp.stack([toeplitz(kern[:, 0], t_idx - s_idx, t_idx >= s_idx),
                       toeplitz(kern[:, 1], s_idx - t_idx, s_idx >= t_idx)]).astype(BF16)

    def summary(w_dir):
        return w_dir.transpose(1, 0, 3, 2).reshape(G, L * C, P)

    p_re = jnp.stack([summary(wr[::-1, 0]), summary(wr[:, 1])])
    p_im = jnp.stack([summary(wi[::-1, 0]), summary(wi[:, 1])])

    qr, qi = apow(jnp.arange(1, L + 1))
    car = c_re[None] * qr[:, :, :, None, :] - c_im[None] * qi[:, :, :, None, :]
    cai = c_re[None] * qi[:, :, :, None, :] + c_im[None] * qr[:, :, :, None, :]

    def readout(ca_dir):
        return ca_dir.transpose(1, 3, 0, 2).reshape(G, P, L * C)

    q_re = jnp.stack([readout(car[:, 0]), readout(car[::-1, 1])])
    q_im = jnp.stack([-readout(cai[:, 0]), -readout(cai[::-1, 1])])

    half = jax.nn.one_hot(jnp.arange(G) % 2, 2, dtype=F32)
    p_pad = jnp.stack([p_re, p_im], axis=-2)[:, :, :, :, None, :] * half[None, :, None, None, :, None]
    p_grp = p_pad.reshape(2, G, L * C, 4 * P).astype(BF16)
    q_grp = jnp.stack([q_re, q_im], axis=2)[:, :, :, None] * half[None, :, None, :, None, None]
    q_grp = q_grp.reshape(2, G, 4 * P, L * C).astype(BF16)

    lc = L * C
    i_new = jnp.arange(2 * lc)
    i_old = (i_new // C % 2) * lc + (i_new // (2 * C)) * C + i_new % C
    perm = (i_old[:, None] == i_new[None, :]).astype(BF16)
    m_bd = m_grp
    p_stack = p_grp.reshape(2, G // 2, 2 * lc, 4 * P)
    q_stack = q_grp.reshape(2, G // 2, 2, 4 * P, lc).transpose(0, 1, 3, 2, 4).reshape(2, G // 2, 4 * P, 2 * lc)

    alr, ali = apow(jnp.arange(L, L + 1))
    a_l = jnp.concatenate([alr[0].reshape(2, G // 2, 2 * P), ali[0].reshape(2, G // 2, 2 * P)], axis=-1)
    return m_bd, p_stack, q_stack, a_l.reshape(2, G // 2, 1, 4 * P), perm


def _s5_kernel(x_ref, m_ref, p_ref, q_ref, al_ref, d_ref, perm_ref, y_ref, s_ref, xin_ref, *, n_chunks, bsz):
    hp = 2 * S5_STATE
    xb = x_ref[0].astype(BF16)
    perm = perm_ref[...]
    nt = (((1,), (1,)), ((), ()))
    m_mat, p_mat, q_mat = [], [], []
    zero = jnp.zeros(m_ref.shape[2:], BF16)
    for d in range(2):
        m_bd = jnp.concatenate([jnp.concatenate([m_ref[d, 0], zero], axis=-1),
                                jnp.concatenate([zero, m_ref[d, 1]], axis=-1)], axis=0)
        cols = lax.dot_general(m_bd, perm, nt, preferred_element_type=F32).astype(BF16)
        m_mat.append(jnp.dot(perm, cols, preferred_element_type=F32).astype(BF16))
        p_mat.append(jnp.dot(perm, p_ref[d, 0], preferred_element_type=F32).astype(BF16))
        q_mat.append(lax.dot_general(q_ref[d, 0], perm, nt, preferred_element_type=F32).astype(BF16))
    for d in range(2):
        s = jnp.dot(xb, p_mat[d], preferred_element_type=F32)
        s_ref[2 * d] = s[:, :hp]
        s_ref[2 * d + 1] = s[:, hp:]

    al = [al_ref[d, 0] for d in range(2)]
    alr = [a[:, :hp] for a in al]
    ali = [a[:, hp:] for a in al]

    def body(i, carry):
        new = []
        for d in range(2):
            re, im = carry[2 * d], carry[2 * d + 1]
            n = i if d == 0 else n_chunks - 1 - i
            rows = pl.ds(n, bsz, stride=n_chunks)
            xin_ref[2 * d, rows, :] = re
            xin_ref[2 * d + 1, rows, :] = im
            new.append(alr[d] * re - ali[d] * im + s_ref[2 * d, rows, :])
            new.append(alr[d] * im + ali[d] * re + s_ref[2 * d + 1, rows, :])
        return tuple(new)

    zero = jnp.zeros((bsz, hp), F32)
    lax.fori_loop(0, n_chunks, body, (zero, zero, zero, zero))

    xin = [jnp.concatenate([xin_ref[2 * d], xin_ref[2 * d + 1]], axis=-1).astype(BF16) for d in range(2)]
    y = x_ref[0] * d_ref[0]
    for d in range(2):
        y = y + jnp.dot(xb, m_mat[d], preferred_element_type=F32)
        y = y + jnp.dot(xin[d], q_mat[d], preferred_element_type=F32)
    y_ref[0] = y


S5_PAIR = 2 * S5_GROUP_SIZE
S5_PER_VREG = LANES // S5_PAIR


def _s5_proj_kernel(x_ref, w_ref, o_ref, tok_ref):
    c = S5_PAIR
    rows = o_ref.shape[1]
    res = jnp.dot(x_ref[...], w_ref[...], preferred_element_type=F32)
    for j in range(tok_ref.shape[0]):
        tok_ref[j] = res[:, j * LANES:(j + 1) * LANES]
    for j in range(tok_ref.shape[0]):
        for t in range(S5_CHUNK):
            blk = tok_ref[j, pl.ds(t, rows, stride=S5_CHUNK), :]
            for gg in range(S5_PER_VREG):
                o_ref[j * S5_PER_VREG + gg, :, t * c:(t + 1) * c] = blk[:, gg * c:(gg + 1) * c]


def _s5_unpack_kernel(y_ref, o_ref, tok_ref):
    c = S5_PAIR
    rows = y_ref.shape[1]
    for j in range(tok_ref.shape[0]):
        for t in range(S5_CHUNK):
            blk = jnp.concatenate([y_ref[j * S5_PER_VREG + gg, :, t * c:(t + 1) * c] for gg in range(S5_PER_VREG)],
                                  axis=-1)
            tok_ref[j, pl.ds(t, rows, stride=S5_CHUNK), :] = blk
    for j in range(tok_ref.shape[0]):
        o_ref[:, j * LANES:(j + 1) * LANES] = tok_ref[j]


def _s5_proj(xb, w_u, tm=256):
    t, d = xb.shape
    rows = t // S5_CHUNK
    lc = S5_CHUNK * S5_PAIR
    w = w_u.shape[1]
    return pl.pallas_call(
        _s5_proj_kernel, grid=(rows // tm,),
        in_specs=[pl.BlockSpec((tm * S5_CHUNK, d), lambda i: (i, 0)), pl.BlockSpec(w_u.shape, lambda i: (0, 0))],
        out_specs=pl.BlockSpec((S5_GROUPS // 2, tm, lc), lambda i: (0, i, 0)),
        out_shape=jax.ShapeDtypeStruct((S5_GROUPS // 2, rows, lc), F32),
        scratch_shapes=[pltpu.VMEM((w // LANES, tm * S5_CHUNK, LANES), F32)],
        compiler_params=_params(1), name="s5_proj")(xb, w_u)


def _s5(x, mats, d_skip, b, s):
    L, C, G = S5_CHUNK, S5_GROUP_SIZE, S5_GROUPS
    w = G * C
    n = s // L
    lc = L * S5_PAIR
    st = 4 * S5_STATE
    m_mat, p_pad, q_pad, a_l, perm = mats
    d_row = jnp.tile(d_skip.astype(F32).reshape(G // 2, 1, S5_PAIR), (1, L, 1)).reshape(G // 2, 1, lc)
    spec = lambda r, c_: pl.BlockSpec((2, 1, r, c_), lambda k: (0, k, 0, 0))
    kern = functools.partial(_s5_kernel, n_chunks=n, bsz=b)
    y = pl.pallas_call(
        kern, grid=(G // 2,),
        in_specs=[pl.BlockSpec((1, n * b, lc), lambda k: (k, 0, 0)),
                  pl.BlockSpec((2, 2, lc // 2, lc // 2), lambda k: (0, k, 0, 0)), spec(lc, st), spec(st, lc),
                  spec(1, st), pl.BlockSpec((1, 1, lc), lambda k: (k, 0, 0)),
                  pl.BlockSpec((lc, lc), lambda k: (0, 0))],
        out_specs=pl.BlockSpec((1, n * b, lc), lambda k: (k, 0, 0)),
        out_shape=jax.ShapeDtypeStruct((G // 2, n * b, lc), F32),
        scratch_shapes=[pltpu.VMEM((4, n * b, LANES), F32), pltpu.VMEM((4, n * b, LANES), F32)],
        compiler_params=_params(1), name="s5")(x, m_mat, p_pad, q_pad, a_l, d_row, perm)
    tm = 256
    return pl.pallas_call(
        _s5_unpack_kernel, grid=(b * n // tm,),
        in_specs=[pl.BlockSpec((G // 2, tm, lc), lambda i: (0, i, 0))],
        out_specs=pl.BlockSpec((tm * L, w), lambda i: (i, 0)),
        out_shape=jax.ShapeDtypeStruct((b * s, w), F32),
        scratch_shapes=[pltpu.VMEM((w // LANES, tm * L, LANES), F32)],
        compiler_params=_params(1), name="s5_unpack")(y)


def _gla_kernel(qf_ref, kf_ref, vf_ref, gf_ref, qb_ref, kb_ref, vb_ref, gb_ref, bg_ref,
                of_ref, ob_ref, st_ref, *, rows):
    c = GLA_CHUNK
    n_sub = rows // c

    @pl.when(pl.program_id(1) == 0)
    def _():
        st_ref[...] = jnp.zeros_like(st_ref)

    nt = (((1,), (1,)), ((), ()))
    tn = (((0,), (0,)), ((), ()))
    log2 = lambda n: int(math.log2(n))
    r_r = lax.broadcasted_iota(jnp.int32, (rows, rows), 0)
    c_r = lax.broadcasted_iota(jnp.int32, (rows, rows), 1)
    same_chunk = (r_r >> log2(c)) == (c_r >> log2(c))
    t_row = lax.broadcasted_iota(jnp.int32, (c, GLA_K), 0)
    t_col = lax.broadcasted_iota(jnp.int32, (c, GLA_K), 1) & (c - 1)
    k_lane = lax.broadcasted_iota(jnp.int32, (1, GLA_K), 1) >> log2(GLA_DK)
    v_lane = lax.broadcasted_iota(jnp.int32, (1, BRANCH_WIDTH), 1) >> log2(GLA_DV)
    st_same_head = ((lax.broadcasted_iota(jnp.int32, (BRANCH_WIDTH, GLA_K), 0) >> log2(GLA_DV))
                    == (lax.broadcasted_iota(jnp.int32, (BRANCH_WIDTH, GLA_K), 1) >> log2(GLA_DK)))

    for d, (q_ref, k_ref, v_ref, g_ref, o_ref) in enumerate(
            ((qf_ref, kf_ref, vf_ref, gf_ref, of_ref), (qb_ref, kb_ref, vb_ref, gb_ref, ob_ref))):
        gate = g_ref[0].astype(F32) + bg_ref[d]
        log_a = (jnp.minimum(gate, 0.0) - jnp.log(1.0 + jnp.exp(-jnp.abs(gate)))) * (1.0 / GLA_TAU)
        if d == 0:
            tri = jnp.where(same_chunk, jnp.where(r_r >= c_r, 1.0, 0.0), 0.0).astype(BF16)
            keep, i_ref, i_last = (t_row >= t_col), c // 2, c - 1
        else:
            tri = jnp.where(same_chunk, jnp.where(c_r >= r_r, 1.0, 0.0), 0.0).astype(BF16)
            keep, i_ref, i_last = (t_col > t_row), c - 1 - c // 2, 0
        la_hi = log_a.astype(BF16)
        la_lo = (log_a - la_hi.astype(F32)).astype(BF16)
        cum_all = (jnp.dot(tri, la_hi, preferred_element_type=F32) + jnp.dot(tri, la_lo, preferred_element_type=F32))
        q = q_ref[0].astype(F32) * (GLA_DK ** -0.5)
        k = k_ref[0].astype(F32)
        v = v_ref[0]
        for ci in (range(n_sub) if d == 0 else range(n_sub - 1, -1, -1)):
            sl = slice(ci * c, (ci + 1) * c)
            cum = cum_all[sl]
            mid = cum[i_ref:i_ref + 1]
            last = cum[i_last:i_last + 1]
            qc, kc, vc = q[sl], k[sl], v[sl]
            qe = (qc * jnp.exp(cum - mid)).astype(BF16)
            ke = kc * jnp.exp(mid - cum)
            kd = (kc * jnp.exp(last - cum)).astype(BF16)
            qs = (qc * jnp.exp(cum)).astype(BF16)
            decay = jnp.exp(last)
            ke4 = jnp.concatenate([jnp.where(k_lane == h, ke, 0.0) for h in range(GLA_HEADS)], axis=0).astype(BF16)
            v4 = jnp.concatenate([jnp.where(v_lane == h, vc, 0.0) for h in range(GLA_HEADS)], axis=0).astype(BF16)
            sc = lax.dot_general(qe, ke4, nt, preferred_element_type=F32)
            sc = jnp.where(keep, sc, 0.0).astype(BF16)
            st = st_ref[d]
            o = (jnp.dot(sc, v4, preferred_element_type=F32)
                 + lax.dot_general(qs, st.astype(BF16), nt, preferred_element_type=F32))
            o_ref[0, sl, :] = o
            ds = lax.dot_general(vc.astype(BF16), kd, tn, preferred_element_type=F32)
            for h in range(GLA_HEADS):
                rs = slice(h * GLA_DV, (h + 1) * GLA_DV)
                ls = slice(h * GLA_DK // LANES * LANES, h * GLA_DK // LANES * LANES + LANES)
                st_ref[d, rs, ls] = decay[:, ls] * st[rs, ls] + jnp.where(st_same_head[rs, ls], ds[rs, ls], 0.0)


def _gla(proj_a3, proj_b3, b_gate, rows=128):
    b, s, _ = proj_a3.shape
    n = s // rows
    bg = b_gate.astype(F32).reshape(2, 1, GLA_K)

    def specs(rev):
        blk = (lambda i: n - 1 - i) if rev else (lambda i: i)
        return [pl.BlockSpec((1, rows, GLA_K), lambda bi, i: (bi, blk(i), A_QC // GLA_K)),
                pl.BlockSpec((1, rows, GLA_K), lambda bi, i: (bi, blk(i), A_KC // GLA_K)),
                pl.BlockSpec((1, rows, BRANCH_WIDTH), lambda bi, i: (bi, blk(i), A_VC // BRANCH_WIDTH)),
                pl.BlockSpec((1, rows, GLA_K), lambda bi, i: (bi, blk(i), GATE_COLS // GLA_K + int(rev)))]

    out_f = pl.BlockSpec((1, rows, BRANCH_WIDTH), lambda bi, i: (bi, i, 0))
    out_b = pl.BlockSpec((1, rows, BRANCH_WIDTH), lambda bi, i: (bi, n - 1 - i, 0))
    shape = jax.ShapeDtypeStruct((b, s, BRANCH_WIDTH), F32)
    return pl.pallas_call(
        functools.partial(_gla_kernel, rows=rows), grid=(b, n),
        in_specs=specs(False) + specs(True) + [pl.BlockSpec((2, 1, GLA_K), lambda bi, i: (0, 0, 0))],
        out_specs=[out_f, out_b], out_shape=[shape, shape],
        scratch_shapes=[pltpu.VMEM((2, BRANCH_WIDTH, GLA_K), F32)],
        compiler_params=_params(2), name="gla")(
            proj_a3, proj_a3, proj_a3, proj_b3, proj_a3, proj_a3, proj_a3, proj_b3, bg)


def _ranks(vals):
    ranks = []
    for i, v in enumerate(vals):
        r = jnp.zeros(v.shape, F32)
        for o, other in enumerate(vals):
            if o != i:
                r = r + (jnp.where(other >= v, 1.0, 0.0) if o < i else jnp.where(other > v, 1.0, 0.0))
        ranks.append(r)
    return ranks


def _route(scores_t, bias_col):
    gs = EXPERTS_PER_GROUP
    s = [scores_t[e:e + 1] for e in range(N_EXPERTS)]
    biased = [s[e] + bias_col[e:e + 1] for e in range(N_EXPERTS)]
    top2, group_score = [], []
    for g in range(N_EXPERTS // gs):
        rk = _ranks(biased[g * gs:(g + 1) * gs])
        top2 += [r < 2.0 for r in rk]
        kept = [jnp.where(top2[g * gs + j], biased[g * gs + j], 0.0) for j in range(gs)]
        group_score.append(functools.reduce(lambda a, b: a + b, kept))
    best = [r < 1.0 for r in _ranks(group_score)]
    w = [jnp.where(top2[e], jnp.where(best[e // gs], s[e], 0.0), 0.0) for e in range(N_EXPERTS)]
    total = functools.reduce(lambda a, b: a + b, w)
    return [we / total for we in w]


def _merge_kernel(x_ref, oa_ref, ys_ref, gf_ref, gb_ref, r_ref, gl0_ref, gl1_ref, gl2_ref,
                  wglu_ref, bglu_ref, gng_ref, wup_ref, mb_ref, wout_ref, lng_ref, lnb_ref, rw_ref, rb_ref,
                  xo_ref, xob_ref, comb_ref):
    y = ys_ref[...]
    gelu = 0.5 * y * (1.0 + jnp.tanh(math.sqrt(2.0 / math.pi) * (y + 0.044715 * (y * y * y))))
    o_b = gelu * _sigmoid(jnp.dot(gelu.astype(BF16), wglu_ref[...], preferred_element_type=F32) + bglu_ref[...])

    r = r_ref[...].astype(F32)
    gated = r * _sigmoid(r)
    heads = []
    for h in range(GLA_HEADS):
        sl = slice(h * GLA_DV, (h + 1) * GLA_DV)
        o = gf_ref[:, sl] + gb_ref[:, sl]
        ms = jnp.mean(o * o, axis=-1, keepdims=True)
        heads.append(o * lax.rsqrt(ms + NORM_EPS) * gng_ref[...] * gated[:, sl])
    o_c = jnp.concatenate(heads, axis=-1)

    merged = None
    for n, (o_n, gl_ref) in enumerate(((oa_ref[...], gl0_ref), (o_b, gl1_ref), (o_c, gl2_ref))):
        gate = _sigmoid(gl_ref[...].astype(F32) + mb_ref[n])
        term = gate * jnp.dot(o_n.astype(BF16), wup_ref[n], preferred_element_type=F32)
        merged = term if merged is None else merged + term
    mix = jnp.dot(merged.astype(BF16), wout_ref[...], preferred_element_type=F32)
    x1 = _layer_norm(DN_ALPHA * x_ref[...] + mix, lng_ref[...], lnb_ref[...])
    xo_ref[...] = x1
    x_hi = x1.astype(BF16)
    xob_ref[...] = x_hi

    x_lo = (x1 - x_hi.astype(F32)).astype(BF16)
    logits = (jnp.dot(x_hi, rw_ref[0], preferred_element_type=F32) + jnp.dot(x_lo, rw_ref[0], preferred_element_type=F32)
              + jnp.dot(x_hi, rw_ref[1], preferred_element_type=F32))
    rows = _route(_sigmoid(logits.T[:N_EXPERTS]), rb_ref[...])
    comb_ref[...] = jnp.zeros(comb_ref.shape, F32)
    for e in range(N_EXPERTS):
        comb_ref[e:e + 1, :] = rows[e]


def _merge(x, o_a, y_s5, g_f, g_b, proj_a, proj_b, wglu, bglu, gng, wup, mb, wout, lng, lnb, rw, rb, tm=256):
    t, d = x.shape
    w = BRANCH_WIDTH
    row = lambda width, cb=0: pl.BlockSpec((tm, width), lambda i: (i, cb))
    full = lambda a: pl.BlockSpec(a.shape, lambda i: (0,) * a.ndim)
    weights = (wglu, bglu, gng, wup, mb, wout, lng, lnb, rw, rb)
    return pl.pallas_call(
        _merge_kernel, grid=(t // tm,),
        in_specs=[row(d), row(w), row(w), row(w), row(w), row(w, A_RC // w), row(d, 0), row(d, 1), row(d, 2)]
        + [full(a) for a in weights],
        out_specs=[row(d), row(d), pl.BlockSpec((LANES, tm), lambda i: (0, i))],
        out_shape=[jax.ShapeDtypeStruct((t, d), F32), jax.ShapeDtypeStruct((t, d), BF16),
                   jax.ShapeDtypeStruct((LANES, t), F32)],
        compiler_params=_params(1), name="merge")(
            x, o_a, y_s5, g_f, g_b, proj_a, proj_b, proj_b, proj_b, *weights)


MOE_SUB = 256
MOE_CAP = 64


def _swiglu(x, wg_ref, wu_ref, wd_ref, row_scale):
    g = jnp.dot(x, wg_ref[0], preferred_element_type=F32)
    u = jnp.dot(x, wu_ref[0], preferred_element_type=F32)
    h = (g * _sigmoid(g)) * u * row_scale
    return jnp.dot(h.astype(BF16), wd_ref[0], preferred_element_type=F32)


def _moe_kernel(xb_ref, xf_ref, comb_ref, wg_ref, wu_ref, wd_ref, lng_ref, lnb_ref, of_ref, ob_ref,
                pos_ref, ind_ref, sel_ref, xg_ref, y_ref, cg_ref, acc_ref, flag_ref, *, tm, cap):
    e = pl.program_id(1)
    ns = tm // MOE_SUB
    gs = EXPERTS_PER_GROUP
    sub = lambda k: slice(k * MOE_SUB, (k + 1) * MOE_SUB)
    slots = lambda k: slice(k * cap, (k + 1) * cap)
    r_i = lax.broadcasted_iota(jnp.int32, (MOE_SUB, MOE_SUB), 0)
    c_i = lax.broadcasted_iota(jnp.int32, (MOE_SUB, MOE_SUB), 1)

    @pl.when(e == 0)
    def _():
        ind = jnp.where(comb_ref[...] > 0.0, 1.0, 0.0)
        ind_ref[...] = ind
        before = jnp.where(r_i < c_i, 1.0, 0.0).astype(BF16)
        most = jnp.zeros((LANES, 1), F32)
        for k in range(ns):
            pos_ref[:, sub(k)] = jnp.dot(ind[:, sub(k)].astype(BF16), before, preferred_element_type=F32)
            most = jnp.maximum(most, jnp.sum(ind[:, sub(k)], axis=-1, keepdims=True))
        flag_ref[0] = (jnp.max(most) > cap).astype(jnp.int32)
        acc_ref[...] = jnp.zeros_like(acc_ref)

    slot_id = lax.broadcasted_iota(jnp.int32, (cap, 1), 0).astype(F32)
    for grp in range(N_EXPERTS // gs):
        @pl.when(e == grp * gs)
        def _(grp=grp):
            for k in range(ns):
                pieces = []
                for j in range(gs):
                    ee = grp * gs + j
                    one_hot = jnp.where(pos_ref[ee:ee + 1, sub(k)] == slot_id, ind_ref[ee:ee + 1, sub(k)], 0.0)
                    cg_ref[j, slots(k), :] = jnp.sum(one_hot * comb_ref[ee:ee + 1, sub(k)], axis=-1, keepdims=True)
                    pieces.append(one_hot)
                sel = jnp.concatenate(pieces, axis=0).astype(BF16)
                sel_ref[k] = sel
                xg = jnp.dot(sel, xb_ref[sub(k), :], preferred_element_type=F32).astype(BF16)
                for j in range(gs):
                    xg_ref[j, slots(k), :] = xg[j * cap:(j + 1) * cap]

    j = e % gs
    y_ref[j] = _swiglu(xg_ref[j], wg_ref, wu_ref, wd_ref, cg_ref[j]).astype(BF16)

    @pl.when(j == gs - 1)
    def _():
        for k in range(ns):
            yk = jnp.concatenate([y_ref[jj, slots(k), :] for jj in range(gs)], axis=0)
            acc_ref[sub(k), :] += lax.dot_general(sel_ref[k], yk, (((0,), (0,)), ((), ())),
                                                  preferred_element_type=F32)

    @pl.when(flag_ref[0] > 0)
    def _():
        comb = comb_ref[...].T
        earlier = jnp.where(r_i > c_i, 1.0, 0.0).astype(BF16)
        over = []
        for k in range(ns):
            pos = jnp.dot(earlier, jnp.where(comb[sub(k)] > 0.0, 1.0, 0.0).astype(BF16), preferred_element_type=F32)
            over.append(jnp.where(pos >= cap, comb[sub(k)], 0.0))
        over = jnp.concatenate(over, axis=0)
        lane = lax.broadcasted_iota(jnp.int32, over.shape, 1)
        ce = jnp.sum(jnp.where(lane == e, over, 0.0), axis=-1, keepdims=True)
        acc_ref[...] += _swiglu(xb_ref[...], wg_ref, wu_ref, wd_ref, ce)

    @pl.when(e == N_EXPERTS - 1)
    def _():
        y = _layer_norm(DN_ALPHA * xf_ref[...] + acc_ref[...], lng_ref[...], lnb_ref[...])
        of_ref[...] = y
        ob_ref[...] = y.astype(BF16)


def _moe(xb, xf, comb, wg, wu, wd, lng, lnb, tm=1024, cap=MOE_CAP):
    t, d = xf.shape
    ns = tm // MOE_SUB
    gs = EXPERTS_PER_GROUP
    row = lambda width: pl.BlockSpec((tm, width), lambda i, e: (i, 0))
    vec = pl.BlockSpec((1, d), lambda i, e: (0, 0))
    return pl.pallas_call(
        functools.partial(_moe_kernel, tm=tm, cap=cap), grid=(t // tm, N_EXPERTS),
        in_specs=[row(d), row(d), pl.BlockSpec((LANES, tm), lambda i, e: (0, i)),
                  pl.BlockSpec((1, d, EXPERT_FF), lambda i, e: (e, 0, 0)),
                  pl.BlockSpec((1, d, EXPERT_FF), lambda i, e: (e, 0, 0)),
                  pl.BlockSpec((1, EXPERT_FF, d), lambda i, e: (e, 0, 0)), vec, vec],
        out_specs=[row(d), row(d)],
        out_shape=[jax.ShapeDtypeStruct((t, d), F32), jax.ShapeDtypeStruct((t, d), BF16)],
        scratch_shapes=[pltpu.VMEM((LANES, tm), F32), pltpu.VMEM((LANES, tm), F32),
                        pltpu.VMEM((ns, gs * cap, MOE_SUB), BF16), pltpu.VMEM((gs, ns * cap, d), BF16),
                        pltpu.VMEM((gs, ns * cap, d), BF16), pltpu.VMEM((gs, ns * cap, 1), F32),
                        pltpu.VMEM((tm, d), F32), pltpu.SMEM((1,), jnp.int32)],
        compiler_params=_params(2), name="moe")(xb, xf, comb, wg, wu, wd, lng, lnb)


def _proj_b_weights(w_in_l, w_gate):
    z0 = W_IN_Z
    folded = [jnp.dot(w_in_l[:, z0 + i * GLA_GATE_RANK:z0 + (i + 1) * GLA_GATE_RANK].astype(F32),
                      w_gate[i].astype(F32), precision=HIGHEST) for i in range(2)]
    return jnp.concatenate([w_in_l[:, z0 + 2 * GLA_GATE_RANK:].astype(F32)] + folded, axis=1).astype(BF16)


def kernel(x, ln0_g, ln0_b, w_in, da_lambda, da_norm_g, s5_a_re, s5_a_im, s5_log_dt, s5_b_re, s5_b_im,
           s5_c_re, s5_c_im, s5_d, s5_w_glu, s5_b_glu, gla_w_gate, gla_b_gate, gla_norm_g, merge_w_up,
           merge_b, w_out, ln1_g, ln1_b, router_w, router_bias, moe_w_gate, moe_w_up, moe_w_down,
           ln2_g, ln2_b):
    b, s, d = x.shape
    t = b * s
    vec = lambda a: a.astype(F32).reshape(1, -1)
    xf, xb = _ln0(x.reshape(t, d), ln0_g.astype(F32), ln0_b.astype(F32))
    rw32 = jnp.zeros((d, LANES), F32).at[:, :N_EXPERTS].set(router_w.astype(F32))
    rw_hi = rw32.astype(BF16)
    rw = jnp.stack([rw_hi, (rw32 - rw_hi.astype(F32)).astype(BF16)])
    rb = router_bias.astype(F32).reshape(N_EXPERTS, 1)
    for l in range(DEPTH):
        w_a = jnp.concatenate([w_in[l][:, :W_IN_U], w_in[l][:, W_IN_GLA:W_IN_Z]], axis=1).astype(BF16)
        w_u = w_in[l][:, W_IN_U:W_IN_GLA].astype(BF16)
        w_b = _proj_b_weights(w_in[l], gla_w_gate[l])
        proj_a = _matmul(xb, w_a, 1024, PROJ_A // 2, "in_proj_a")
        proj_b = _matmul(xb, w_b, 512, PROJ_B, "in_proj_b")
        proj_a3 = proj_a.reshape(b, s, PROJ_A)
        proj_b3 = proj_b.reshape(b, s, PROJ_B)

        lam_init = 0.8 - 0.6 * math.exp(-0.3 * l)
        o_a = _diff_attention(proj_a3, da_lambda[l].astype(F32), da_norm_g[l].astype(F32), lam_init)
        mats = _s5_matrices(s5_a_re[l], s5_a_im[l], s5_log_dt[l], s5_b_re[l], s5_b_im[l], s5_c_re[l], s5_c_im[l])
        y_s5 = _s5(_s5_proj(xb, w_u), mats, s5_d[l], b, s)
        g_f, g_b = _gla(proj_a3, proj_b3, gla_b_gate[l])

        xf, xb, comb = _merge(
            xf, o_a.reshape(t, BRANCH_WIDTH), y_s5, g_f.reshape(t, BRANCH_WIDTH), g_b.reshape(t, BRANCH_WIDTH),
            proj_a, proj_b, s5_w_glu[l].astype(BF16), vec(s5_b_glu[l]), vec(gla_norm_g[l]),
            merge_w_up[l].astype(BF16), merge_b[l].astype(F32).reshape(N_BRANCH, 1, d), w_out[l].astype(BF16),
            vec(ln1_g[l]), vec(ln1_b[l]), rw, rb)
        xf, xb = _moe(xb, xf, comb, moe_w_gate[l].astype(BF16), moe_w_up[l].astype(BF16),
                      moe_w_down[l].astype(BF16), vec(ln2_g[l]), vec(ln2_b[l]))
    return xf.reshape(b, s, d)
```

```python
import functools
import math

import jax
import jax.numpy as jnp
from jax import lax
from jax.experimental import pallas as pl
from jax.experimental.pallas import tpu as pltpu

F32 = jnp.float32
BF16 = jnp.bfloat16
HIGHEST = lax.Precision.HIGHEST

D_MODEL = 1024
DEPTH = 2
BRANCH_WIDTH = D_MODEL // 2
N_BRANCH = 3

DA_HEADS = 4
DA_HEAD_DIM = BRANCH_WIDTH // (2 * DA_HEADS)
DA_V_DIM = 2 * DA_HEAD_DIM
ALIBI_MAX_EXP = 8.0

S5_GROUP_SIZE = 16
S5_GROUPS = BRANCH_WIDTH // S5_GROUP_SIZE
S5_STATE = 64
S5_CHUNK = 16

GLA_HEADS = 4
GLA_DV = BRANCH_WIDTH // GLA_HEADS
GLA_DK = GLA_DV // 2
GLA_K = GLA_HEADS * GLA_DK
GLA_GATE_RANK = 16
GLA_TAU = 16.0
GLA_CHUNK = 64

N_EXPERTS = 16
EXPERTS_PER_GROUP = 4
EXPERT_FF = D_MODEL // 2

DN_ALPHA = (2.0 * DEPTH) ** 0.25
NORM_EPS = 1e-5
LOG2E = math.log2(math.e)

LANES = 128
W_IN_U = 3 * BRANCH_WIDTH
W_IN_GLA = W_IN_U + BRANCH_WIDTH
W_IN_Z = W_IN_GLA + 2 * GLA_K + 2 * BRANCH_WIDTH
A_QC = W_IN_U
A_KC = A_QC + GLA_K
A_VC = A_KC + GLA_K
A_RC = A_VC + BRANCH_WIDTH
PROJ_A = A_RC + BRANCH_WIDTH
GATE_COLS = N_BRANCH * D_MODEL
PROJ_B = GATE_COLS + 2 * GLA_K
VMEM_LIMIT = 56 * 1024 * 1024


def _params(n_grid_dims):
    return pltpu.CompilerParams(dimension_semantics=("arbitrary",) * n_grid_dims,
                                vmem_limit_bytes=VMEM_LIMIT)


def _layer_norm(xf, g, b):
    mu = jnp.mean(xf, axis=-1, keepdims=True)
    xc = xf - mu
    var = jnp.mean(xc * xc, axis=-1, keepdims=True)
    return xc * lax.rsqrt(var + NORM_EPS) * g + b


def _sigmoid(x):
    return 1.0 / (1.0 + jnp.exp(-x))


def _ln0_kernel(x_ref, g_ref, b_ref, of_ref, ob_ref):
    y = _layer_norm(x_ref[...], g_ref[...], b_ref[...])
    of_ref[...] = y
    ob_ref[...] = y.astype(BF16)


def _ln0(x2, g, b, tm=512):
    t, d = x2.shape
    row = pl.BlockSpec((tm, d), lambda i: (i, 0))
    vec = pl.BlockSpec((1, d), lambda i: (0, 0))
    return pl.pallas_call(
        _ln0_kernel, grid=(t // tm,), in_specs=[row, vec, vec], out_specs=[row, row],
        out_shape=[jax.ShapeDtypeStruct((t, d), F32), jax.ShapeDtypeStruct((t, d), BF16)],
        compiler_params=_params(1), name="ln0")(x2, g.reshape(1, d), b.reshape(1, d))


def _mm_kernel(x_ref, w_ref, o_ref):
    o_ref[...] = jnp.dot(x_ref[...], w_ref[...], preferred_element_type=F32).astype(o_ref.dtype)


def _matmul(x, w, tm, tn, name):
    t, k = x.shape
    n = w.shape[1]
    return pl.pallas_call(
        _mm_kernel, grid=(t // tm, n // tn),
        in_specs=[pl.BlockSpec((tm, k), lambda i, j: (i, 0)), pl.BlockSpec((k, tn), lambda i, j: (0, j))],
        out_specs=pl.BlockSpec((tm, tn), lambda i, j: (i, j)),
        out_shape=jax.ShapeDtypeStruct((t, n), BF16),
        compiler_params=_params(2), name=name)(x, w)


def _split3(x):
    hi = x.astype(BF16).astype(F32)
    mid = (x - hi).astype(BF16).astype(F32)
    lo = (x - hi - mid).astype(BF16).astype(F32)
    return hi, mid, lo


def _attn_kernel(lam_ref, g_ref, q_ref, k_ref, v_ref, o_ref, ka_ref, vb_ref, lhs_ref, fix_ref, m_ref, acc_ref,
                 *, tq, tk, rc, seq, lam_init):
    h = pl.program_id(1)
    qi = pl.program_id(2)
    dh = DA_HEAD_DIM

    slopes = [2.0 ** (-ALIBI_MAX_EXP * (i + 1) / DA_HEADS) for i in range(DA_HEADS)]
    slope = jnp.float32(slopes[-1])
    for i in range(DA_HEADS - 1):
        slope = jnp.where(h == i, jnp.float32(slopes[i]), slope)
    sigma = slope * LOG2E

    @pl.when(qi == 0)
    def _():
        ka_ref[:LANES, :] = k_ref[0].astype(F32).T.astype(BF16)
        t_hi, t_mid, t_lo = _split3(lax.broadcasted_iota(jnp.int32, (1, seq), 1).astype(F32) * sigma)
        sub_k = lax.broadcasted_iota(jnp.int32, (LANES, seq), 0)
        aug_k = jnp.where(sub_k < 3, 1.0, jnp.where(sub_k == 3, t_hi, jnp.where(sub_k == 4, t_mid,
                          jnp.where(sub_k == 5, t_lo, 0.0))))
        ka_ref[LANES:, :] = aug_k.astype(BF16)
        lane_v = lax.broadcasted_iota(jnp.int32, (seq, LANES), 1)
        vb_ref[:, :LANES] = v_ref[0].astype(BF16)
        vb_ref[:, LANES:] = jnp.where(lane_v == 0, 1.0, 0.0).astype(BF16)
        for r in range(tk // rc):
            delta = (r * rc + lax.broadcasted_iota(jnp.int32, (rc, 1), 0)
                     - lax.broadcasted_iota(jnp.int32, (1, tk), 1)).astype(F32)
            fix_ref[r] = (2.0 * sigma) * jnp.minimum(delta, 0.0)

    q = q_ref[0].astype(F32) * (dh ** -0.5 * LOG2E)
    lane = lax.broadcasted_iota(jnp.int32, (tq, LANES), 1)
    row = lax.broadcasted_iota(jnp.int32, (tq, 1), 0)
    u_hi, u_mid, u_lo = _split3((qi * tq + row).astype(F32) * (-sigma))
    aug_q = jnp.where(lane == 0, u_hi, jnp.where(lane == 1, u_mid, jnp.where(lane == 2, u_lo,
                      jnp.where(lane < 6, 1.0, 0.0))))
    for mp in range(2):
        qz = jnp.where(lane < dh, q, 0.0) if mp == 0 else jnp.where(lane >= dh, q, 0.0)
        lhs_ref[mp, 0] = jnp.concatenate([qz, aug_q], axis=-1).astype(BF16)
        lhs_ref[mp, 1] = jnp.concatenate([qz, -aug_q], axis=-1).astype(BF16)
    m_ref[...] = jnp.full(m_ref.shape, -1e30, F32)
    acc_ref[...] = jnp.zeros(acc_ref.shape, F32)

    def block(j, local):
        off = pl.multiple_of(j * tk, tk)
        ka = ka_ref[:, pl.ds(off, tk)]
        vb = vb_ref[pl.ds(off, tk), :]
        for mp in range(2):
            for c in range(tq // rc):
                rows = slice(c * rc, (c + 1) * rc)
                if isinstance(local, str):
                    sign, fixup = (0 if local == "before" else 1), None
                elif (c + 1) * rc <= local * tk:
                    sign, fixup = 1, None
                elif c * rc >= (local + 1) * tk:
                    sign, fixup = 0, None
                else:
                    sign, fixup = 0, fix_ref[(c * rc - local * tk) // rc]
                s = jnp.dot(lhs_ref[mp, sign, rows, :], ka, preferred_element_type=F32)
                if fixup is not None:
                    s = s + fixup
                m_old = m_ref[mp, rows, :]
                m_new = jnp.maximum(m_old, jnp.max(s, axis=-1, keepdims=True))
                alpha = jnp.exp2(m_old - m_new)
                p = jnp.exp2(s - m_new).astype(BF16)
                acc_ref[mp, rows, :] = alpha * acc_ref[mp, rows, :] + jnp.dot(p, vb, preferred_element_type=F32)
                m_ref[mp, rows, :] = m_new

    def run(lo, hi, where):
        def body(j, carry):
            block(j, where)
            return carry
        lax.fori_loop(lo, hi, body, 0)

    j0 = (qi * tq) // tk
    run(0, j0, "before")
    for local in range(tq // tk):
        block(j0 + local, local)
    run(j0 + tq // tk, seq // tk, "after")

    lv = lam_ref[...]
    lam = (jnp.exp(jnp.sum(lv[0:1] * lv[1:2], axis=-1, keepdims=True))
           - jnp.exp(jnp.sum(lv[2:3] * lv[3:4], axis=-1, keepdims=True)) + lam_init)
    o1 = acc_ref[0, :, :LANES] / acc_ref[0, :, LANES:LANES + 1]
    o2 = acc_ref[1, :, :LANES] / acc_ref[1, :, LANES:LANES + 1]
    o = o1 - lam * o2
    ms = jnp.mean(o * o, axis=-1, keepdims=True)
    o_ref[0] = o * lax.rsqrt(ms + NORM_EPS) * g_ref[...] * (1.0 - lam_init)


def _diff_attention(proj_a3, da_lambda, da_norm_g, lam_init, tq=2048, tk=1024, rc=256):
    b, s, _ = proj_a3.shape
    assert tq % tk == 0 and s % tq == 0 and tk % rc == 0
    kern = functools.partial(_attn_kernel, tq=tq, tk=tk, rc=rc, seq=s, lam_init=lam_init)
    return pl.pallas_call(
        kern, grid=(b, DA_HEADS, s // tq),
        in_specs=[
            pl.BlockSpec((4, DA_HEAD_DIM), lambda bi, h, i: (0, 0)),
            pl.BlockSpec((1, DA_V_DIM), lambda bi, h, i: (0, 0)),
            pl.BlockSpec((1, tq, LANES), lambda bi, h, i: (bi, i, h)),
            pl.BlockSpec((1, s, LANES), lambda bi, h, i: (bi, 0, DA_HEADS + h)),
            pl.BlockSpec((1, s, LANES), lambda bi, h, i: (bi, 0, 2 * DA_HEADS + h)),
        ],
        out_specs=pl.BlockSpec((1, tq, LANES), lambda bi, h, i: (bi, i, h)),
        out_shape=jax.ShapeDtypeStruct((b, s, BRANCH_WIDTH), F32),
        scratch_shapes=[pltpu.VMEM((2 * LANES, s), BF16), pltpu.VMEM((s, 2 * LANES), BF16),
                        pltpu.VMEM((2, 2, tq, 2 * LANES), BF16), pltpu.VMEM((tk // rc, rc, tk), F32),
                        pltpu.VMEM((2, tq, 1), F32), pltpu.VMEM((2, tq, 2 * LANES), F32)],
        compiler_params=_params(3), name="diff_attention")(
            da_lambda, da_norm_g.reshape(1, DA_V_DIM), proj_a3, proj_a3, proj_a3)


def _s5_matrices(a_re, a_im, log_dt, b_re, b_im, c_re, c_im):
    L, C, P, G = S5_CHUNK, S5_GROUP_SIZE, S5_STATE, S5_GROUPS
    f = lambda t: t.astype(F32)
    a_re, a_im, b_re, b_im, c_re, c_im = map(f, (a_re, a_im, b_re, b_im, c_re, c_im))
    dt = jnp.exp(f(log_dt))[..., None]
    lr, li = a_re * dt, a_im * dt

    def apow(j):
        jj = j.astype(F32)[:, None, None, None]
        mag = jnp.exp(lr[None] * jj)
        return mag * jnp.cos(li[None] * jj), mag * jnp.sin(li[None] * jj)

    a1r, a1i = apow(jnp.arange(1, 2))
    nr, ni = a1r[0] - 1.0, a1i[0]
    den = a_re * a_re + a_im * a_im
    cr, ci = (nr * a_re + ni * a_im) / den, (ni * a_re - nr * a_im) / den
    bbr = cr[..., None] * b_re - ci[..., None] * b_im
    bbi = cr[..., None] * b_im + ci[..., None] * b_re

    pr, pi = apow(jnp.arange(L))
    wr = pr[..., None] * bbr[None] - pi[..., None] * bbi[None]
    wi = pr[..., None] * bbi[None] + pi[..., None] * bbr[None]
    kern = (jnp.einsum('dgcp,jdgpe->jdgce', c_re, wr, precision=HIGHEST)
            - jnp.einsum('dgcp,jdgpe->jdgce', c_im, wi, precision=HIGHEST))

    s_idx = jnp.arange(L)[:, None]
    t_idx = jnp.arange(L)[None, :]

    def toeplitz(kd, lag, valid):
        m = kd[jnp.clip(lag, 0, L - 1)] * valid[:, :, None, None, None].astype(F32)
        return m.transpose(2, 0, 4, 1, 3).reshape(G, L * C, L * C)

    m_grp = jnp.stack([toeplitz(kern[:, 0], t_idx - s_idx, t_idx >= s_idx),
                       toeplitz(kern[:, 1], s_idx - t_idx, s_idx >= t_idx)]).astype(BF16)

    def summary(w_dir):
        return w_dir.transpose(1, 0, 3, 2).reshape(G, L * C, P)

    p_re = jnp.stack([summary(wr[::-1, 0]), summary(wr[:, 1])])
    p_im = jnp.stack([summary(wi[::-1, 0]), summary(wi[:, 1])])

    qr, qi = apow(jnp.arange(1, L + 1))
    car = c_re[None] * qr[:, :, :, None, :] - c_im[None] * qi[:, :, :, None, :]
    cai = c_re[None] * qi[:, :, :, None, :] + c_im[None] * qr[:, :, :, None, :]

    def readout(ca_dir):
        return ca_dir.transpose(1, 3, 0, 2).reshape(G, P, L * C)

    q_re = jnp.stack([readout(car[:, 0]), readout(car[::-1, 1])])
    q_im = jnp.stack([-readout(cai[:, 0]), -readout(cai[::-1, 1])])

    half = jax.nn.one_hot(jnp.arange(G) % 2, 2, dtype=F32)
    p_pad = jnp.stack([p_re, p_im], axis=-2)[:, :, :, :, None, :] * half[None, :, None, None, :, None]
    p_grp = p_pad.reshape(2, G, L * C, 4 * P).astype(BF16)
    q_grp = jnp.stack([q_re, q_im], axis=2)[:, :, :, None] * half[None, :, None, :, None, None]
    q_grp = q_grp.reshape(2, G, 4 * P, L * C).astype(BF16)

    lc = L * C
    i_new = jnp.arange(2 * lc)
    i_old = (i_new // C % 2) * lc + (i_new // (2 * C)) * C + i_new % C
    perm = (i_old[:, None] == i_new[None, :]).astype(BF16)
    m_bd = m_grp
    p_stack = p_grp.reshape(2, G // 2, 2 * lc, 4 * P)
    q_stack = q_grp.reshape(2, G // 2, 2, 4 * P, lc).transpose(0, 1, 3, 2, 4).reshape(2, G // 2, 4 * P, 2 * lc)

    alr, ali = apow(jnp.arange(L, L + 1))
    a_l = jnp.concatenate([alr[0].reshape(2, G // 2, 2 * P), ali[0].reshape(2, G // 2, 2 * P)], axis=-1)
    return m_bd, p_stack, q_stack, a_l.reshape(2, G // 2, 1, 4 * P), perm


def _s5_kernel(x_ref, m_ref, p_ref, q_ref, al_ref, d_ref, perm_ref, y_ref, s_ref, xin_ref, *, n_chunks, bsz):
    hp = 2 * S5_STATE
    xb = x_ref[0].astype(BF16)
    perm = perm_ref[...]
    nt = (((1,), (1,)), ((), ()))
    m_mat, p_mat, q_mat = [], [], []
    zero = jnp.zeros(m_ref.shape[2:], BF16)
    for d in range(2):
        m_bd = jnp.concatenate([jnp.concatenate([m_ref[d, 0], zero], axis=-1),
                                jnp.concatenate([zero, m_ref[d, 1]], axis=-1)], axis=0)
        cols = lax.dot_general(m_bd, perm, nt, preferred_element_type=F32).astype(BF16)
        m_mat.append(jnp.dot(perm, cols, preferred_element_type=F32).astype(BF16))
        p_mat.append(jnp.dot(perm, p_ref[d, 0], preferred_element_type=F32).astype(BF16))
        q_mat.append(lax.dot_general(q_ref[d, 0], perm, nt, preferred_element_type=F32).astype(BF16))
    for d in range(2):
        s = jnp.dot(xb, p_mat[d], preferred_element_type=F32)
        s_ref[2 * d] = s[:, :hp]
        s_ref[2 * d + 1] = s[:, hp:]

    al = [al_ref[d, 0] for d in range(2)]
    alr = [a[:, :hp] for a in al]
    ali = [a[:, hp:] for a in al]

    def body(i, carry):
        new = []
        for d in range(2):
            re, im = carry[2 * d], carry[2 * d + 1]
            n = i if d == 0 else n_chunks - 1 - i
            rows = pl.ds(n, bsz, stride=n_chunks)
            xin_ref[2 * d, rows, :] = re
            xin_ref[2 * d + 1, rows, :] = im
            new.append(alr[d] * re - ali[d] * im + s_ref[2 * d, rows, :])
            new.append(alr[d] * im + ali[d] * re + s_ref[2 * d + 1, rows, :])
        return tuple(new)

    zero = jnp.zeros((bsz, hp), F32)
    lax.fori_loop(0, n_chunks, body, (zero, zero, zero, zero))

    xin = [jnp.concatenate([xin_ref[2 * d], xin_ref[2 * d + 1]], axis=-1).astype(BF16) for d in range(2)]
    y = x_ref[0] * d_ref[0]
    for d in range(2):
        y = y + jnp.dot(xb, m_mat[d], preferred_element_type=F32)
        y = y + jnp.dot(xin[d], q_mat[d], preferred_element_type=F32)
    y_ref[0] = y


S5_PAIR = 2 * S5_GROUP_SIZE
S5_PER_VREG = LANES // S5_PAIR


def _s5_proj_kernel(x_ref, w_ref, o_ref, tok_ref):
    c = S5_PAIR
    rows = o_ref.shape[1]
    res = jnp.dot(x_ref[...], w_ref[...], preferred_element_type=F32)
    for j in range(tok_ref.shape[0]):
        tok_ref[j] = res[:, j * LANES:(j + 1) * LANES]
    for j in range(tok_ref.shape[0]):
        for t in range(S5_CHUNK):
            blk = tok_ref[j, pl.ds(t, rows, stride=S5_CHUNK), :]
            for gg in range(S5_PER_VREG):
                o_ref[j * S5_PER_VREG + gg, :, t * c:(t + 1) * c] = blk[:, gg * c:(gg + 1) * c]


def _s5_unpack_kernel(y_ref, o_ref, tok_ref):
    c = S5_PAIR
    rows = y_ref.shape[1]
    for j in range(tok_ref.shape[0]):
        for t in range(S5_CHUNK):
            blk = jnp.concatenate([y_ref[j * S5_PER_VREG + gg, :, t * c:(t + 1) * c] for gg in range(S5_PER_VREG)],
                                  axis=-1)
            tok_ref[j, pl.ds(t, rows, stride=S5_CHUNK), :] = blk
    for j in range(tok_ref.shape[0]):
        o_ref[:, j * LANES:(j + 1) * LANES] = tok_ref[j]


def _s5_proj(xb, w_u, tm=256):
    t, d = xb.shape
    rows = t // S5_CHUNK
    lc = S5_CHUNK * S5_PAIR
    w = w_u.shape[1]
    return pl.pallas_call(
        _s5_proj_kernel, grid=(rows // tm,),
        in_specs=[pl.BlockSpec((tm * S5_CHUNK, d), lambda i: (i, 0)), pl.BlockSpec(w_u.shape, lambda i: (0, 0))],
        out_specs=pl.BlockSpec((S5_GROUPS // 2, tm, lc), lambda i: (0, i, 0)),
        out_shape=jax.ShapeDtypeStruct((S5_GROUPS // 2, rows, lc), F32),
        scratch_shapes=[pltpu.VMEM((w // LANES, tm * S5_CHUNK, LANES), F32)],
        compiler_params=_params(1), name="s5_proj")(xb, w_u)


def _s5(x, mats, d_skip, b, s):
    L, C, G = S5_CHUNK, S5_GROUP_SIZE, S5_GROUPS
    w = G * C
    n = s // L
    lc = L * S5_PAIR
    st = 4 * S5_STATE
    m_mat, p_pad, q_pad, a_l, perm = mats
    d_row = jnp.tile(d_skip.astype(F32).reshape(G // 2, 1, S5_PAIR), (1, L, 1)).reshape(G // 2, 1, lc)
    spec = lambda r, c_: pl.BlockSpec((2, 1, r, c_), lambda k: (0, k, 0, 0))
    kern = functools.partial(_s5_kernel, n_chunks=n, bsz=b)
    y = pl.pallas_call(
        kern, grid=(G // 2,),
        in_specs=[pl.BlockSpec((1, n * b, lc), lambda k: (k, 0, 0)),
                  pl.BlockSpec((2, 2, lc // 2, lc // 2), lambda k: (0, k, 0, 0)), spec(lc, st), spec(st, lc),
                  spec(1, st), pl.BlockSpec((1, 1, lc), lambda k: (k, 0, 0)),
                  pl.BlockSpec((lc, lc), lambda k: (0, 0))],
        out_specs=pl.BlockSpec((1, n * b, lc), lambda k: (k, 0, 0)),
        out_shape=jax.ShapeDtypeStruct((G // 2, n * b, lc), F32),
        scratch_shapes=[pltpu.VMEM((4, n * b, LANES), F32), pltpu.VMEM((4, n * b, LANES), F32)],
        compiler_params=_params(1), name="s5")(x, m_mat, p_pad, q_pad, a_l, d_row, perm)
    tm = 256
    return pl.pallas_call(
        _s5_unpack_kernel, grid=(b * n // tm,),
        in_specs=[pl.BlockSpec((G // 2, tm, lc), lambda i: (0, i, 0))],
        out_specs=pl.BlockSpec((tm * L, w), lambda i: (i, 0)),
        out_shape=jax.ShapeDtypeStruct((b * s, w), F32),
        scratch_shapes=[pltpu.VMEM((w // LANES, tm * L, LANES), F32)],
        compiler_params=_params(1), name="s5_unpack")(y)


def _gla_kernel(qf_ref, kf_ref, vf_ref, gf_ref, qb_ref, kb_ref, vb_ref, gb_ref, bg_ref,
                of_ref, ob_ref, st_ref, *, rows):
    c = GLA_CHUNK
    n_sub = rows // c

    @pl.when(pl.program_id(1) == 0)
    def _():
        st_ref[...] = jnp.zeros_like(st_ref)

    nt = (((1,), (1,)), ((), ()))
    tn = (((0,), (0,)), ((), ()))
    log2 = lambda n: int(math.log2(n))
    r_r = lax.broadcasted_iota(jnp.int32, (rows, rows), 0)
    c_r = lax.broadcasted_iota(jnp.int32, (rows, rows), 1)
    same_chunk = (r_r >> log2(c)) == (c_r >> log2(c))
    t_row = lax.broadcasted_iota(jnp.int32, (c, GLA_K), 0)
    t_col = lax.broadcasted_iota(jnp.int32, (c, GLA_K), 1) & (c - 1)
    k_lane = lax.broadcasted_iota(jnp.int32, (1, GLA_K), 1) >> log2(GLA_DK)
    v_lane = lax.broadcasted_iota(jnp.int32, (1, BRANCH_WIDTH), 1) >> log2(GLA_DV)
    st_same_head = ((lax.broadcasted_iota(jnp.int32, (BRANCH_WIDTH, GLA_K), 0) >> log2(GLA_DV))
                    == (lax.broadcasted_iota(jnp.int32, (BRANCH_WIDTH, GLA_K), 1) >> log2(GLA_DK)))

    for d, (q_ref, k_ref, v_ref, g_ref, o_ref) in enumerate(
            ((qf_ref, kf_ref, vf_ref, gf_ref, of_ref), (qb_ref, kb_ref, vb_ref, gb_ref, ob_ref))):
        gate = g_ref[0].astype(F32) + bg_ref[d]
        log_a = (jnp.minimum(gate, 0.0) - jnp.log(1.0 + jnp.exp(-jnp.abs(gate)))) * (1.0 / GLA_TAU)
        if d == 0:
            tri = jnp.where(same_chunk, jnp.where(r_r >= c_r, 1.0, 0.0), 0.0).astype(BF16)
            keep, i_ref, i_last = (t_row >= t_col), c // 2, c - 1
        else:
            tri = jnp.where(same_chunk, jnp.where(c_r >= r_r, 1.0, 0.0), 0.0).astype(BF16)
            keep, i_ref, i_last = (t_col > t_row), c - 1 - c // 2, 0
        la_hi = log_a.astype(BF16)
        la_lo = (log_a - la_hi.astype(F32)).astype(BF16)
        cum_all = (jnp.dot(tri, la_hi, preferred_element_type=F32) + jnp.dot(tri, la_lo, preferred_element_type=F32))
        q = q_ref[0].astype(F32) * (GLA_DK ** -0.5)
        k = k_ref[0].astype(F32)
        v = v_ref[0]
        for ci in (range(n_sub) if d == 0 else range(n_sub - 1, -1, -1)):
            sl = slice(ci * c, (ci + 1) * c)
            cum = cum_all[sl]
            mid = cum[i_ref:i_ref + 1]
            last = cum[i_last:i_last + 1]
            qc, kc, vc = q[sl], k[sl], v[sl]
            qe = (qc * jnp.exp(cum - mid)).astype(BF16)
            ke = kc * jnp.exp(mid - cum)
            kd = (kc * jnp.exp(last - cum)).astype(BF16)
            qs = (qc * jnp.exp(cum)).astype(BF16)
            decay = jnp.exp(last)
            ke4 = jnp.concatenate([jnp.where(k_lane == h, ke, 0.0) for h in range(GLA_HEADS)], axis=0).astype(BF16)
            v4 = jnp.concatenate([jnp.where(v_lane == h, vc, 0.0) for h in range(GLA_HEADS)], axis=0).astype(BF16)
            sc = lax.dot_general(qe, ke4, nt, preferred_element_type=F32)
            sc = jnp.where(keep, sc, 0.0).astype(BF16)
            st = st_ref[d]
            o = (jnp.dot(sc, v4, preferred_element_type=F32)
                 + lax.dot_general(qs, st.astype(BF16), nt, preferred_element_type=F32))
            o_ref[0, sl, :] = o
            ds = lax.dot_general(vc.astype(BF16), kd, tn, preferred_element_type=F32)
            for h in range(GLA_HEADS):
                rs = slice(h * GLA_DV, (h + 1) * GLA_DV)
                ls = slice(h * GLA_DK // LANES * LANES, h * GLA_DK // LANES * LANES + LANES)
                st_ref[d, rs, ls] = decay[:, ls] * st[rs, ls] + jnp.where(st_same_head[rs, ls], ds[rs, ls], 0.0)


def _gla(proj_a3, proj_b3, b_gate, rows=128):
    b, s, _ = proj_a3.shape
    n = s // rows
    bg = b_gate.astype(F32).reshape(2, 1, GLA_K)

    def specs(rev):
        blk = (lambda i: n - 1 - i) if rev else (lambda i: i)
        return [pl.BlockSpec((1, rows, GLA_K), lambda bi, i: (bi, blk(i), A_QC // GLA_K)),
                pl.BlockSpec((1, rows, GLA_K), lambda bi, i: (bi, blk(i), A_KC // GLA_K)),
                pl.BlockSpec((1, rows, BRANCH_WIDTH), lambda bi, i: (bi, blk(i), A_VC // BRANCH_WIDTH)),
                pl.BlockSpec((1, rows, GLA_K), lambda bi, i: (bi, blk(i), GATE_COLS // GLA_K + int(rev)))]

    out_f = pl.BlockSpec((1, rows, BRANCH_WIDTH), lambda bi, i: (bi, i, 0))
    out_b = pl.BlockSpec((1, rows, BRANCH_WIDTH), lambda bi, i: (bi, n - 1 - i, 0))
    shape = jax.ShapeDtypeStruct((b, s, BRANCH_WIDTH), F32)
    return pl.pallas_call(
        functools.partial(_gla_kernel, rows=rows), grid=(b, n),
        in_specs=specs(False) + specs(True) + [pl.BlockSpec((2, 1, GLA_K), lambda bi, i: (0, 0, 0))],
        out_specs=[out_f, out_b], out_shape=[shape, shape],
        scratch_shapes=[pltpu.VMEM((2, BRANCH_WIDTH, GLA_K), F32)],
        compiler_params=_params(2), name="gla")(
            proj_a3, proj_a3, proj_a3, proj_b3, proj_a3, proj_a3, proj_a3, proj_b3, bg)


def _ranks(vals):
    ranks = []
    for i, v in enumerate(vals):
        r = jnp.zeros(v.shape, F32)
        for o, other in enumerate(vals):
            if o != i:
                r = r + (jnp.where(other >= v, 1.0, 0.0) if o < i else jnp.where(other > v, 1.0, 0.0))
        ranks.append(r)
    return ranks


def _route(scores_t, bias_col):
    gs = EXPERTS_PER_GROUP
    s = [scores_t[e:e + 1] for e in range(N_EXPERTS)]
    biased = [s[e] + bias_col[e:e + 1] for e in range(N_EXPERTS)]
    top2, group_score = [], []
    for g in range(N_EXPERTS // gs):
        rk = _ranks(biased[g * gs:(g + 1) * gs])
        top2 += [r < 2.0 for r in rk]
        kept = [jnp.where(top2[g * gs + j], biased[g * gs + j], 0.0) for j in range(gs)]
        group_score.append(functools.reduce(lambda a, b: a + b, kept))
    best = [r < 1.0 for r in _ranks(group_score)]
    w = [jnp.where(top2[e], jnp.where(best[e // gs], s[e], 0.0), 0.0) for e in range(N_EXPERTS)]
    total = functools.reduce(lambda a, b: a + b, w)
    return [we / total for we in w]


def _merge_kernel(x_ref, oa_ref, ys_ref, gf_ref, gb_ref, r_ref, gl0_ref, gl1_ref, gl2_ref,
                  wglu_ref, bglu_ref, gng_ref, wup_ref, mb_ref, wout_ref, lng_ref, lnb_ref, rw_ref, rb_ref,
                  xo_ref, xob_ref, comb_ref):
    y = ys_ref[...]
    gelu = 0.5 * y * (1.0 + jnp.tanh(math.sqrt(2.0 / math.pi) * (y + 0.044715 * (y * y * y))))
    o_b = gelu * _sigmoid(jnp.dot(gelu.astype(BF16), wglu_ref[...], preferred_element_type=F32) + bglu_ref[...])

    r = r_ref[...].astype(F32)
    gated = r * _sigmoid(r)
    heads = []
    for h in range(GLA_HEADS):
        sl = slice(h * GLA_DV, (h + 1) * GLA_DV)
        o = gf_ref[:, sl] + gb_ref[:, sl]
        ms = jnp.mean(o * o, axis=-1, keepdims=True)
        heads.append(o * lax.rsqrt(ms + NORM_EPS) * gng_ref[...] * gated[:, sl])
    o_c = jnp.concatenate(heads, axis=-1)

    merged = None
    for n, (o_n, gl_ref) in enumerate(((oa_ref[...], gl0_ref), (o_b, gl1_ref), (o_c, gl2_ref))):
        gate = _sigmoid(gl_ref[...].astype(F32) + mb_ref[n])
        term = gate * jnp.dot(o_n.astype(BF16), wup_ref[n], preferred_element_type=F32)
        merged = term if merged is None else merged + term
    mix = jnp.dot(merged.astype(BF16), wout_ref[...], preferred_element_type=F32)
    x1 = _layer_norm(DN_ALPHA * x_ref[...] + mix, lng_ref[...], lnb_ref[...])
    xo_ref[...] = x1
    x_hi = x1.astype(BF16)
    xob_ref[...] = x_hi

    x_lo = (x1 - x_hi.astype(F32)).astype(BF16)
    logits = (jnp.dot(x_hi, rw_ref[0], preferred_element_type=F32) + jnp.dot(x_lo, rw_ref[0], preferred_element_type=F32)
              + jnp.dot(x_hi, rw_ref[1], preferred_element_type=F32))
    rows = _route(_sigmoid(logits.T[:N_EXPERTS]), rb_ref[...])
    comb_ref[...] = jnp.zeros(comb_ref.shape, F32)
    for e in range(N_EXPERTS):
        comb_ref[e:e + 1, :] = rows[e]


def _merge(x, o_a, y_s5, g_f, g_b, proj_a, proj_b, wglu, bglu, gng, wup, mb, wout, lng, lnb, rw, rb, tm=256):
    t, d = x.shape
    w = BRANCH_WIDTH
    row = lambda width, cb=0: pl.BlockSpec((tm, width), lambda i: (i, cb))
    full = lambda a: pl.BlockSpec(a.shape, lambda i: (0,) * a.ndim)
    weights = (wglu, bglu, gng, wup, mb, wout, lng, lnb, rw, rb)
    return pl.pallas_call(
        _merge_kernel, grid=(t // tm,),
        in_specs=[row(d), row(w), row(w), row(w), row(w), row(w, A_RC // w), row(d, 0), row(d, 1), row(d, 2)]
        + [full(a) for a in weights],
        out_specs=[row(d), row(d), pl.BlockSpec((LANES, tm), lambda i: (0, i))],
        out_shape=[jax.ShapeDtypeStruct((t, d), F32), jax.ShapeDtypeStruct((t, d), BF16),
                   jax.ShapeDtypeStruct((LANES, t), F32)],
        compiler_params=_params(1), name="merge")(
            x, o_a, y_s5, g_f, g_b, proj_a, proj_b, proj_b, proj_b, *weights)


MOE_SUB = 256
MOE_CAP = 64


MOE_RING = 3


def _swiglu(x, wg_ref, wu_ref, wd_ref, row_scale):
    g = jnp.dot(x, wg_ref[...], preferred_element_type=F32)
    u = jnp.dot(x, wu_ref[...], preferred_element_type=F32)
    h = (g * _sigmoid(g)) * u * row_scale
    return jnp.dot(h.astype(BF16), wd_ref[...], preferred_element_type=F32)


def _moe_kernel(xb_ref, xf_ref, comb_ref, wg_hbm, wu_hbm, wd_hbm, lng_ref, lnb_ref, of_ref, ob_ref,
                pos_ref, ind_ref, sel_ref, xg_ref, y_ref, cg_ref, acc_ref, flag_ref,
                wg_buf, wu_buf, wd_buf, w_sem, *, tm, cap):
    e = pl.program_id(1)
    ns = tm // MOE_SUB
    gs = EXPERTS_PER_GROUP
    sub = lambda k: slice(k * MOE_SUB, (k + 1) * MOE_SUB)
    slots = lambda k: slice(k * cap, (k + 1) * cap)

    step = pl.program_id(0) * N_EXPERTS + e
    n_steps = pl.num_programs(0) * N_EXPERTS

    def weight_copies(st):
        slot = st % MOE_RING
        ex = st % N_EXPERTS
        return [pltpu.make_async_copy(hbm.at[ex], buf.at[slot], w_sem.at[k, slot])
                for k, (hbm, buf) in enumerate(((wg_hbm, wg_buf), (wu_hbm, wu_buf), (wd_hbm, wd_buf)))]

    @pl.when(step == 0)
    def _():
        for st in range(MOE_RING - 1):
            for cp in weight_copies(st):
                cp.start()

    @pl.when(step + (MOE_RING - 1) < n_steps)
    def _():
        for cp in weight_copies(step + (MOE_RING - 1)):
            cp.start()

    for cp in weight_copies(step):
        cp.wait()
    cur = step % MOE_RING
    wg_ref, wu_ref, wd_ref = wg_buf.at[cur], wu_buf.at[cur], wd_buf.at[cur]

    r_i = lax.broadcasted_iota(jnp.int32, (MOE_SUB, MOE_SUB), 0)
    c_i = lax.broadcasted_iota(jnp.int32, (MOE_SUB, MOE_SUB), 1)

    @pl.when(e == 0)
    def _():
        ind = jnp.where(comb_ref[...] > 0.0, 1.0, 0.0)
        ind_ref[...] = ind
        before = jnp.where(r_i < c_i, 1.0, 0.0).astype(BF16)
        most = jnp.zeros((LANES, 1), F32)
        for k in range(ns):
            pos_ref[:, sub(k)] = jnp.dot(ind[:, sub(k)].astype(BF16), before, preferred_element_type=F32)
            most = jnp.maximum(most, jnp.sum(ind[:, sub(k)], axis=-1, keepdims=True))
        flag_ref[0] = (jnp.max(most) > cap).astype(jnp.int32)
        acc_ref[...] = jnp.zeros_like(acc_ref)

    slot_id = lax.broadcasted_iota(jnp.int32, (cap, 1), 0).astype(F32)
    for grp in range(N_EXPERTS // gs):
        @pl.when(e == grp * gs)
        def _(grp=grp):
            for k in range(ns):
                pieces = []
                for j in range(gs):
                    ee = grp * gs + j
                    one_hot = jnp.where(pos_ref[ee:ee + 1, sub(k)] == slot_id, ind_ref[ee:ee + 1, sub(k)], 0.0)
                    cg_ref[j, slots(k), :] = jnp.sum(one_hot * comb_ref[ee:ee + 1, sub(k)], axis=-1, keepdims=True)
                    pieces.append(one_hot)
                sel = jnp.concatenate(pieces, axis=0).astype(BF16)
                sel_ref[k] = sel
                xg = jnp.dot(sel, xb_ref[sub(k), :], preferred_element_type=F32).astype(BF16)
                for j in range(gs):
                    xg_ref[j, slots(k), :] = xg[j * cap:(j + 1) * cap]

    j = e % gs
    y_ref[j] = _swiglu(xg_ref[j], wg_ref, wu_ref, wd_ref, cg_ref[j]).astype(BF16)

    @pl.when(j == gs - 1)
    def _():
        for k in range(ns):
            yk = jnp.concatenate([y_ref[jj, slots(k), :] for jj in range(gs)], axis=0)
            acc_ref[sub(k), :] += lax.dot_general(sel_ref[k], yk, (((0,), (0,)), ((), ())),
                                                  preferred_element_type=F32)

    @pl.when(flag_ref[0] > 0)
    def _():
        comb = comb_ref[...].T
        earlier = jnp.where(r_i > c_i, 1.0, 0.0).astype(BF16)
        over = []
        for k in range(ns):
            pos = jnp.dot(earlier, jnp.where(comb[sub(k)] > 0.0, 1.0, 0.0).astype(BF16), preferred_element_type=F32)
            over.append(jnp.where(pos >= cap, comb[sub(k)], 0.0))
        over = jnp.concatenate(over, axis=0)
        lane = lax.broadcasted_iota(jnp.int32, over.shape, 1)
        ce = jnp.sum(jnp.where(lane == e, over, 0.0), axis=-1, keepdims=True)
        acc_ref[...] += _swiglu(xb_ref[...], wg_ref, wu_ref, wd_ref, ce)

    @pl.when(e == N_EXPERTS - 1)
    def _():
        y = _layer_norm(DN_ALPHA * xf_ref[...] + acc_ref[...], lng_ref[...], lnb_ref[...])
        of_ref[...] = y
        ob_ref[...] = y.astype(BF16)


def _moe(xb, xf, comb, wg, wu, wd, lng, lnb, tm=1024, cap=MOE_CAP):
    t, d = xf.shape
    ns = tm // MOE_SUB
    gs = EXPERTS_PER_GROUP
    row = lambda width: pl.BlockSpec((tm, width), lambda i, e: (i, 0))
    vec = pl.BlockSpec((1, d), lambda i, e: (0, 0))
    return pl.pallas_call(
        functools.partial(_moe_kernel, tm=tm, cap=cap), grid=(t // tm, N_EXPERTS),
        in_specs=[row(d), row(d), pl.BlockSpec((LANES, tm), lambda i, e: (0, i)),
                  pl.BlockSpec(memory_space=pl.ANY), pl.BlockSpec(memory_space=pl.ANY),
                  pl.BlockSpec(memory_space=pl.ANY), vec, vec],
        out_specs=[row(d), row(d)],
        out_shape=[jax.ShapeDtypeStruct((t, d), F32), jax.ShapeDtypeStruct((t, d), BF16)],
        scratch_shapes=[pltpu.VMEM((LANES, tm), F32), pltpu.VMEM((LANES, tm), F32),
                        pltpu.VMEM((ns, gs * cap, MOE_SUB), BF16), pltpu.VMEM((gs, ns * cap, d), BF16),
                        pltpu.VMEM((gs, ns * cap, d), BF16), pltpu.VMEM((gs, ns * cap, 1), F32),
                        pltpu.VMEM((tm, d), F32), pltpu.SMEM((1,), jnp.int32),
                        pltpu.VMEM((MOE_RING, d, EXPERT_FF), BF16), pltpu.VMEM((MOE_RING, d, EXPERT_FF), BF16),
                        pltpu.VMEM((MOE_RING, EXPERT_FF, d), BF16), pltpu.SemaphoreType.DMA((3, MOE_RING))],
        compiler_params=_params(2), name="moe")(xb, xf, comb, wg, wu, wd, lng, lnb)


def _proj_b_weights(w_in_l, w_gate):
    z0 = W_IN_Z
    folded = [jnp.dot(w_in_l[:, z0 + i * GLA_GATE_RANK:z0 + (i + 1) * GLA_GATE_RANK].astype(F32),
                      w_gate[i].astype(F32), precision=HIGHEST) for i in range(2)]
    return jnp.concatenate([w_in_l[:, z0 + 2 * GLA_GATE_RANK:].astype(F32)] + folded, axis=1).astype(BF16)


def kernel(x, ln0_g, ln0_b, w_in, da_lambda, da_norm_g, s5_a_re, s5_a_im, s5_log_dt, s5_b_re, s5_b_im,
           s5_c_re, s5_c_im, s5_d, s5_w_glu, s5_b_glu, gla_w_gate, gla_b_gate, gla_norm_g, merge_w_up,
           merge_b, w_out, ln1_g, ln1_b, router_w, router_bias, moe_w_gate, moe_w_up, moe_w_down,
           ln2_g, ln2_b):
    b, s, d = x.shape
    t = b * s
    vec = lambda a: a.astype(F32).reshape(1, -1)
    xf, xb = _ln0(x.reshape(t, d), ln0_g.astype(F32), ln0_b.astype(F32))
    rw32 = jnp.zeros((d, LANES), F32).at[:, :N_EXPERTS].set(router_w.astype(F32))
    rw_hi = rw32.astype(BF16)
    rw = jnp.stack([rw_hi, (rw32 - rw_hi.astype(F32)).astype(BF16)])
    rb = router_bias.astype(F32).reshape(N_EXPERTS, 1)
    for l in range(DEPTH):
        w_a = jnp.concatenate([w_in[l][:, :W_IN_U], w_in[l][:, W_IN_GLA:W_IN_Z]], axis=1).astype(BF16)
        w_u = w_in[l][:, W_IN_U:W_IN_GLA].astype(BF16)
        w_b = _proj_b_weights(w_in[l], gla_w_gate[l])
        proj_a = _matmul(xb, w_a, 1024, PROJ_A // 2, "in_proj_a")
        proj_b = _matmul(xb, w_b, 512, PROJ_B, "in_proj_b")
        proj_a3 = proj_a.reshape(b, s, PROJ_A)
        proj_b3 = proj_b.reshape(b, s, PROJ_B)

        lam_init = 0.8 - 0.6 * math.exp(-0.3 * l)
        o_a = _diff_attention(proj_a3, da_lambda[l].astype(F32), da_norm_g[l].astype(F32), lam_init)
        mats = _s5_matrices(s5_a_re[l], s5_a_im[l], s5_log_dt[l], s5_b_re[l], s5_b_im[l], s5_c_re[l], s5_c_im[l])
        y_s5 = _s5(_s5_proj(xb, w_u), mats, s5_d[l], b, s)
        g_f, g_b = _gla(proj_a3, proj_b3, gla_b_gate[l])

        xf, xb, comb = _merge(
            xf, o_a.reshape(t, BRANCH_WIDTH), y_s5, g_f.reshape(t, BRANCH_WIDTH), g_b.reshape(t, BRANCH_WIDTH),
            proj_a, proj_b, s5_w_glu[l].astype(BF16), vec(s5_b_glu[l]), vec(gla_norm_g[l]),
            merge_w_up[l].astype(BF16), merge_b[l].astype(F32).reshape(N_BRANCH, 1, d), w_out[l].astype(BF16),
            vec(ln1_g[l]), vec(ln1_b[l]), rw, rb)
        xf, xb = _moe(xb, xf, comb, moe_w_gate[l].astype(BF16), moe_w_up[l].astype(BF16),
                      moe_w_down[l].astype(BF16), vec(ln2_g[l]), vec(ln2_b[l]))
    return xf.reshape(b, s, d)
```
